```python
import jax, jax.numpy as jnp
from jax import lax
import numpy as np

D_MODEL = 1024
BATCH = 8
SEQ = 2048
DEPTH = 1
DEC_BATCH = 128
DEC_SEQ = 8
PAST_LEN = 16384
PAGE_SIZE = 128

POOL_WINDOWS = (2, 4, 8, 16)
N_POOL_GROUPS = len(POOL_WINDOWS)
POOL_WIDTH = D_MODEL // 2
POOL_GROUP = POOL_WIDTH // N_POOL_GROUPS
POOL_HIST = max(POOL_WINDOWS) - 1
CONV_WIDTH = D_MODEL // 2
CONV_K = 3
CONV_HIST = CONV_K - 1
D_FF = -(-8 * D_MODEL // (3 * 256)) * 256
D_PLE = 256
EPS = 1e-6
IN_WIDTH = POOL_WIDTH + 3 * CONV_WIDTH + 2 * D_MODEL

kernel_name = "pool_shortconv_gated_hybrid_step"


def rmsnorm(x, g):
    xf = x.astype(jnp.float32)
    y = xf * lax.rsqrt(jnp.mean(xf * xf, axis=-1, keepdims=True) + EPS)
    return (y * g.astype(jnp.float32)).astype(x.dtype)


def pool_branch(u, prefix, n_valid, w_group, scale):
    B, T, _ = u.shape
    ext = jnp.concatenate([prefix.astype(u.dtype), u], axis=1)
    cs = jnp.cumsum(ext.astype(jnp.float32), axis=1)
    cs0 = jnp.concatenate([jnp.zeros((B, 1, POOL_WIDTH), jnp.float32), cs], axis=1)
    t = jnp.arange(T)
    H = POOL_HIST
    means = []
    for gi, w in enumerate(POOL_WINDOWS):
        c0, c1 = gi * POOL_GROUP, (gi + 1) * POOL_GROUP
        win = cs0[:, H + 1:H + 1 + T, c0:c1] - cs0[:, H + 1 - w:H + 1 - w + T, c0:c1]
        cnt = jnp.minimum(w, t + 1 + n_valid).astype(jnp.float32)[None, :, None]
        means.append(win / cnt)
    d = (jnp.concatenate(means, axis=-1) - u.astype(jnp.float32)).astype(u.dtype)
    d = d.reshape(B, T, N_POOL_GROUPS, POOL_GROUP)
    mixed = jnp.einsum('btgc,gcd->btgd', d, w_group).reshape(B, T, POOL_WIDTH) * scale
    return mixed, ext[:, -POOL_HIST:]


def conv_branch(b, c, h, prefix, w_conv):
    T = b.shape[1]
    v = c * h
    ext = jnp.concatenate([prefix.astype(v.dtype), v], axis=1)
    y = sum(w_conv[k] * ext[:, k:k + T] for k in range(CONV_K))
    return b * y, ext[:, -CONV_HIST:]


def layer(x, p_l, pool_prefix, pool_valid, conv_prefix,
          g_mix, w_in, w_pool_group, pool_scale, w_pool_up, w_conv, w_conv_out, w_o,
          g_ffn, w_ffn_in, w_ffn_out, g_ple, w_ple, w_ple_gate):
    hn = rmsnorm(x, g_mix)
    z = hn @ w_in
    o = 0
    u = z[..., o:o + POOL_WIDTH]; o += POOL_WIDTH
    bg = z[..., o:o + CONV_WIDTH]; o += CONV_WIDTH
    cg = z[..., o:o + CONV_WIDTH]; o += CONV_WIDTH
    hv = z[..., o:o + CONV_WIDTH]; o += CONV_WIDTH
    gp = z[..., o:o + D_MODEL]; o += D_MODEL
    gc = z[..., o:o + D_MODEL]
    pool_out, pool_state = pool_branch(u, pool_prefix, pool_valid, w_pool_group, pool_scale)
    conv_out, conv_state = conv_branch(bg, cg, hv, conv_prefix, w_conv)
    merged = jax.nn.sigmoid(gp) * (pool_out @ w_pool_up) + jax.nn.sigmoid(gc) * (conv_out @ w_conv_out)
    x = x + merged @ w_o
    hn = rmsnorm(x, g_ffn)
    gu = hn @ w_ffn_in
    x = x + (jax.nn.silu(gu[..., :D_FF]) * gu[..., D_FF:]) @ w_ffn_out
    x = x + jax.nn.sigmoid(rmsnorm(x, g_ple) @ w_ple_gate) * (p_l @ w_ple)
    return x, pool_state, conv_state


def setup_inputs(seed: int = 0) -> dict:
    key = jax.random.key(seed)
    ks = jax.random.split(key, 24)
    nrm = lambda k, s, sc: jax.random.normal(k, s, jnp.float32) * sc
    gain = lambda k, s: 1.0 + 0.05 * jax.random.normal(k, s, jnp.float32)
    return {
        "x_prompt": nrm(ks[0], (BATCH, SEQ, D_MODEL), 1.0),
        "x_sample": nrm(ks[1], (DEC_BATCH, DEC_SEQ, D_MODEL), 1.0),
        "state_pool": nrm(ks[2], (DEPTH, DEC_BATCH, POOL_HIST, POOL_WIDTH), 1.0),
        "state_conv": nrm(ks[3], (DEPTH, DEC_BATCH, CONV_HIST, CONV_WIDTH), 1.0),
        "p_prompt": nrm(ks[4], (DEPTH, BATCH, SEQ, D_PLE), 1.0),
        "p_sample": nrm(ks[5], (DEPTH, DEC_BATCH, DEC_SEQ, D_PLE), 1.0),
        "g_mix": gain(ks[6], (DEPTH, D_MODEL)),
        "w_in": nrm(ks[7], (DEPTH, D_MODEL, IN_WIDTH), D_MODEL ** -0.5),
        "w_pool_group": nrm(ks[8], (DEPTH, N_POOL_GROUPS, POOL_GROUP, POOL_GROUP), POOL_GROUP ** -0.5),
        "pool_scale": gain(ks[9], (DEPTH, POOL_WIDTH)),
        "w_pool_up": nrm(ks[10], (DEPTH, POOL_WIDTH, D_MODEL), POOL_WIDTH ** -0.5),
        "w_conv": nrm(ks[11], (DEPTH, CONV_K, CONV_WIDTH), CONV_K ** -0.5),
        "w_conv_out": nrm(ks[12], (DEPTH, CONV_WIDTH, D_MODEL), CONV_WIDTH ** -0.5),
        "w_o": nrm(ks[13], (DEPTH, D_MODEL, D_MODEL), D_MODEL ** -0.5),
        "g_ffn": gain(ks[14], (DEPTH, D_MODEL)),
        "w_ffn_in": nrm(ks[15], (DEPTH, D_MODEL, 2 * D_FF), D_MODEL ** -0.5),
        "w_ffn_out": nrm(ks[16], (DEPTH, D_FF, D_MODEL), D_FF ** -0.5),
        "g_ple": gain(ks[17], (DEPTH, D_MODEL)),
        "w_ple": nrm(ks[18], (DEPTH, D_PLE, D_MODEL), D_PLE ** -0.5),
        "w_ple_gate": nrm(ks[19], (DEPTH, D_MODEL, D_MODEL), D_MODEL ** -0.5),
        "g_final": gain(ks[20], (D_MODEL,)),
    }


def reference(x_prompt, x_sample, state_pool, state_conv, p_prompt, p_sample,
              g_mix, w_in, w_pool_group, pool_scale, w_pool_up, w_conv, w_conv_out, w_o,
              g_ffn, w_ffn_in, w_ffn_out, g_ple, w_ple, w_ple_gate, g_final):
    xp, xs = x_prompt, x_sample
    sample_pool_valid = min(PAST_LEN, POOL_HIST)
    pool_p, conv_p, pool_s, conv_s = [], [], [], []
    for i in range(DEPTH):
        wts = (g_mix[i], w_in[i], w_pool_group[i], pool_scale[i], w_pool_up[i], w_conv[i],
               w_conv_out[i], w_o[i], g_ffn[i], w_ffn_in[i], w_ffn_out[i], g_ple[i],
               w_ple[i], w_ple_gate[i])
        zp_pool = jnp.zeros((xp.shape[0], POOL_HIST, POOL_WIDTH), xp.dtype)
        zp_conv = jnp.zeros((xp.shape[0], CONV_HIST, CONV_WIDTH), xp.dtype)
        xp, sp, sc = layer(xp, p_prompt[i], zp_pool, 0, zp_conv, *wts)
        pool_p.append(sp); conv_p.append(sc)
        xs, sp, sc = layer(xs, p_sample[i], state_pool[i], sample_pool_valid, state_conv[i], *wts)
        pool_s.append(sp); conv_s.append(sc)
    y_prompt = rmsnorm(xp, g_final)
    y_sample = rmsnorm(xs, g_final)
    new_pool_prompt = jnp.stack(pool_p, axis=0)
    new_conv_prompt = jnp.stack(conv_p, axis=0)
    new_pool_sample = jnp.stack(pool_s, axis=0)
    new_conv_sample = jnp.stack(conv_s, axis=0)
    return (y_prompt, y_sample, new_pool_prompt, new_conv_prompt, new_pool_sample, new_conv_sample)
```

```python
import functools

import jax
import jax.numpy as jnp
from jax.experimental import pallas as pl
from jax.experimental.pallas import tpu as pltpu

D_MODEL = 1024
POOL_WINDOWS = (2, 4, 8, 16)
POOL_GROUP = 128
POOL_WIDTH = 512
POOL_HIST = 15
CONV_WIDTH = 512
CONV_K = 3
CONV_HIST = 2
D_FF = 2816
D_PLE = 256
EPS = 1e-6

OFF_U = 0
OFF_B = OFF_U + POOL_WIDTH
OFF_C = OFF_B + CONV_WIDTH
OFF_H = OFF_C + CONV_WIDTH
OFF_GP = OFF_H + CONV_WIDTH
OFF_GC = OFF_GP + D_MODEL

SUBLANES = 8
POOL_PAD = 16
CONV_PAD = 8
FF_CHUNK = 256
TILE_PROMPT = 256
TILE_SAMPLE = 256
VMEM_LIMIT_BYTES = 56 * 1024 * 1024


def _rmsnorm(x, g):
    ms = jnp.mean(x * x, axis=-1, keepdims=True)
    return x * jax.lax.rsqrt(ms + EPS) * g


def _dot(a, b):
    return jnp.dot(a, b, preferred_element_type=jnp.float32)


def _layer_kernel(x_ref, p_ref, pool_hist_ref, conv_hist_ref,
                  g_mix_ref, w_in_ref, w_group_ref, pool_scale_ref, w_pool_up_ref,
                  w_conv_ref, w_conv_out_ref, w_o_ref, g_ffn_ref, w_ffn_in_ref,
                  w_ffn_out_ref, g_ple_ref, w_ple_ref, w_ple_gate_ref, g_final_ref,
                  y_ref, u_out_ref, v_out_ref,
                  ext_ref, vext_ref, act_ref, *, tile, tiles_per_seq, sample):
    bf16 = jnp.bfloat16
    if not sample:
        @pl.when(pl.program_id(0) % tiles_per_seq == 0)
        def _():
            ext_ref[:POOL_PAD, :] = jnp.zeros((POOL_PAD, POOL_WIDTH), jnp.float32)
            vext_ref[:CONV_PAD, :] = jnp.zeros((CONV_PAD, CONV_WIDTH), jnp.float32)

    x = x_ref[...]
    hn = _rmsnorm(x, g_mix_ref[...]).astype(bf16)

    def proj(off, width):
        return _dot(hn, w_in_ref[:, off:off + width])

    u = proj(OFF_U, POOL_WIDTH)
    v = proj(OFF_C, CONV_WIDTH) * proj(OFF_H, CONV_WIDTH)

    if sample:
        seqs = tile // SUBLANES
        ext_ref[:, :POOL_PAD, :] = pool_hist_ref[...]
        ext_ref[:, POOL_PAD:, :] = u.reshape(seqs, SUBLANES, POOL_WIDTH)
        vext_ref[:, :CONV_PAD, :] = conv_hist_ref[...]
        vext_ref[:, CONV_PAD:, :] = v.reshape(seqs, SUBLANES, CONV_WIDTH)

        def ext_rows(back, c0, c1):
            r = ext_ref[:, POOL_PAD - back:POOL_PAD - back + SUBLANES, c0:c1]
            return r.reshape(tile, c1 - c0)

        def vext_rows(back):
            r = vext_ref[:, CONV_PAD - back:CONV_PAD - back + SUBLANES, :]
            return r.reshape(tile, CONV_WIDTH)
    else:
        ext_ref[POOL_PAD:, :] = u
        vext_ref[CONV_PAD:, :] = v

        def ext_rows(back, c0, c1):
            return ext_ref[POOL_PAD - back:POOL_PAD - back + tile, c0:c1]

        def vext_rows(back):
            return vext_ref[CONV_PAD - back:CONV_PAD - back + tile, :]

    if sample:
        valid = None
    else:
        pos = (pl.program_id(0) % tiles_per_seq) * tile + jax.lax.broadcasted_iota(
            jnp.int32, (tile, POOL_GROUP), 0)
        valid = (pos + 1).astype(jnp.float32)
    mixed = []
    for gi, w in enumerate(POOL_WINDOWS):
        c0, c1 = gi * POOL_GROUP, (gi + 1) * POOL_GROUP
        win = ext_rows(0, c0, c1)
        for back in range(1, w):
            win = win + ext_rows(back, c0, c1)
        if valid is None:
            mean = win * (1.0 / w)
        else:
            mean = win / jnp.minimum(valid, float(w))
        d = (mean - ext_rows(0, c0, c1)).astype(bf16)
        mixed.append(_dot(d, w_group_ref[gi]))
    pool_out = (jnp.concatenate(mixed, axis=-1) * pool_scale_ref[...]).astype(bf16)
    pool_up = _dot(pool_out, w_pool_up_ref[...])

    wc = w_conv_ref[...]
    yconv = wc[0:1] * vext_rows(2) + wc[1:2] * vext_rows(1) + wc[2:3] * vext_rows(0)
    conv_out = (proj(OFF_B, CONV_WIDTH) * yconv).astype(bf16)
    conv_up = _dot(conv_out, w_conv_out_ref[...])

    if sample:
        u_out_ref[...] = u
        v_out_ref[...] = v
    else:
        u_tail = ext_ref[tile:tile + POOL_PAD, :]
        v_tail = vext_ref[tile:tile + CONV_PAD, :]
        ext_ref[:POOL_PAD, :] = u_tail
        vext_ref[:CONV_PAD, :] = v_tail
        u_out_ref[0] = u_tail
        v_out_ref[0] = v_tail

    merged = (jax.nn.sigmoid(proj(OFF_GP, D_MODEL)) * pool_up
              + jax.nn.sigmoid(proj(OFF_GC, D_MODEL)) * conv_up).astype(bf16)
    x = x + _dot(merged, w_o_ref[...])

    hn = _rmsnorm(x, g_ffn_ref[...]).astype(bf16)
    for c0 in range(0, D_FF, FF_CHUNK):
        gate = _dot(hn, w_ffn_in_ref[:, c0:c0 + FF_CHUNK])
        up = _dot(hn, w_ffn_in_ref[:, D_FF + c0:D_FF + c0 + FF_CHUNK])
        act_ref[:, c0:c0 + FF_CHUNK] = (jax.nn.silu(gate) * up).astype(bf16)
    x = x + _dot(act_ref[...], w_ffn_out_ref[...])

    hn = _rmsnorm(x, g_ple_ref[...]).astype(bf16)
    ple_gate = jax.nn.sigmoid(_dot(hn, w_ple_gate_ref[...]))
    x = x + ple_gate * _dot(p_ref[...].astype(bf16), w_ple_ref[...])

    y_ref[...] = _rmsnorm(x, g_final_ref[...])


def _resident(shape):
    zeros = (0,) * len(shape)
    return pl.BlockSpec(shape, lambda i: zeros, pipeline_mode=pl.Buffered(1))


def _run_group(x, p, pool_hist, conv_hist, weights, *, tile, seq_len, sample):
    tokens = x.shape[0]
    assert tokens % tile == 0 and tile % SUBLANES == 0
    n_tiles = tokens // tile
    row = lambda width: pl.BlockSpec((tile, width), lambda i: (i, 0))
    if sample:
        assert seq_len == SUBLANES
        seqs = tile // SUBLANES
        tiles_per_seq = 1
        hist_specs = [
            pl.BlockSpec((seqs, POOL_PAD, POOL_WIDTH), lambda i: (i, 0, 0)),
            pl.BlockSpec((seqs, CONV_PAD, CONV_WIDTH), lambda i: (i, 0, 0)),
        ]
        state_specs = [row(POOL_WIDTH), row(CONV_WIDTH)]
        state_shapes = [jax.ShapeDtypeStruct((tokens, POOL_WIDTH), jnp.float32),
                        jax.ShapeDtypeStruct((tokens, CONV_WIDTH), jnp.float32)]
        scratch = [pltpu.VMEM((seqs, POOL_PAD + SUBLANES, POOL_WIDTH), jnp.float32),
                   pltpu.VMEM((seqs, CONV_PAD + SUBLANES, CONV_WIDTH), jnp.float32)]
    else:
        assert seq_len % tile == 0 and tile >= POOL_PAD
        tiles_per_seq = seq_len // tile
        n_seq = tokens // seq_len
        hist_specs = [_resident(pool_hist.shape), _resident(conv_hist.shape)]
        state_specs = [
            pl.BlockSpec((1, POOL_PAD, POOL_WIDTH), lambda i: (i // tiles_per_seq, 0, 0)),
            pl.BlockSpec((1, CONV_PAD, CONV_WIDTH), lambda i: (i // tiles_per_seq, 0, 0)),
        ]
        state_shapes = [jax.ShapeDtypeStruct((n_seq, POOL_PAD, POOL_WIDTH), jnp.float32),
                        jax.ShapeDtypeStruct((n_seq, CONV_PAD, CONV_WIDTH), jnp.float32)]
        scratch = [pltpu.VMEM((POOL_PAD + tile, POOL_WIDTH), jnp.float32),
                   pltpu.VMEM((CONV_PAD + tile, CONV_WIDTH), jnp.float32)]
    scratch.append(pltpu.VMEM((tile, D_FF), jnp.bfloat16))

    kernel = functools.partial(_layer_kernel, tile=tile, tiles_per_seq=tiles_per_seq,
                               sample=sample)
    return pl.pallas_call(
        kernel,
        grid=(n_tiles,),
        in_specs=[row(D_MODEL), row(D_PLE)] + hist_specs + [_resident(w.shape) for w in weights],
        out_specs=[row(D_MODEL)] + state_specs,
        out_shape=[jax.ShapeDtypeStruct((tokens, D_MODEL), jnp.float32)] + state_shapes,
        scratch_shapes=scratch,
        compiler_params=pltpu.CompilerParams(
            dimension_semantics=("arbitrary",),
            vmem_limit_bytes=VMEM_LIMIT_BYTES),
        name="hybrid_layer_sample" if sample else "hybrid_layer_prompt",
    )(x, p, pool_hist, conv_hist, *weights)


def kernel(x_prompt, x_sample, state_pool, state_conv, p_prompt, p_sample, g_mix, w_in, w_pool_group, pool_scale, w_pool_up, w_conv, w_conv_out, w_o, g_ffn, w_ffn_in, w_ffn_out, g_ple, w_ple, w_ple_gate, g_final):
    depth = g_mix.shape[0]
    assert depth == 1, "single-layer step"
    bf16 = jnp.bfloat16
    vec = lambda a: a.reshape(1, -1)
    weights = (
        vec(g_mix[0]), w_in[0].astype(bf16), w_pool_group[0].astype(bf16), vec(pool_scale[0]),
        w_pool_up[0].astype(bf16), w_conv[0], w_conv_out[0].astype(bf16), w_o[0].astype(bf16),
        vec(g_ffn[0]), w_ffn_in[0].astype(bf16), w_ffn_out[0].astype(bf16), vec(g_ple[0]),
        w_ple[0].astype(bf16), w_ple_gate[0].astype(bf16), vec(g_final),
    )

    batch, seq, _ = x_prompt.shape
    dec_batch, dec_seq, _ = x_sample.shape

    dummy_pool = jnp.zeros((SUBLANES, POOL_GROUP), jnp.float32)
    dummy_conv = jnp.zeros((SUBLANES, POOL_GROUP), jnp.float32)
    y_p, u_tail, v_tail = _run_group(
        x_prompt.reshape(batch * seq, D_MODEL), p_prompt[0].reshape(batch * seq, D_PLE),
        dummy_pool, dummy_conv, weights, tile=TILE_PROMPT, seq_len=seq, sample=False)
    new_pool_prompt = u_tail[:, POOL_PAD - POOL_HIST:]
    new_conv_prompt = v_tail[:, CONV_PAD - CONV_HIST:]

    pool_hist = jnp.pad(state_pool[0], ((0, 0), (POOL_PAD - POOL_HIST, 0), (0, 0)))
    conv_hist = jnp.pad(state_conv[0], ((0, 0), (CONV_PAD - CONV_HIST, 0), (0, 0)))
    y_s, u_s, v_s = _run_group(
        x_sample.reshape(dec_batch * dec_seq, D_MODEL),
        p_sample[0].reshape(dec_batch * dec_seq, D_PLE),
        pool_hist, conv_hist, weights, tile=TILE_SAMPLE, seq_len=dec_seq, sample=True)
    u_s = u_s.reshape(dec_batch, dec_seq, POOL_WIDTH)
    v_s = v_s.reshape(dec_batch, dec_seq, CONV_WIDTH)
    new_pool_sample = jnp.concatenate([state_pool[0], u_s], axis=1)[:, -POOL_HIST:]
    new_conv_sample = jnp.concatenate([state_conv[0], v_s], axis=1)[:, -CONV_HIST:]

    return (y_p.reshape(batch, seq, D_MODEL), y_s.reshape(dec_batch, dec_seq, D_MODEL),
            new_pool_prompt[None], new_conv_prompt[None],
            new_pool_sample[None], new_conv_sample[None])
```

```python
import functools

import jax
import jax.numpy as jnp
from jax.experimental import pallas as pl
from jax.experimental.pallas import tpu as pltpu

D_MODEL = 1024
POOL_WINDOWS = (2, 4, 8, 16)
POOL_GROUP = 128
POOL_WIDTH = 512
POOL_HIST = 15
CONV_WIDTH = 512
CONV_K = 3
CONV_HIST = 2
D_FF = 2816
D_PLE = 256
EPS = 1e-6

OFF_U = 0
OFF_B = OFF_U + POOL_WIDTH
OFF_C = OFF_B + CONV_WIDTH
OFF_H = OFF_C + CONV_WIDTH
OFF_GP = OFF_H + CONV_WIDTH
OFF_GC = OFF_GP + D_MODEL

SUBLANES = 8
POOL_PAD = 16
CONV_PAD = 8
FF_CHUNK = 256
TILE_PROMPT = 512
TILE_SAMPLE = 256
VMEM_LIMIT_BYTES = 56 * 1024 * 1024


def _rmsnorm(x, g):
    ms = jnp.mean(x * x, axis=-1, keepdims=True)
    return x * jax.lax.rsqrt(ms + EPS) * g


def _dot(a, b):
    return jnp.dot(a, b, preferred_element_type=jnp.float32)


def _layer_kernel(x_ref, p_ref, pool_hist_ref, conv_hist_ref,
                  g_mix_ref, w_in_ref, w_group_ref, pool_scale_ref, w_pool_up_ref,
                  w_conv_ref, w_conv_out_ref, w_o_ref, g_ffn_ref, w_ffn_in_ref,
                  w_ffn_out_ref, g_ple_ref, w_ple_ref, w_ple_gate_ref, g_final_ref,
                  y_ref, u_out_ref, v_out_ref,
                  ext_ref, vext_ref, act_ref, *, tile, tiles_per_seq, sample):
    bf16 = jnp.bfloat16
    if not sample:
        @pl.when(pl.program_id(0) % tiles_per_seq == 0)
        def _():
            ext_ref[:POOL_PAD, :] = jnp.zeros((POOL_PAD, POOL_WIDTH), jnp.float32)
            vext_ref[:CONV_PAD, :] = jnp.zeros((CONV_PAD, CONV_WIDTH), jnp.float32)

    x = x_ref[...]
    hn = _rmsnorm(x, g_mix_ref[...]).astype(bf16)

    def proj(off, width):
        return _dot(hn, w_in_ref[:, off:off + width])

    u = proj(OFF_U, POOL_WIDTH)
    v = proj(OFF_C, CONV_WIDTH) * proj(OFF_H, CONV_WIDTH)

    if sample:
        seqs = tile // SUBLANES
        ext_ref[:, :POOL_PAD, :] = pool_hist_ref[...]
        ext_ref[:, POOL_PAD:, :] = u.reshape(seqs, SUBLANES, POOL_WIDTH)
        vext_ref[:, :CONV_PAD, :] = conv_hist_ref[...]
        vext_ref[:, CONV_PAD:, :] = v.reshape(seqs, SUBLANES, CONV_WIDTH)

        def ext_rows(back, c0, c1):
            r = ext_ref[:, POOL_PAD - back:POOL_PAD - back + SUBLANES, c0:c1]
            return r.reshape(tile, c1 - c0)

        def vext_rows(back):
            r = vext_ref[:, CONV_PAD - back:CONV_PAD - back + SUBLANES, :]
            return r.reshape(tile, CONV_WIDTH)
    else:
        ext_ref[POOL_PAD:, :] = u
        vext_ref[CONV_PAD:, :] = v

        def ext_rows(back, c0, c1):
            return ext_ref[POOL_PAD - back:POOL_PAD - back + tile, c0:c1]

        def vext_rows(back):
            return vext_ref[CONV_PAD - back:CONV_PAD - back + tile, :]

    if sample:
        valid = None
    else:
        pos = (pl.program_id(0) % tiles_per_seq) * tile + jax.lax.broadcasted_iota(
            jnp.int32, (tile, POOL_GROUP), 0)
        valid = (pos + 1).astype(jnp.float32)
    mixed = []
    for gi, w in enumerate(POOL_WINDOWS):
        c0, c1 = gi * POOL_GROUP, (gi + 1) * POOL_GROUP
        win = ext_rows(0, c0, c1)
        for back in range(1, w):
            win = win + ext_rows(back, c0, c1)
        if valid is None:
            mean = win * (1.0 / w)
        else:
            mean = win / jnp.minimum(valid, float(w))
        d = (mean - ext_rows(0, c0, c1)).astype(bf16)
        mixed.append(_dot(d, w_group_ref[gi]))
    pool_out = (jnp.concatenate(mixed, axis=-1) * pool_scale_ref[...]).astype(bf16)
    pool_up = _dot(pool_out, w_pool_up_ref[...])

    wc = w_conv_ref[...]
    yconv = wc[0:1] * vext_rows(2) + wc[1:2] * vext_rows(1) + wc[2:3] * vext_rows(0)
    conv_out = (proj(OFF_B, CONV_WIDTH) * yconv).astype(bf16)
    conv_up = _dot(conv_out, w_conv_out_ref[...])

    if sample:
        u_out_ref[...] = u
        v_out_ref[...] = v
    else:
        u_tail = ext_ref[tile:tile + POOL_PAD, :]
        v_tail = vext_ref[tile:tile + CONV_PAD, :]
        ext_ref[:POOL_PAD, :] = u_tail
        vext_ref[:CONV_PAD, :] = v_tail
        u_out_ref[0] = u_tail
        v_out_ref[0] = v_tail

    merged = (jax.nn.sigmoid(proj(OFF_GP, D_MODEL)) * pool_up
              + jax.nn.sigmoid(proj(OFF_GC, D_MODEL)) * conv_up).astype(bf16)
    x = x + _dot(merged, w_o_ref[...])

    hn = _rmsnorm(x, g_ffn_ref[...]).astype(bf16)
    for c0 in range(0, D_FF, FF_CHUNK):
        gate = _dot(hn, w_ffn_in_ref[:, c0:c0 + FF_CHUNK])
        up = _dot(hn, w_ffn_in_ref[:, D_FF + c0:D_FF + c0 + FF_CHUNK])
        act_ref[:, c0:c0 + FF_CHUNK] = (jax.nn.silu(gate) * up).astype(bf16)
    x = x + _dot(act_ref[...], w_ffn_out_ref[...])

    hn = _rmsnorm(x, g_ple_ref[...]).astype(bf16)
    ple_gate = jax.nn.sigmoid(_dot(hn, w_ple_gate_ref[...]))
    x = x + ple_gate * _dot(p_ref[...].astype(bf16), w_ple_ref[...])

    y_ref[...] = _rmsnorm(x, g_final_ref[...])


def _resident(shape):
    zeros = (0,) * len(shape)
    return pl.BlockSpec(shape, lambda i: zeros, pipeline_mode=pl.Buffered(1))


def _run_group(x, p, pool_hist, conv_hist, weights, *, tile, seq_len, sample):
    tokens = x.shape[0]
    assert tokens % tile == 0 and tile % SUBLANES == 0
    n_tiles = tokens // tile
    row = lambda width: pl.BlockSpec((tile, width), lambda i: (i, 0))
    if sample:
        assert seq_len == SUBLANES
        seqs = tile // SUBLANES
        tiles_per_seq = 1
        hist_specs = [
            pl.BlockSpec((seqs, POOL_PAD, POOL_WIDTH), lambda i: (i, 0, 0)),
            pl.BlockSpec((seqs, CONV_PAD, CONV_WIDTH), lambda i: (i, 0, 0)),
        ]
        state_specs = [row(POOL_WIDTH), row(CONV_WIDTH)]
        state_shapes = [jax.ShapeDtypeStruct((tokens, POOL_WIDTH), jnp.float32),
                        jax.ShapeDtypeStruct((tokens, CONV_WIDTH), jnp.float32)]
        scratch = [pltpu.VMEM((seqs, POOL_PAD + SUBLANES, POOL_WIDTH), jnp.float32),
                   pltpu.VMEM((seqs, CONV_PAD + SUBLANES, CONV_WIDTH), jnp.float32)]
    else:
        assert seq_len % tile == 0 and tile >= POOL_PAD
        tiles_per_seq = seq_len // tile
        n_seq = tokens // seq_len
        hist_specs = [_resident(pool_hist.shape), _resident(conv_hist.shape)]
        state_specs = [
            pl.BlockSpec((1, POOL_PAD, POOL_WIDTH), lambda i: (i // tiles_per_seq, 0, 0)),
            pl.BlockSpec((1, CONV_PAD, CONV_WIDTH), lambda i: (i // tiles_per_seq, 0, 0)),
        ]
        state_shapes = [jax.ShapeDtypeStruct((n_seq, POOL_PAD, POOL_WIDTH), jnp.float32),
                        jax.ShapeDtypeStruct((n_seq, CONV_PAD, CONV_WIDTH), jnp.float32)]
        scratch = [pltpu.VMEM((POOL_PAD + tile, POOL_WIDTH), jnp.float32),
                   pltpu.VMEM((CONV_PAD + tile, CONV_WIDTH), jnp.float32)]
    scratch.append(pltpu.VMEM((tile, D_FF), jnp.bfloat16))

    kernel = functools.partial(_layer_kernel, tile=tile, tiles_per_seq=tiles_per_seq,
                               sample=sample)
    return pl.pallas_call(
        kernel,
        grid=(n_tiles,),
        in_specs=[row(D_MODEL), row(D_PLE)] + hist_specs + [_resident(w.shape) for w in weights],
        out_specs=[row(D_MODEL)] + state_specs,
        out_shape=[jax.ShapeDtypeStruct((tokens, D_MODEL), jnp.float32)] + state_shapes,
        scratch_shapes=scratch,
        compiler_params=pltpu.CompilerParams(
            dimension_semantics=("arbitrary",),
            vmem_limit_bytes=VMEM_LIMIT_BYTES),
        name="hybrid_layer_sample" if sample else "hybrid_layer_prompt",
    )(x, p, pool_hist, conv_hist, *weights)


def kernel(x_prompt, x_sample, state_pool, state_conv, p_prompt, p_sample, g_mix, w_in, w_pool_group, pool_scale, w_pool_up, w_conv, w_conv_out, w_o, g_ffn, w_ffn_in, w_ffn_out, g_ple, w_ple, w_ple_gate, g_final):
    depth = g_mix.shape[0]
    assert depth == 1, "single-layer step"
    bf16 = jnp.bfloat16
    vec = lambda a: a.reshape(1, -1)
    weights = (
        vec(g_mix[0]), w_in[0].astype(bf16), w_pool_group[0].astype(bf16), vec(pool_scale[0]),
        w_pool_up[0].astype(bf16), w_conv[0], w_conv_out[0].astype(bf16), w_o[0].astype(bf16),
        vec(g_ffn[0]), w_ffn_in[0].astype(bf16), w_ffn_out[0].astype(bf16), vec(g_ple[0]),
        w_ple[0].astype(bf16), w_ple_gate[0].astype(bf16), vec(g_final),
    )

    batch, seq, _ = x_prompt.shape
    dec_batch, dec_seq, _ = x_sample.shape

    dummy_pool = jnp.zeros((SUBLANES, POOL_GROUP), jnp.float32)
    dummy_conv = jnp.zeros((SUBLANES, POOL_GROUP), jnp.float32)
    y_p, u_tail, v_tail = _run_group(
        x_prompt.reshape(batch * seq, D_MODEL), p_prompt[0].reshape(batch * seq, D_PLE),
        dummy_pool, dummy_conv, weights, tile=TILE_PROMPT, seq_len=seq, sample=False)
    new_pool_prompt = u_tail[:, POOL_PAD - POOL_HIST:]
    new_conv_prompt = v_tail[:, CONV_PAD - CONV_HIST:]

    pool_hist = jnp.pad(state_pool[0], ((0, 0), (POOL_PAD - POOL_HIST, 0), (0, 0)))
    conv_hist = jnp.pad(state_conv[0], ((0, 0), (CONV_PAD - CONV_HIST, 0), (0, 0)))
    y_s, u_s, v_s = _run_group(
        x_sample.reshape(dec_batch * dec_seq, D_MODEL),
        p_sample[0].reshape(dec_batch * dec_seq, D_PLE),
        pool_hist, conv_hist, weights, tile=TILE_SAMPLE, seq_len=dec_seq, sample=True)
    u_s = u_s.reshape(dec_batch, dec_seq, POOL_WIDTH)
    v_s = v_s.reshape(dec_batch, dec_seq, CONV_WIDTH)
    new_pool_sample = jnp.concatenate([state_pool[0], u_s], axis=1)[:, -POOL_HIST:]
    new_conv_sample = jnp.concatenate([state_conv[0], v_s], axis=1)[:, -CONV_HIST:]

    return (y_p.reshape(batch, seq, D_MODEL), y_s.reshape(dec_batch, dec_seq, D_MODEL),
            new_pool_prompt[None], new_conv_prompt[None],
            new_pool_sample[None], new_conv_sample[None])
```

```python
import functools

import jax
import jax.numpy as jnp
from jax.experimental import pallas as pl
from jax.experimental.pallas import tpu as pltpu

D_MODEL = 1024
POOL_WINDOWS = (2, 4, 8, 16)
POOL_GROUP = 128
POOL_WIDTH = 512
POOL_HIST = 15
CONV_WIDTH = 512
CONV_K = 3
CONV_HIST = 2
D_FF = 2816
D_PLE = 256
EPS = 1e-6

OFF_U = 0
OFF_B = OFF_U + POOL_WIDTH
OFF_C = OFF_B + CONV_WIDTH
OFF_H = OFF_C + CONV_WIDTH
OFF_GP = OFF_H + CONV_WIDTH
OFF_GC = OFF_GP + D_MODEL

SUBLANES = 8
POOL_PAD = 16
CONV_PAD = 8
FF_CHUNK = 256
TILE_PROMPT = 512
TILE_SAMPLE = 256
PREP_ROWS = 256
PREP_COLS = 1024
PREP_BUFS = 3
VMEM_LIMIT_BYTES = 58 * 1024 * 1024

N_VECTORS = 6
N_MATRICES = 9


def _rmsnorm(x, g):
    ms = jnp.mean(x * x, axis=-1, keepdims=True)
    return x * jax.lax.rsqrt(ms + EPS) * g


def _dot(a, b):
    return jnp.dot(a, b, preferred_element_type=jnp.float32)


def _prep_weights(w_hbm, w_vmem, stage_ref, sem):
    chunks = []
    for src, dst in zip(w_hbm, w_vmem):
        n_rows, n_cols = dst.shape
        for r0 in range(0, n_rows, PREP_ROWS):
            for c0 in range(0, n_cols, PREP_COLS):
                chunks.append((src, dst, r0, min(PREP_ROWS, n_rows - r0),
                               c0, min(PREP_COLS, n_cols - c0)))

    def chunk_copy(k):
        src, _, r0, nr, c0, nc = chunks[k]
        slot = k % PREP_BUFS
        return pltpu.make_async_copy(src.at[r0:r0 + nr, c0:c0 + nc],
                                     stage_ref.at[slot, :nr, :nc], sem.at[slot])

    for k in range(min(PREP_BUFS, len(chunks))):
        chunk_copy(k).start()
    for k, (_, dst, r0, nr, c0, nc) in enumerate(chunks):
        chunk_copy(k).wait()
        dst[r0:r0 + nr, c0:c0 + nc] = stage_ref[k % PREP_BUFS, :nr, :nc].astype(jnp.bfloat16)
        if k + PREP_BUFS < len(chunks):
            chunk_copy(k + PREP_BUFS).start()


def _layer_kernel(*refs, tile, tiles_per_seq, sample):
    x_ref, p_ref, pool_hist_ref, conv_hist_ref = refs[:4]
    vectors = refs[4:4 + N_VECTORS]
    w_hbm = refs[4 + N_VECTORS:4 + N_VECTORS + N_MATRICES]
    y_ref, u_out_ref, v_out_ref = refs[4 + N_VECTORS + N_MATRICES:7 + N_VECTORS + N_MATRICES]
    scratch = refs[7 + N_VECTORS + N_MATRICES:]
    ext_ref, vext_ref, act_ref = scratch[:3]
    w_vmem = scratch[3:3 + N_MATRICES]
    stage_ref, sem = scratch[3 + N_MATRICES:]

    step = pl.program_id(0)

    @pl.when(step == 0)
    def _():
        _prep_weights(w_hbm, w_vmem, stage_ref, sem)

    @pl.when(step > 0)
    def _():
        _layer_tile(step - 1, x_ref, p_ref, pool_hist_ref, conv_hist_ref, vectors, w_vmem,
                    y_ref, u_out_ref, v_out_ref, ext_ref, vext_ref, act_ref,
                    tile=tile, tiles_per_seq=tiles_per_seq, sample=sample)


def _layer_tile(t, x_ref, p_ref, pool_hist_ref, conv_hist_ref, vectors, w_vmem,
                y_ref, u_out_ref, v_out_ref, ext_ref, vext_ref, act_ref,
                *, tile, tiles_per_seq, sample):
    g_mix_ref, pool_scale_ref, w_conv_ref, g_ffn_ref, g_ple_ref, g_final_ref = vectors
    (w_in_ref, w_group_ref, w_pool_up_ref, w_conv_out_ref, w_o_ref, w_ffn_in_ref,
     w_ffn_out_ref, w_ple_ref, w_ple_gate_ref) = w_vmem
    bf16 = jnp.bfloat16

    if not sample:
        @pl.when(t % tiles_per_seq == 0)
        def _():
            ext_ref[:POOL_PAD, :] = jnp.zeros((POOL_PAD, POOL_WIDTH), jnp.float32)
            vext_ref[:CONV_PAD, :] = jnp.zeros((CONV_PAD, CONV_WIDTH), jnp.float32)

    x = x_ref[...]
    hn = _rmsnorm(x, g_mix_ref[...]).astype(bf16)

    def proj(off, width):
        return _dot(hn, w_in_ref[:, off:off + width])

    u = proj(OFF_U, POOL_WIDTH)
    v = proj(OFF_C, CONV_WIDTH) * proj(OFF_H, CONV_WIDTH)

    if sample:
        seqs = tile // SUBLANES
        ext_ref[:, :POOL_PAD, :] = pool_hist_ref[...]
        ext_ref[:, POOL_PAD:, :] = u.reshape(seqs, SUBLANES, POOL_WIDTH)
        vext_ref[:, :CONV_PAD, :] = conv_hist_ref[...]
        vext_ref[:, CONV_PAD:, :] = v.reshape(seqs, SUBLANES, CONV_WIDTH)

        def ext_rows(back, c0, c1):
            r = ext_ref[:, POOL_PAD - back:POOL_PAD - back + SUBLANES, c0:c1]
            return r.reshape(tile, c1 - c0)

        def vext_rows(back):
            r = vext_ref[:, CONV_PAD - back:CONV_PAD - back + SUBLANES, :]
            return r.reshape(tile, CONV_WIDTH)
    else:
        ext_ref[POOL_PAD:, :] = u
        vext_ref[CONV_PAD:, :] = v

        def ext_rows(back, c0, c1):
            return ext_ref[POOL_PAD - back:POOL_PAD - back + tile, c0:c1]

        def vext_rows(back):
            return vext_ref[CONV_PAD - back:CONV_PAD - back + tile, :]

    if sample:
        valid = None
    else:
        pos = (t % tiles_per_seq) * tile + jax.lax.broadcasted_iota(
            jnp.int32, (tile, POOL_GROUP), 0)
        valid = (pos + 1).astype(jnp.float32)
    mixed = []
    for gi, w in enumerate(POOL_WINDOWS):
        c0, c1 = gi * POOL_GROUP, (gi + 1) * POOL_GROUP
        win = ext_rows(0, c0, c1)
        for back in range(1, w):
            win = win + ext_rows(back, c0, c1)
        if valid is None:
            mean = win * (1.0 / w)
        else:
            mean = win / jnp.minimum(valid, float(w))
        d = (mean - ext_rows(0, c0, c1)).astype(bf16)
        mixed.append(_dot(d, w_group_ref[c0:c1, :]))
    pool_out = (jnp.concatenate(mixed, axis=-1) * pool_scale_ref[...]).astype(bf16)
    pool_up = _dot(pool_out, w_pool_up_ref[...])

    wc = w_conv_ref[...]
    yconv = wc[0:1] * vext_rows(2) + wc[1:2] * vext_rows(1) + wc[2:3] * vext_rows(0)
    conv_out = (proj(OFF_B, CONV_WIDTH) * yconv).astype(bf16)
    conv_up = _dot(conv_out, w_conv_out_ref[...])

    if sample:
        u_out_ref[...] = u
        v_out_ref[...] = v
    else:
        u_tail = ext_ref[tile:tile + POOL_PAD, :]
        v_tail = vext_ref[tile:tile + CONV_PAD, :]
        ext_ref[:POOL_PAD, :] = u_tail
        vext_ref[:CONV_PAD, :] = v_tail
        u_out_ref[0] = u_tail
        v_out_ref[0] = v_tail

    merged = (jax.nn.sigmoid(proj(OFF_GP, D_MODEL)) * pool_up
              + jax.nn.sigmoid(proj(OFF_GC, D_MODEL)) * conv_up).astype(bf16)
    x = x + _dot(merged, w_o_ref[...])

    hn = _rmsnorm(x, g_ffn_ref[...]).astype(bf16)
    for c0 in range(0, D_FF, FF_CHUNK):
        gate = _dot(hn, w_ffn_in_ref[:, c0:c0 + FF_CHUNK])
        up = _dot(hn, w_ffn_in_ref[:, D_FF + c0:D_FF + c0 + FF_CHUNK])
        act_ref[:, c0:c0 + FF_CHUNK] = (jax.nn.silu(gate) * up).astype(bf16)
    x = x + _dot(act_ref[...], w_ffn_out_ref[...])

    hn = _rmsnorm(x, g_ple_ref[...]).astype(bf16)
    ple_gate = jax.nn.sigmoid(_dot(hn, w_ple_gate_ref[...]))
    x = x + ple_gate * _dot(p_ref[...].astype(bf16), w_ple_ref[...])

    y_ref[...] = _rmsnorm(x, g_final_ref[...])


def _resident(shape):
    zeros = (0,) * len(shape)
    return pl.BlockSpec(shape, lambda i: zeros, pipeline_mode=pl.Buffered(1))


def _run_group(x, p, pool_hist, conv_hist, vectors, matrices, *, tile, seq_len, sample):
    tokens = x.shape[0]
    assert tokens % tile == 0 and tile % SUBLANES == 0
    assert len(vectors) == N_VECTORS and len(matrices) == N_MATRICES
    n_tiles = tokens // tile
    tile_of = lambda i: jnp.maximum(i - 1, 0)
    row = lambda width: pl.BlockSpec((tile, width), lambda i: (tile_of(i), 0))
    if sample:
        assert seq_len == SUBLANES
        seqs = tile // SUBLANES
        tiles_per_seq = 1
        hist_specs = [
            pl.BlockSpec((seqs, POOL_PAD, POOL_WIDTH), lambda i: (tile_of(i), 0, 0)),
            pl.BlockSpec((seqs, CONV_PAD, CONV_WIDTH), lambda i: (tile_of(i), 0, 0)),
        ]
        state_specs = [row(POOL_WIDTH), row(CONV_WIDTH)]
        state_shapes = [jax.ShapeDtypeStruct((tokens, POOL_WIDTH), jnp.float32),
                        jax.ShapeDtypeStruct((tokens, CONV_WIDTH), jnp.float32)]
        scratch = [pltpu.VMEM((seqs, POOL_PAD + SUBLANES, POOL_WIDTH), jnp.float32),
                   pltpu.VMEM((seqs, CONV_PAD + SUBLANES, CONV_WIDTH), jnp.float32)]
    else:
        assert seq_len % tile == 0 and tile >= POOL_PAD
        tiles_per_seq = seq_len // tile
        n_seq = tokens // seq_len
        hist_specs = [_resident(pool_hist.shape), _resident(conv_hist.shape)]
        seq_of = lambda i: tile_of(i) // tiles_per_seq
        state_specs = [
            pl.BlockSpec((1, POOL_PAD, POOL_WIDTH), lambda i: (seq_of(i), 0, 0)),
            pl.BlockSpec((1, CONV_PAD, CONV_WIDTH), lambda i: (seq_of(i), 0, 0)),
        ]
        state_shapes = [jax.ShapeDtypeStruct((n_seq, POOL_PAD, POOL_WIDTH), jnp.float32),
                        jax.ShapeDtypeStruct((n_seq, CONV_PAD, CONV_WIDTH), jnp.float32)]
        scratch = [pltpu.VMEM((POOL_PAD + tile, POOL_WIDTH), jnp.float32),
                   pltpu.VMEM((CONV_PAD + tile, CONV_WIDTH), jnp.float32)]
    scratch.append(pltpu.VMEM((tile, D_FF), jnp.bfloat16))
    scratch += [pltpu.VMEM(w.shape, jnp.bfloat16) for w in matrices]
    scratch += [pltpu.VMEM((PREP_BUFS, PREP_ROWS, PREP_COLS), jnp.float32),
                pltpu.SemaphoreType.DMA((PREP_BUFS,))]

    kernel = functools.partial(_layer_kernel, tile=tile, tiles_per_seq=tiles_per_seq,
                               sample=sample)
    return pl.pallas_call(
        kernel,
        grid=(n_tiles + 1,),
        in_specs=([row(D_MODEL), row(D_PLE)] + hist_specs
                  + [_resident(v.shape) for v in vectors]
                  + [pl.BlockSpec(memory_space=pl.ANY)] * N_MATRICES),
        out_specs=[row(D_MODEL)] + state_specs,
        out_shape=[jax.ShapeDtypeStruct((tokens, D_MODEL), jnp.float32)] + state_shapes,
        scratch_shapes=scratch,
        compiler_params=pltpu.CompilerParams(
            dimension_semantics=("arbitrary",),
            vmem_limit_bytes=VMEM_LIMIT_BYTES),
        name="hybrid_layer_sample" if sample else "hybrid_layer_prompt",
    )(x, p, pool_hist, conv_hist, *vectors, *matrices)


def kernel(x_prompt, x_sample, state_pool, state_conv, p_prompt, p_sample, g_mix, w_in, w_pool_group, pool_scale, w_pool_up, w_conv, w_conv_out, w_o, g_ffn, w_ffn_in, w_ffn_out, g_ple, w_ple, w_ple_gate, g_final):
    depth = g_mix.shape[0]
    assert depth == 1, "single-layer step"
    vec = lambda a: a.reshape(1, -1)
    vectors = (vec(g_mix[0]), vec(pool_scale[0]), w_conv[0], vec(g_ffn[0]), vec(g_ple[0]),
               vec(g_final))
    matrices = (w_in[0], w_pool_group[0].reshape(POOL_WIDTH, POOL_GROUP), w_pool_up[0],
                w_conv_out[0], w_o[0], w_ffn_in[0], w_ffn_out[0], w_ple[0], w_ple_gate[0])

    batch, seq, _ = x_prompt.shape
    dec_batch, dec_seq, _ = x_sample.shape

    dummy_pool = jnp.zeros((SUBLANES, POOL_GROUP), jnp.float32)
    dummy_conv = jnp.zeros((SUBLANES, POOL_GROUP), jnp.float32)
    y_p, u_tail, v_tail = _run_group(
        x_prompt.reshape(batch * seq, D_MODEL), p_prompt[0].reshape(batch * seq, D_PLE),
        dummy_pool, dummy_conv, vectors, matrices, tile=TILE_PROMPT, seq_len=seq, sample=False)
    new_pool_prompt = u_tail[:, POOL_PAD - POOL_HIST:]
    new_conv_prompt = v_tail[:, CONV_PAD - CONV_HIST:]

    pool_hist = jnp.pad(state_pool[0], ((0, 0), (POOL_PAD - POOL_HIST, 0), (0, 0)))
    conv_hist = jnp.pad(state_conv[0], ((0, 0), (CONV_PAD - CONV_HIST, 0), (0, 0)))
    y_s, u_s, v_s = _run_group(
        x_sample.reshape(dec_batch * dec_seq, D_MODEL),
        p_sample[0].reshape(dec_batch * dec_seq, D_PLE),
        pool_hist, conv_hist, vectors, matrices, tile=TILE_SAMPLE, seq_len=dec_seq, sample=True)
    u_s = u_s.reshape(dec_batch, dec_seq, POOL_WIDTH)
    v_s = v_s.reshape(dec_batch, dec_seq, CONV_WIDTH)
    new_pool_sample = jnp.concatenate([state_pool[0], u_s], axis=1)[:, -POOL_HIST:]
    new_conv_sample = jnp.concatenate([state_conv[0], v_s], axis=1)[:, -CONV_HIST:]

    return (y_p.reshape(batch, seq, D_MODEL), y_s.reshape(dec_batch, dec_seq, D_MODEL),
            new_pool_prompt[None], new_conv_prompt[None],
            new_pool_sample[None], new_conv_sample[None])
```

```python
import functools

import jax
import jax.numpy as jnp
from jax.experimental import pallas as pl
from jax.experimental.pallas import tpu as pltpu

D_MODEL = 1024
POOL_WINDOWS = (2, 4, 8, 16)
POOL_GROUP = 128
POOL_WIDTH = 512
POOL_HIST = 15
CONV_WIDTH = 512
CONV_K = 3
CONV_HIST = 2
D_FF = 2816
D_PLE = 256
EPS = 1e-6

OFF_U = 0
OFF_B = OFF_U + POOL_WIDTH
OFF_C = OFF_B + CONV_WIDTH
OFF_H = OFF_C + CONV_WIDTH
OFF_GP = OFF_H + CONV_WIDTH
OFF_GC = OFF_GP + D_MODEL

SUBLANES = 8
POOL_PAD = 16
CONV_PAD = 8
FF_CHUNK = 256
TILE_PROMPT = 512
TILE_SAMPLE = 256
PREP_ROWS = 256
PREP_COLS = 1024
PREP_BUFS = 4
VMEM_LIMIT_BYTES = 60 * 1024 * 1024

N_VECTORS = 6
N_MATRICES = 9
N_INPUTS = 6
N_OUTPUTS = 6


def _rmsnorm(x, g):
    ms = jnp.mean(x * x, axis=-1, keepdims=True)
    return x * jax.lax.rsqrt(ms + EPS) * g


def _dot(a, b):
    return jnp.dot(a, b, preferred_element_type=jnp.float32)


def _prep_weights(w_hbm, w_vmem, stage_ref, sem):
    chunks = []
    for src, dst in zip(w_hbm, w_vmem):
        n_rows, n_cols = dst.shape
        for r0 in range(0, n_rows, PREP_ROWS):
            for c0 in range(0, n_cols, PREP_COLS):
                chunks.append((src, dst, r0, min(PREP_ROWS, n_rows - r0),
                               c0, min(PREP_COLS, n_cols - c0)))

    def chunk_copy(k):
        src, _, r0, nr, c0, nc = chunks[k]
        slot = k % PREP_BUFS
        return pltpu.make_async_copy(src.at[r0:r0 + nr, c0:c0 + nc],
                                     stage_ref.at[slot, :nr, :nc], sem.at[slot])

    for k in range(min(PREP_BUFS, len(chunks))):
        chunk_copy(k).start()
    for k, (_, dst, r0, nr, c0, nc) in enumerate(chunks):
        chunk_copy(k).wait()
        dst[r0:r0 + nr, c0:c0 + nc] = stage_ref[k % PREP_BUFS, :nr, :nc].astype(jnp.bfloat16)
        if k + PREP_BUFS < len(chunks):
            chunk_copy(k + PREP_BUFS).start()


def _layer_math(x, p, vectors, w_vmem, act_ref, pool_conv):
    g_mix_ref, pool_scale_ref, _, g_ffn_ref, g_ple_ref, g_final_ref = vectors
    (w_in_ref, w_group_ref, w_pool_up_ref, w_conv_out_ref, w_o_ref, w_ffn_in_ref,
     w_ffn_out_ref, w_ple_ref, w_ple_gate_ref) = w_vmem
    bf16 = jnp.bfloat16

    hn = _rmsnorm(x, g_mix_ref[...]).astype(bf16)

    def proj(off, width):
        return _dot(hn, w_in_ref[:, off:off + width])

    u = proj(OFF_U, POOL_WIDTH)
    v = proj(OFF_C, CONV_WIDTH) * proj(OFF_H, CONV_WIDTH)
    pooled, yconv = pool_conv(u, v)

    mixed = [_dot(d, w_group_ref[gi * POOL_GROUP:(gi + 1) * POOL_GROUP, :])
             for gi, d in enumerate(pooled)]
    pool_out = (jnp.concatenate(mixed, axis=-1) * pool_scale_ref[...]).astype(bf16)
    pool_up = _dot(pool_out, w_pool_up_ref[...])
    conv_out = (proj(OFF_B, CONV_WIDTH) * yconv).astype(bf16)
    conv_up = _dot(conv_out, w_conv_out_ref[...])

    merged = (jax.nn.sigmoid(proj(OFF_GP, D_MODEL)) * pool_up
              + jax.nn.sigmoid(proj(OFF_GC, D_MODEL)) * conv_up).astype(bf16)
    x = x + _dot(merged, w_o_ref[...])

    rows = x.shape[0]
    hn = _rmsnorm(x, g_ffn_ref[...]).astype(bf16)
    for c0 in range(0, D_FF, FF_CHUNK):
        gate = _dot(hn, w_ffn_in_ref[:, c0:c0 + FF_CHUNK])
        up = _dot(hn, w_ffn_in_ref[:, D_FF + c0:D_FF + c0 + FF_CHUNK])
        act_ref[:rows, c0:c0 + FF_CHUNK] = (jax.nn.silu(gate) * up).astype(bf16)
    x = x + _dot(act_ref[:rows, :], w_ffn_out_ref[...])

    hn = _rmsnorm(x, g_ple_ref[...]).astype(bf16)
    ple_gate = jax.nn.sigmoid(_dot(hn, w_ple_gate_ref[...]))
    x = x + ple_gate * _dot(p.astype(bf16), w_ple_ref[...])

    return _rmsnorm(x, g_final_ref[...])


def _prompt_tile(t, x_ref, p_ref, vectors, w_vmem, y_ref, u_tail_ref, v_tail_ref,
                 ext_ref, vext_ref, act_ref, *, tile, tiles_per_seq):
    w_conv_ref = vectors[2]

    @pl.when(t % tiles_per_seq == 0)
    def _():
        ext_ref[:POOL_PAD, :] = jnp.zeros((POOL_PAD, POOL_WIDTH), jnp.float32)
        vext_ref[:CONV_PAD, :] = jnp.zeros((CONV_PAD, CONV_WIDTH), jnp.float32)

    def pool_conv(u, v):
        ext_ref[POOL_PAD:, :] = u
        vext_ref[CONV_PAD:, :] = v

        def ext_rows(back, c0, c1):
            return ext_ref[POOL_PAD - back:POOL_PAD - back + tile, c0:c1]

        def vext_rows(back):
            return vext_ref[CONV_PAD - back:CONV_PAD - back + tile, :]

        pos = (t % tiles_per_seq) * tile + jax.lax.broadcasted_iota(
            jnp.int32, (tile, POOL_GROUP), 0)
        valid = (pos + 1).astype(jnp.float32)
        pooled = []
        for gi, w in enumerate(POOL_WINDOWS):
            c0, c1 = gi * POOL_GROUP, (gi + 1) * POOL_GROUP
            win = ext_rows(0, c0, c1)
            for back in range(1, w):
                win = win + ext_rows(back, c0, c1)
            mean = win / jnp.minimum(valid, float(w))
            pooled.append((mean - ext_rows(0, c0, c1)).astype(jnp.bfloat16))

        wc = w_conv_ref[...]
        yconv = wc[0:1] * vext_rows(2) + wc[1:2] * vext_rows(1) + wc[2:3] * vext_rows(0)

        u_tail = ext_ref[tile:tile + POOL_PAD, :]
        v_tail = vext_ref[tile:tile + CONV_PAD, :]
        ext_ref[:POOL_PAD, :] = u_tail
        vext_ref[:CONV_PAD, :] = v_tail
        u_tail_ref[0] = u_tail
        v_tail_ref[0] = v_tail
        return pooled, yconv

    y_ref[...] = _layer_math(x_ref[...], p_ref[...], vectors, w_vmem, act_ref, pool_conv)


def _sample_tile(x_ref, p_ref, pool_hist_ref, conv_hist_ref, vectors, w_vmem,
                 y_ref, pool_new_ref, conv_new_ref, act_ref, *, seqs, steps):
    w_conv_ref = vectors[2]
    by_step = lambda ref: jnp.concatenate([ref[:, s, :] for s in range(steps)], axis=0)
    slab = lambda a, s: a[s * seqs:(s + 1) * seqs]

    def pool_conv(u, v):
        line = [pool_hist_ref[s] for s in range(POOL_HIST)] + [slab(u, s) for s in range(steps)]
        pooled = []
        for gi, w in enumerate(POOL_WINDOWS):
            c0, c1 = gi * POOL_GROUP, (gi + 1) * POOL_GROUP
            per_step = []
            for s in range(steps):
                now = POOL_HIST + s
                win = line[now][:, c0:c1]
                for back in range(1, w):
                    win = win + line[now - back][:, c0:c1]
                per_step.append(win * (1.0 / w) - line[now][:, c0:c1])
            pooled.append(jnp.concatenate(per_step, axis=0).astype(jnp.bfloat16))
        for s in range(POOL_HIST):
            pool_new_ref[s] = line[steps + s]

        vline = [conv_hist_ref[s] for s in range(CONV_HIST)] + [slab(v, s) for s in range(steps)]
        wc = w_conv_ref[...]
        yconv = jnp.concatenate(
            [wc[0:1] * vline[s] + wc[1:2] * vline[s + 1] + wc[2:3] * vline[s + 2]
             for s in range(steps)], axis=0)
        for s in range(CONV_HIST):
            conv_new_ref[s] = vline[steps + s]
        return pooled, yconv

    y = _layer_math(by_step(x_ref), by_step(p_ref), vectors, w_vmem, act_ref, pool_conv)
    for s in range(steps):
        y_ref[:, s, :] = slab(y, s)


def _layer_kernel(*refs, n_prompt_tiles, tile_prompt, tiles_per_seq, seqs, steps):
    xp_ref, pp_ref, xs_ref, ps_ref, pool_hist_ref, conv_hist_ref = refs[:N_INPUTS]
    refs = refs[N_INPUTS:]
    vectors, refs = refs[:N_VECTORS], refs[N_VECTORS:]
    w_hbm, refs = refs[:N_MATRICES], refs[N_MATRICES:]
    (yp_ref, u_tail_ref, v_tail_ref, ys_ref, pool_new_ref, conv_new_ref) = refs[:N_OUTPUTS]
    refs = refs[N_OUTPUTS:]
    ext_ref, vext_ref, act_ref = refs[:3]
    w_vmem = refs[3:3 + N_MATRICES]
    stage_ref, sem = refs[3 + N_MATRICES:]

    step = pl.program_id(0)

    @pl.when(step == 0)
    def _():
        _prep_weights(w_hbm, w_vmem, stage_ref, sem)

    @pl.when(jnp.logical_and(step >= 1, step <= n_prompt_tiles))
    def _():
        _prompt_tile(step - 1, xp_ref, pp_ref, vectors, w_vmem, yp_ref, u_tail_ref, v_tail_ref,
                     ext_ref, vext_ref, act_ref, tile=tile_prompt, tiles_per_seq=tiles_per_seq)

    @pl.when(step > n_prompt_tiles)
    def _():
        _sample_tile(xs_ref, ps_ref, pool_hist_ref, conv_hist_ref, vectors, w_vmem,
                     ys_ref, pool_new_ref, conv_new_ref, act_ref, seqs=seqs, steps=steps)


def _resident(shape):
    zeros = (0,) * len(shape)
    return pl.BlockSpec(shape, lambda i: zeros, pipeline_mode=pl.Buffered(1))


def _run_layer(x_p, p_p, x_s, p_s, pool_hist, conv_hist, vectors, matrices, *, seq_len, steps):
    tokens_p, tokens_s = x_p.shape[0], x_s.shape[0] * steps
    assert x_s.shape[1] == steps
    tile_p, tile_s = TILE_PROMPT, TILE_SAMPLE
    assert tokens_p % tile_p == 0 and seq_len % tile_p == 0 and tile_p >= POOL_PAD
    assert tokens_s % tile_s == 0 and tile_s % (steps * SUBLANES) == 0
    assert len(vectors) == N_VECTORS and len(matrices) == N_MATRICES
    n_p, n_s = tokens_p // tile_p, tokens_s // tile_s
    tiles_per_seq = seq_len // tile_p
    n_seq = tokens_p // seq_len
    seqs = tile_s // steps

    tile_p_of = lambda i: jnp.clip(i - 1, 0, n_p - 1)
    tile_s_of = lambda i: jnp.clip(i - 1 - n_p, 0, n_s - 1)
    row_p = lambda width: pl.BlockSpec((tile_p, width), lambda i: (tile_p_of(i), 0))
    seq_p = lambda rows: pl.BlockSpec((1, rows, POOL_WIDTH),
                                      lambda i: (tile_p_of(i) // tiles_per_seq, 0, 0))
    row_s_in = lambda width: pl.BlockSpec((seqs, steps, width), lambda i: (tile_s_of(i), 0, 0),
                                          pipeline_mode=pl.Buffered(1))
    hist_in = lambda rows: pl.BlockSpec((rows, seqs, POOL_WIDTH), lambda i: (0, tile_s_of(i), 0),
                                        pipeline_mode=pl.Buffered(1))
    hist_out = lambda rows: pl.BlockSpec((rows, seqs, POOL_WIDTH), lambda i: (0, tile_s_of(i), 0),
                                         pipeline_mode=pl.Buffered(1))

    f32 = jnp.float32
    out_shape = [
        jax.ShapeDtypeStruct((tokens_p, D_MODEL), f32),
        jax.ShapeDtypeStruct((n_seq, POOL_PAD, POOL_WIDTH), f32),
        jax.ShapeDtypeStruct((n_seq, CONV_PAD, CONV_WIDTH), f32),
        jax.ShapeDtypeStruct((tokens_s // steps, steps, D_MODEL), f32),
        jax.ShapeDtypeStruct((POOL_HIST, tokens_s // steps, POOL_WIDTH), f32),
        jax.ShapeDtypeStruct((CONV_HIST, tokens_s // steps, CONV_WIDTH), f32),
    ]
    out_specs = [row_p(D_MODEL), seq_p(POOL_PAD), seq_p(CONV_PAD),
                 pl.BlockSpec((seqs, steps, D_MODEL), lambda i: (tile_s_of(i), 0, 0),
                              pipeline_mode=pl.Buffered(1)),
                 hist_out(POOL_HIST), hist_out(CONV_HIST)]
    scratch = [pltpu.VMEM((POOL_PAD + tile_p, POOL_WIDTH), f32),
               pltpu.VMEM((CONV_PAD + tile_p, CONV_WIDTH), f32),
               pltpu.VMEM((max(tile_p, tile_s), D_FF), jnp.bfloat16)]
    scratch += [pltpu.VMEM(w.shape, jnp.bfloat16) for w in matrices]
    scratch += [pltpu.VMEM((PREP_BUFS, PREP_ROWS, PREP_COLS), f32),
                pltpu.SemaphoreType.DMA((PREP_BUFS,))]

    kernel = functools.partial(_layer_kernel, n_prompt_tiles=n_p, tile_prompt=tile_p,
                               tiles_per_seq=tiles_per_seq, seqs=seqs, steps=steps)
    return pl.pallas_call(
        kernel,
        grid=(1 + n_p + n_s,),
        in_specs=([row_p(D_MODEL), row_p(D_PLE), row_s_in(D_MODEL), row_s_in(D_PLE),
                   hist_in(POOL_HIST), hist_in(CONV_HIST)]
                  + [_resident(v.shape) for v in vectors]
                  + [pl.BlockSpec(memory_space=pl.ANY)] * N_MATRICES),
        out_specs=out_specs,
        out_shape=out_shape,
        scratch_shapes=scratch,
        compiler_params=pltpu.CompilerParams(
            dimension_semantics=("arbitrary",),
            vmem_limit_bytes=VMEM_LIMIT_BYTES),
        name="hybrid_layer",
    )(x_p, p_p, x_s, p_s, pool_hist, conv_hist, *vectors, *matrices)


def kernel(x_prompt, x_sample, state_pool, state_conv, p_prompt, p_sample, g_mix, w_in, w_pool_group, pool_scale, w_pool_up, w_conv, w_conv_out, w_o, g_ffn, w_ffn_in, w_ffn_out, g_ple, w_ple, w_ple_gate, g_final):
    depth = g_mix.shape[0]
    assert depth == 1, "single-layer step"
    vec = lambda a: a.reshape(1, -1)
    vectors = (vec(g_mix[0]), vec(pool_scale[0]), w_conv[0], vec(g_ffn[0]), vec(g_ple[0]),
               vec(g_final))
    matrices = (w_in[0], w_pool_group[0].reshape(POOL_WIDTH, POOL_GROUP), w_pool_up[0],
                w_conv_out[0], w_o[0], w_ffn_in[0], w_ffn_out[0], w_ple[0], w_ple_gate[0])

    batch, seq, _ = x_prompt.shape
    dec_seq = x_sample.shape[1]

    y_p, u_tail, v_tail, y_s, pool_new, conv_new = _run_layer(
        x_prompt.reshape(batch * seq, D_MODEL), p_prompt[0].reshape(batch * seq, D_PLE),
        x_sample, p_sample[0],
        jnp.swapaxes(state_pool[0], 0, 1), jnp.swapaxes(state_conv[0], 0, 1),
        vectors, matrices, seq_len=seq, steps=dec_seq)

    new_pool_prompt = u_tail[:, POOL_PAD - POOL_HIST:]
    new_conv_prompt = v_tail[:, CONV_PAD - CONV_HIST:]
    new_pool_sample = jnp.swapaxes(pool_new, 0, 1)
    new_conv_sample = jnp.swapaxes(conv_new, 0, 1)

    return (y_p.reshape(batch, seq, D_MODEL), y_s,
            new_pool_prompt[None], new_conv_prompt[None],
            new_pool_sample[None], new_conv_sample[None])
```

```python
import functools

import jax
import jax.numpy as jnp
from jax.experimental import pallas as pl
from jax.experimental.pallas import tpu as pltpu

D_MODEL = 1024
POOL_WINDOWS = (2, 4, 8, 16)
POOL_GROUP = 128
POOL_WIDTH = 512
POOL_HIST = 15
CONV_WIDTH = 512
CONV_K = 3
CONV_HIST = 2
D_FF = 2816
D_PLE = 256
EPS = 1e-6

OFF_U = 0
OFF_B = OFF_U + POOL_WIDTH
OFF_C = OFF_B + CONV_WIDTH
OFF_H = OFF_C + CONV_WIDTH
OFF_GP = OFF_H + CONV_WIDTH
OFF_GC = OFF_GP + D_MODEL

SUBLANES = 8
POOL_PAD = 16
CONV_PAD = 8
FF_CHUNK = 256
TILE_PROMPT = 512
TILE_SAMPLE = 512
PROMPT_CHAINS = 2
SAMPLE_CHAINS = 2
PREP_ROWS = 256
PREP_COLS = 1024
PREP_BUFS = 5
VMEM_LIMIT_BYTES = 62 * 1024 * 1024

N_VECTORS = 6
N_MATRICES = 9
N_INPUTS = 6
N_OUTPUTS = 6


def _rmsnorm(x, g):
    ms = jnp.mean(x * x, axis=-1, keepdims=True)
    return x * jax.lax.rsqrt(ms + EPS) * g


def _dot(a, b):
    return jnp.dot(a, b, preferred_element_type=jnp.float32)


def _prep_weights(w_hbm, w_vmem, stage_ref, sem):
    chunks = []
    for src, dst in zip(w_hbm, w_vmem):
        n_rows, n_cols = dst.shape
        for r0 in range(0, n_rows, PREP_ROWS):
            for c0 in range(0, n_cols, PREP_COLS):
                chunks.append((src, dst, r0, min(PREP_ROWS, n_rows - r0),
                               c0, min(PREP_COLS, n_cols - c0)))

    def chunk_copy(k):
        src, _, r0, nr, c0, nc = chunks[k]
        slot = k % PREP_BUFS
        return pltpu.make_async_copy(src.at[r0:r0 + nr, c0:c0 + nc],
                                     stage_ref.at[slot, :nr, :nc], sem.at[slot])

    for k in range(min(PREP_BUFS, len(chunks))):
        chunk_copy(k).start()
    for k, (_, dst, r0, nr, c0, nc) in enumerate(chunks):
        chunk_copy(k).wait()
        dst[r0:r0 + nr, c0:c0 + nc] = stage_ref[k % PREP_BUFS, :nr, :nc].astype(jnp.bfloat16)
        if k + PREP_BUFS < len(chunks):
            chunk_copy(k + PREP_BUFS).start()


def _layer_stages(x, p, vectors, w_vmem, act_ref, pool_conv):
    g_mix_ref, pool_scale_ref, _, g_ffn_ref, g_ple_ref, g_final_ref = vectors
    (w_in_ref, w_group_ref, w_pool_up_ref, w_conv_out_ref, w_o_ref, w_ffn_in_ref,
     w_ffn_out_ref, w_ple_ref, w_ple_gate_ref) = w_vmem
    bf16 = jnp.bfloat16

    hn = _rmsnorm(x, g_mix_ref[...]).astype(bf16)

    def proj(off, width):
        return _dot(hn, w_in_ref[:, off:off + width])

    u = proj(OFF_U, POOL_WIDTH)
    yield
    v = proj(OFF_C, CONV_WIDTH) * proj(OFF_H, CONV_WIDTH)
    yield
    pooled, yconv = pool_conv(u, v)
    mixed = [_dot(d, w_group_ref[gi * POOL_GROUP:(gi + 1) * POOL_GROUP, :])
             for gi, d in enumerate(pooled)]
    pool_out = (jnp.concatenate(mixed, axis=-1) * pool_scale_ref[...]).astype(bf16)
    yield
    pool_up = _dot(pool_out, w_pool_up_ref[...])
    yield
    conv_out = (proj(OFF_B, CONV_WIDTH) * yconv).astype(bf16)
    conv_up = _dot(conv_out, w_conv_out_ref[...])
    yield
    gated_pool = jax.nn.sigmoid(proj(OFF_GP, D_MODEL)) * pool_up
    yield
    merged = (gated_pool + jax.nn.sigmoid(proj(OFF_GC, D_MODEL)) * conv_up).astype(bf16)
    yield
    x = x + _dot(merged, w_o_ref[...])
    yield

    hn = _rmsnorm(x, g_ffn_ref[...]).astype(bf16)
    for c0 in range(0, D_FF, FF_CHUNK):
        gate = _dot(hn, w_ffn_in_ref[:, c0:c0 + FF_CHUNK])
        up = _dot(hn, w_ffn_in_ref[:, D_FF + c0:D_FF + c0 + FF_CHUNK])
        act_ref[:, c0:c0 + FF_CHUNK] = (jax.nn.silu(gate) * up).astype(bf16)
        yield
    x = x + _dot(act_ref[...], w_ffn_out_ref[...])
    yield

    hn = _rmsnorm(x, g_ple_ref[...]).astype(bf16)
    ple_gate = jax.nn.sigmoid(_dot(hn, w_ple_gate_ref[...]))
    yield
    x = x + ple_gate * _dot(p.astype(bf16), w_ple_ref[...])
    return _rmsnorm(x, g_final_ref[...])


def _run_interleaved(chains):
    results = [None] * len(chains)
    live = list(range(len(chains)))
    while live:
        for i in list(live):
            try:
                next(chains[i])
            except StopIteration as done:
                results[i] = done.value
                live.remove(i)
    return results


def _prompt_tile(t, x_ref, p_ref, vectors, w_vmem, y_ref, u_tail_ref, v_tail_ref,
                 u_carry_ref, v_carry_ref, ext_ref, vext_ref, act_ref, *, tile, tiles_per_seq):
    w_conv_ref = vectors[2]
    rows = tile // PROMPT_CHAINS

    @pl.when(t % tiles_per_seq == 0)
    def _():
        u_carry_ref[...] = jnp.zeros((POOL_PAD, POOL_WIDTH), jnp.float32)
        v_carry_ref[...] = jnp.zeros((CONV_PAD, CONV_WIDTH), jnp.float32)

    ext_ref[:POOL_PAD, :] = u_carry_ref[...]
    vext_ref[:CONV_PAD, :] = v_carry_ref[...]

    def pool_conv_at(r0):
        def pool_conv(u, v):
            ext_ref[POOL_PAD + r0:POOL_PAD + r0 + rows, :] = u
            vext_ref[CONV_PAD + r0:CONV_PAD + r0 + rows, :] = v

            def ext_rows(back, c0, c1):
                return ext_ref[POOL_PAD + r0 - back:POOL_PAD + r0 - back + rows, c0:c1]

            def vext_rows(back):
                return vext_ref[CONV_PAD + r0 - back:CONV_PAD + r0 - back + rows, :]

            pos = (t % tiles_per_seq) * tile + r0 + jax.lax.broadcasted_iota(
                jnp.int32, (rows, POOL_GROUP), 0)
            valid = (pos + 1).astype(jnp.float32)
            pooled = []
            for gi, w in enumerate(POOL_WINDOWS):
                c0, c1 = gi * POOL_GROUP, (gi + 1) * POOL_GROUP
                win = ext_rows(0, c0, c1)
                for back in range(1, w):
                    win = win + ext_rows(back, c0, c1)
                mean = win / jnp.minimum(valid, float(w))
                pooled.append((mean - ext_rows(0, c0, c1)).astype(jnp.bfloat16))

            wc = w_conv_ref[...]
            yconv = wc[0:1] * vext_rows(2) + wc[1:2] * vext_rows(1) + wc[2:3] * vext_rows(0)
            return pooled, yconv
        return pool_conv

    starts = range(0, tile, rows)
    ys = _run_interleaved([
        _layer_stages(x_ref[r0:r0 + rows, :], p_ref[r0:r0 + rows, :], vectors, w_vmem,
                      act_ref.at[r0:r0 + rows, :], pool_conv_at(r0)) for r0 in starts])
    for r0, y in zip(starts, ys):
        y_ref[r0:r0 + rows, :] = y

    u_tail = ext_ref[tile:tile + POOL_PAD, :]
    v_tail = vext_ref[tile:tile + CONV_PAD, :]
    u_carry_ref[...] = u_tail
    v_carry_ref[...] = v_tail
    u_tail_ref[0] = u_tail
    v_tail_ref[0] = v_tail


def _sample_tile(x_ref, p_ref, pool_hist_ref, conv_hist_ref, vectors, w_vmem,
                 y_ref, pool_new_ref, conv_new_ref, act_ref, *, seqs, steps):
    w_conv_ref = vectors[2]
    n = seqs // SAMPLE_CHAINS

    def chain(q0):
        qs = slice(q0, q0 + n)
        by_step = lambda ref: jnp.concatenate([ref[qs, s, :] for s in range(steps)], axis=0)
        slab = lambda a, s: a[s * n:(s + 1) * n]

        def pool_conv(u, v):
            line = ([pool_hist_ref[s, qs, :] for s in range(POOL_HIST)]
                    + [slab(u, s) for s in range(steps)])
            pooled = []
            for gi, w in enumerate(POOL_WINDOWS):
                c0, c1 = gi * POOL_GROUP, (gi + 1) * POOL_GROUP
                per_step = []
                for s in range(steps):
                    now = POOL_HIST + s
                    win = line[now][:, c0:c1]
                    for back in range(1, w):
                        win = win + line[now - back][:, c0:c1]
                    per_step.append(win * (1.0 / w) - line[now][:, c0:c1])
                pooled.append(jnp.concatenate(per_step, axis=0).astype(jnp.bfloat16))
            for s in range(POOL_HIST):
                pool_new_ref[s, qs, :] = line[steps + s]

            vline = ([conv_hist_ref[s, qs, :] for s in range(CONV_HIST)]
                     + [slab(v, s) for s in range(steps)])
            wc = w_conv_ref[...]
            yconv = jnp.concatenate(
                [wc[0:1] * vline[s] + wc[1:2] * vline[s + 1] + wc[2:3] * vline[s + 2]
                 for s in range(steps)], axis=0)
            for s in range(CONV_HIST):
                conv_new_ref[s, qs, :] = vline[steps + s]
            return pooled, yconv

        return _layer_stages(by_step(x_ref), by_step(p_ref), vectors, w_vmem,
                             act_ref.at[q0 * steps:(q0 + n) * steps, :], pool_conv)

    starts = range(0, seqs, n)
    for q0, y in zip(starts, _run_interleaved([chain(q0) for q0 in starts])):
        for s in range(steps):
            y_ref[q0:q0 + n, s, :] = y[s * n:(s + 1) * n]


def _layer_kernel(*refs, n_prompt_tiles, tile_prompt, tiles_per_seq, seqs, steps):
    xp_ref, pp_ref, xs_ref, ps_ref, pool_hist_ref, conv_hist_ref = refs[:N_INPUTS]
    refs = refs[N_INPUTS:]
    vectors, refs = refs[:N_VECTORS], refs[N_VECTORS:]
    w_hbm, refs = refs[:N_MATRICES], refs[N_MATRICES:]
    (yp_ref, u_tail_ref, v_tail_ref, ys_ref, pool_new_ref, conv_new_ref) = refs[:N_OUTPUTS]
    refs = refs[N_OUTPUTS:]
    u_carry_ref, v_carry_ref = refs[:2]
    w_vmem = refs[2:]
    f32, bf16 = jnp.float32, jnp.bfloat16

    step = pl.program_id(0)

    @pl.when(step == 0)
    def _():
        pl.run_scoped(
            functools.partial(_prep_weights, w_hbm, w_vmem),
            pltpu.VMEM((PREP_BUFS, PREP_ROWS, PREP_COLS), f32),
            pltpu.SemaphoreType.DMA((PREP_BUFS,)))

    @pl.when(jnp.logical_and(step >= 1, step <= n_prompt_tiles))
    def _():
        pl.run_scoped(
            functools.partial(_prompt_tile, step - 1, xp_ref, pp_ref, vectors, w_vmem, yp_ref,
                              u_tail_ref, v_tail_ref, u_carry_ref, v_carry_ref,
                              tile=tile_prompt, tiles_per_seq=tiles_per_seq),
            pltpu.VMEM((POOL_PAD + tile_prompt, POOL_WIDTH), f32),
            pltpu.VMEM((CONV_PAD + tile_prompt, CONV_WIDTH), f32),
            pltpu.VMEM((tile_prompt, D_FF), bf16))

    @pl.when(step > n_prompt_tiles)
    def _():
        pl.run_scoped(
            functools.partial(_sample_tile, xs_ref, ps_ref, pool_hist_ref, conv_hist_ref, vectors,
                              w_vmem, ys_ref, pool_new_ref, conv_new_ref, seqs=seqs, steps=steps),
            pltpu.VMEM((seqs * steps, D_FF), bf16))


def _resident(shape):
    zeros = (0,) * len(shape)
    return pl.BlockSpec(shape, lambda i: zeros, pipeline_mode=pl.Buffered(1))


def _run_layer(x_p, p_p, x_s, p_s, pool_hist, conv_hist, vectors, matrices, *, seq_len, steps):
    tokens_p, tokens_s = x_p.shape[0], x_s.shape[0] * steps
    assert x_s.shape[1] == steps
    tile_p, tile_s = TILE_PROMPT, TILE_SAMPLE
    assert tokens_p % tile_p == 0 and seq_len % tile_p == 0
    assert tile_p % (PROMPT_CHAINS * SUBLANES) == 0 and tile_p // PROMPT_CHAINS >= POOL_PAD
    assert tokens_s % tile_s == 0 and tile_s % (SAMPLE_CHAINS * steps * SUBLANES) == 0
    assert len(vectors) == N_VECTORS and len(matrices) == N_MATRICES
    n_p, n_s = tokens_p // tile_p, tokens_s // tile_s
    tiles_per_seq = seq_len // tile_p
    n_seq = tokens_p // seq_len
    seqs = tile_s // steps

    tile_p_of = lambda i: jnp.clip(i - 1, 0, n_p - 1)
    tile_s_of = lambda i: jnp.clip(i - 1 - n_p, 0, n_s - 1)
    row_p = lambda width: pl.BlockSpec((tile_p, width), lambda i: (tile_p_of(i), 0))
    seq_p = lambda rows: pl.BlockSpec((1, rows, POOL_WIDTH),
                                      lambda i: (tile_p_of(i) // tiles_per_seq, 0, 0))
    row_s_in = lambda width: pl.BlockSpec((seqs, steps, width), lambda i: (tile_s_of(i), 0, 0),
                                          pipeline_mode=pl.Buffered(1))
    hist_in = lambda rows: pl.BlockSpec((rows, seqs, POOL_WIDTH), lambda i: (0, tile_s_of(i), 0),
                                        pipeline_mode=pl.Buffered(1))
    hist_out = lambda rows: pl.BlockSpec((rows, seqs, POOL_WIDTH), lambda i: (0, tile_s_of(i), 0),
                                         pipeline_mode=pl.Buffered(1))

    f32 = jnp.float32
    out_shape = [
        jax.ShapeDtypeStruct((tokens_p, D_MODEL), f32),
        jax.ShapeDtypeStruct((n_seq, POOL_PAD, POOL_WIDTH), f32),
        jax.ShapeDtypeStruct((n_seq, CONV_PAD, CONV_WIDTH), f32),
        jax.ShapeDtypeStruct((tokens_s // steps, steps, D_MODEL), f32),
        jax.ShapeDtypeStruct((POOL_HIST, tokens_s // steps, POOL_WIDTH), f32),
        jax.ShapeDtypeStruct((CONV_HIST, tokens_s // steps, CONV_WIDTH), f32),
    ]
    out_specs = [row_p(D_MODEL), seq_p(POOL_PAD), seq_p(CONV_PAD),
                 pl.BlockSpec((seqs, steps, D_MODEL), lambda i: (tile_s_of(i), 0, 0),
                              pipeline_mode=pl.Buffered(1)),
                 hist_out(POOL_HIST), hist_out(CONV_HIST)]
    scratch = [pltpu.VMEM((POOL_PAD, POOL_WIDTH), f32), pltpu.VMEM((CONV_PAD, CONV_WIDTH), f32)]
    scratch += [pltpu.VMEM(w.shape, jnp.bfloat16) for w in matrices]

    kernel = functools.partial(_layer_kernel, n_prompt_tiles=n_p, tile_prompt=tile_p,
                               tiles_per_seq=tiles_per_seq, seqs=seqs, steps=steps)
    return pl.pallas_call(
        kernel,
        grid=(1 + n_p + n_s,),
        in_specs=([row_p(D_MODEL), row_p(D_PLE), row_s_in(D_MODEL), row_s_in(D_PLE),
                   hist_in(POOL_HIST), hist_in(CONV_HIST)]
                  + [_resident(v.shape) for v in vectors]
                  + [pl.BlockSpec(memory_space=pl.ANY)] * N_MATRICES),
        out_specs=out_specs,
        out_shape=out_shape,
        scratch_shapes=scratch,
        compiler_params=pltpu.CompilerParams(
            dimension_semantics=("arbitrary",),
            vmem_limit_bytes=VMEM_LIMIT_BYTES),
        name="hybrid_layer",
    )(x_p, p_p, x_s, p_s, pool_hist, conv_hist, *vectors, *matrices)


def kernel(x_prompt, x_sample, state_pool, state_conv, p_prompt, p_sample, g_mix, w_in, w_pool_group, pool_scale, w_pool_up, w_conv, w_conv_out, w_o, g_ffn, w_ffn_in, w_ffn_out, g_ple, w_ple, w_ple_gate, g_final):
    depth = g_mix.shape[0]
    assert depth == 1, "single-layer step"
    vec = lambda a: a.reshape(1, -1)
    vectors = (vec(g_mix[0]), vec(pool_scale[0]), w_conv[0], vec(g_ffn[0]), vec(g_ple[0]),
               vec(g_final))
    matrices = (w_in[0], w_pool_group[0].reshape(POOL_WIDTH, POOL_GROUP), w_pool_up[0],
                w_conv_out[0], w_o[0], w_ffn_in[0], w_ffn_out[0], w_ple[0], w_ple_gate[0])

    batch, seq, _ = x_prompt.shape
    dec_seq = x_sample.shape[1]

    y_p, u_tail, v_tail, y_s, pool_new, conv_new = _run_layer(
        x_prompt.reshape(batch * seq, D_MODEL), p_prompt[0].reshape(batch * seq, D_PLE),
        x_sample, p_sample[0],
        jnp.swapaxes(state_pool[0], 0, 1), jnp.swapaxes(state_conv[0], 0, 1),
        vectors, matrices, seq_len=seq, steps=dec_seq)

    new_pool_prompt = u_tail[:, POOL_PAD - POOL_HIST:]
    new_conv_prompt = v_tail[:, CONV_PAD - CONV_HIST:]
    new_pool_sample = jnp.swapaxes(pool_new, 0, 1)
    new_conv_sample = jnp.swapaxes(conv_new, 0, 1)

    return (y_p.reshape(batch, seq, D_MODEL), y_s,
            new_pool_prompt[None], new_conv_prompt[None],
            new_pool_sample[None], new_conv_sample[None])
```

```python
import functools

import jax
import jax.numpy as jnp
from jax.experimental import pallas as pl
from jax.experimental.pallas import tpu as pltpu

D_MODEL = 1024
POOL_WINDOWS = (2, 4, 8, 16)
POOL_GROUP = 128
POOL_WIDTH = 512
POOL_HIST = 15
CONV_WIDTH = 512
CONV_K = 3
CONV_HIST = 2
D_FF = 2816
D_PLE = 256
EPS = 1e-6

OFF_U = 0
OFF_B = OFF_U + POOL_WIDTH
OFF_C = OFF_B + CONV_WIDTH
OFF_H = OFF_C + CONV_WIDTH
OFF_GP = OFF_H + CONV_WIDTH
OFF_GC = OFF_GP + D_MODEL

SUBLANES = 8
POOL_PAD = 16
CONV_PAD = 8
FF_CHUNK = 256
TILE_PROMPT = 512
TILE_SAMPLE = 256
PROMPT_CHAINS = 2
SAMPLE_CHAINS = 1
PREP_ROWS = 256
PREP_COLS = 1024
PREP_BUFS = 5
VMEM_LIMIT_BYTES = 60 * 1024 * 1024

N_VECTORS = 6
N_MATRICES = 9
N_INPUTS = 6
N_OUTPUTS = 6


def _rmsnorm(x, g):
    ms = jnp.mean(x * x, axis=-1, keepdims=True)
    return x * jax.lax.rsqrt(ms + EPS) * g


def _dot(a, b):
    return jnp.dot(a, b, preferred_element_type=jnp.float32)


def _prep_weights(w_hbm, w_vmem, stage_ref, sem):
    chunks = []
    for src, dst in zip(w_hbm, w_vmem):
        n_rows, n_cols = dst.shape
        for r0 in range(0, n_rows, PREP_ROWS):
            for c0 in range(0, n_cols, PREP_COLS):
                chunks.append((src, dst, r0, min(PREP_ROWS, n_rows - r0),
                               c0, min(PREP_COLS, n_cols - c0)))

    def chunk_copy(k):
        src, _, r0, nr, c0, nc = chunks[k]
        slot = k % PREP_BUFS
        return pltpu.make_async_copy(src.at[r0:r0 + nr, c0:c0 + nc],
                                     stage_ref.at[slot, :nr, :nc], sem.at[slot])

    for k in range(min(PREP_BUFS, len(chunks))):
        chunk_copy(k).start()
    for k, (_, dst, r0, nr, c0, nc) in enumerate(chunks):
        chunk_copy(k).wait()
        dst[r0:r0 + nr, c0:c0 + nc] = stage_ref[k % PREP_BUFS, :nr, :nc].astype(jnp.bfloat16)
        if k + PREP_BUFS < len(chunks):
            chunk_copy(k + PREP_BUFS).start()


def _layer_stages(x, p, vectors, w_vmem, act_ref, pool_conv):
    g_mix_ref, pool_scale_ref, _, g_ffn_ref, g_ple_ref, g_final_ref = vectors
    (w_in_ref, w_group_ref, w_pool_up_ref, w_conv_out_ref, w_o_ref, w_ffn_in_ref,
     w_ffn_out_ref, w_ple_ref, w_ple_gate_ref) = w_vmem
    bf16 = jnp.bfloat16

    hn = _rmsnorm(x, g_mix_ref[...]).astype(bf16)

    def proj(off, width):
        return _dot(hn, w_in_ref[:, off:off + width])

    u = proj(OFF_U, POOL_WIDTH)
    yield
    v = proj(OFF_C, CONV_WIDTH) * proj(OFF_H, CONV_WIDTH)
    yield
    pooled, yconv = pool_conv(u, v)
    mixed = [_dot(d, w_group_ref[gi * POOL_GROUP:(gi + 1) * POOL_GROUP, :])
             for gi, d in enumerate(pooled)]
    pool_out = (jnp.concatenate(mixed, axis=-1) * pool_scale_ref[...]).astype(bf16)
    yield
    pool_up = _dot(pool_out, w_pool_up_ref[...])
    yield
    conv_out = (proj(OFF_B, CONV_WIDTH) * yconv).astype(bf16)
    conv_up = _dot(conv_out, w_conv_out_ref[...])
    yield
    gated_pool = jax.nn.sigmoid(proj(OFF_GP, D_MODEL)) * pool_up
    yield
    merged = (gated_pool + jax.nn.sigmoid(proj(OFF_GC, D_MODEL)) * conv_up).astype(bf16)
    yield
    x = x + _dot(merged, w_o_ref[...])
    yield

    hn = _rmsnorm(x, g_ffn_ref[...]).astype(bf16)
    for c0 in range(0, D_FF, FF_CHUNK):
        gate = _dot(hn, w_ffn_in_ref[:, c0:c0 + FF_CHUNK])
        up = _dot(hn, w_ffn_in_ref[:, D_FF + c0:D_FF + c0 + FF_CHUNK])
        act_ref[:, c0:c0 + FF_CHUNK] = (jax.nn.silu(gate) * up).astype(bf16)
        yield
    x = x + _dot(act_ref[...], w_ffn_out_ref[...])
    yield

    hn = _rmsnorm(x, g_ple_ref[...]).astype(bf16)
    ple_gate = jax.nn.sigmoid(_dot(hn, w_ple_gate_ref[...]))
    yield
    x = x + ple_gate * _dot(p.astype(bf16), w_ple_ref[...])
    return _rmsnorm(x, g_final_ref[...])


def _run_interleaved(chains):
    results = [None] * len(chains)
    live = list(range(len(chains)))
    while live:
        for i in list(live):
            try:
                next(chains[i])
            except StopIteration as done:
                results[i] = done.value
                live.remove(i)
    return results


def _prompt_tile(t, x_ref, p_ref, vectors, w_vmem, y_ref, u_tail_ref, v_tail_ref,
                 u_carry_ref, v_carry_ref, ext_ref, vext_ref, act_ref, *, tile, tiles_per_seq):
    w_conv_ref = vectors[2]
    rows = tile // PROMPT_CHAINS

    @pl.when(t % tiles_per_seq == 0)
    def _():
        u_carry_ref[...] = jnp.zeros((POOL_PAD, POOL_WIDTH), jnp.float32)
        v_carry_ref[...] = jnp.zeros((CONV_PAD, CONV_WIDTH), jnp.float32)

    ext_ref[:POOL_PAD, :] = u_carry_ref[...]
    vext_ref[:CONV_PAD, :] = v_carry_ref[...]

    def pool_conv_at(r0):
        def pool_conv(u, v):
            ext_ref[POOL_PAD + r0:POOL_PAD + r0 + rows, :] = u
            vext_ref[CONV_PAD + r0:CONV_PAD + r0 + rows, :] = v

            def ext_rows(back, c0, c1):
                return ext_ref[POOL_PAD + r0 - back:POOL_PAD + r0 - back + rows, c0:c1]

            def vext_rows(back):
                return vext_ref[CONV_PAD + r0 - back:CONV_PAD + r0 - back + rows, :]

            pos = (t % tiles_per_seq) * tile + r0 + jax.lax.broadcasted_iota(
                jnp.int32, (rows, POOL_GROUP), 0)
            valid = (pos + 1).astype(jnp.float32)
            pooled = []
            for gi, w in enumerate(POOL_WINDOWS):
                c0, c1 = gi * POOL_GROUP, (gi + 1) * POOL_GROUP
                win = ext_rows(0, c0, c1)
                for back in range(1, w):
                    win = win + ext_rows(back, c0, c1)
                mean = win / jnp.minimum(valid, float(w))
                pooled.append((mean - ext_rows(0, c0, c1)).astype(jnp.bfloat16))

            wc = w_conv_ref[...]
            yconv = wc[0:1] * vext_rows(2) + wc[1:2] * vext_rows(1) + wc[2:3] * vext_rows(0)
            return pooled, yconv
        return pool_conv

    starts = range(0, tile, rows)
    ys = _run_interleaved([
        _layer_stages(x_ref[r0:r0 + rows, :], p_ref[r0:r0 + rows, :], vectors, w_vmem,
                      act_ref.at[r0:r0 + rows, :], pool_conv_at(r0)) for r0 in starts])
    for r0, y in zip(starts, ys):
        y_ref[r0:r0 + rows, :] = y

    u_tail = ext_ref[tile:tile + POOL_PAD, :]
    v_tail = vext_ref[tile:tile + CONV_PAD, :]
    u_carry_ref[...] = u_tail
    v_carry_ref[...] = v_tail
    u_tail_ref[0] = u_tail
    v_tail_ref[0] = v_tail


def _sample_tile(x_ref, p_ref, pool_hist_ref, conv_hist_ref, vectors, w_vmem,
                 y_ref, pool_new_ref, conv_new_ref, act_ref, *, seqs, steps):
    w_conv_ref = vectors[2]
    n = seqs // SAMPLE_CHAINS

    def chain(q0):
        qs = slice(q0, q0 + n)
        by_step = lambda ref: jnp.concatenate([ref[qs, s, :] for s in range(steps)], axis=0)
        slab = lambda a, s: a[s * n:(s + 1) * n]

        def pool_conv(u, v):
            line = ([pool_hist_ref[s, qs, :] for s in range(POOL_HIST)]
                    + [slab(u, s) for s in range(steps)])
            pooled = []
            for gi, w in enumerate(POOL_WINDOWS):
                c0, c1 = gi * POOL_GROUP, (gi + 1) * POOL_GROUP
                per_step = []
                for s in range(steps):
                    now = POOL_HIST + s
                    win = line[now][:, c0:c1]
                    for back in range(1, w):
                        win = win + line[now - back][:, c0:c1]
                    per_step.append(win * (1.0 / w) - line[now][:, c0:c1])
                pooled.append(jnp.concatenate(per_step, axis=0).astype(jnp.bfloat16))
            for s in range(POOL_HIST):
                pool_new_ref[s, qs, :] = line[steps + s]

            vline = ([conv_hist_ref[s, qs, :] for s in range(CONV_HIST)]
                     + [slab(v, s) for s in range(steps)])
            wc = w_conv_ref[...]
            yconv = jnp.concatenate(
                [wc[0:1] * vline[s] + wc[1:2] * vline[s + 1] + wc[2:3] * vline[s + 2]
                 for s in range(steps)], axis=0)
            for s in range(CONV_HIST):
                conv_new_ref[s, qs, :] = vline[steps + s]
            return pooled, yconv

        return _layer_stages(by_step(x_ref), by_step(p_ref), vectors, w_vmem,
                             act_ref.at[q0 * steps:(q0 + n) * steps, :], pool_conv)

    starts = range(0, seqs, n)
    for q0, y in zip(starts, _run_interleaved([chain(q0) for q0 in starts])):
        for s in range(steps):
            y_ref[q0:q0 + n, s, :] = y[s * n:(s + 1) * n]


def _layer_kernel(*refs, n_prompt_tiles, tile_prompt, tiles_per_seq, seqs, steps):
    xp_ref, pp_ref, xs_ref, ps_ref, pool_hist_ref, conv_hist_ref = refs[:N_INPUTS]
    refs = refs[N_INPUTS:]
    vectors, refs = refs[:N_VECTORS], refs[N_VECTORS:]
    w_hbm, refs = refs[:N_MATRICES], refs[N_MATRICES:]
    (yp_ref, u_tail_ref, v_tail_ref, ys_ref, pool_new_ref, conv_new_ref) = refs[:N_OUTPUTS]
    refs = refs[N_OUTPUTS:]
    u_carry_ref, v_carry_ref = refs[:2]
    w_vmem = refs[2:]
    f32, bf16 = jnp.float32, jnp.bfloat16

    step = pl.program_id(0)

    @pl.when(step == 0)
    def _():
        pl.run_scoped(
            functools.partial(_prep_weights, w_hbm, w_vmem),
            pltpu.VMEM((PREP_BUFS, PREP_ROWS, PREP_COLS), f32),
            pltpu.SemaphoreType.DMA((PREP_BUFS,)))

    @pl.when(jnp.logical_and(step >= 1, step <= n_prompt_tiles))
    def _():
        pl.run_scoped(
            functools.partial(_prompt_tile, step - 1, xp_ref, pp_ref, vectors, w_vmem, yp_ref,
                              u_tail_ref, v_tail_ref, u_carry_ref, v_carry_ref,
                              tile=tile_prompt, tiles_per_seq=tiles_per_seq),
            pltpu.VMEM((POOL_PAD + tile_prompt, POOL_WIDTH), f32),
            pltpu.VMEM((CONV_PAD + tile_prompt, CONV_WIDTH), f32),
            pltpu.VMEM((tile_prompt, D_FF), bf16))

    @pl.when(step > n_prompt_tiles)
    def _():
        pl.run_scoped(
            functools.partial(_sample_tile, xs_ref, ps_ref, pool_hist_ref, conv_hist_ref, vectors,
                              w_vmem, ys_ref, pool_new_ref, conv_new_ref, seqs=seqs, steps=steps),
            pltpu.VMEM((seqs * steps, D_FF), bf16))


def _resident(shape):
    zeros = (0,) * len(shape)
    return pl.BlockSpec(shape, lambda i: zeros, pipeline_mode=pl.Buffered(1))


def _run_layer(x_p, p_p, x_s, p_s, pool_hist, conv_hist, vectors, matrices, *, seq_len, steps):
    tokens_p, tokens_s = x_p.shape[0], x_s.shape[0] * steps
    assert x_s.shape[1] == steps
    tile_p, tile_s = TILE_PROMPT, TILE_SAMPLE
    assert tokens_p % tile_p == 0 and seq_len % tile_p == 0
    assert tile_p % (PROMPT_CHAINS * SUBLANES) == 0 and tile_p // PROMPT_CHAINS >= POOL_PAD
    assert tokens_s % tile_s == 0 and tile_s % (SAMPLE_CHAINS * steps * SUBLANES) == 0
    assert len(vectors) == N_VECTORS and len(matrices) == N_MATRICES
    n_p, n_s = tokens_p // tile_p, tokens_s // tile_s
    tiles_per_seq = seq_len // tile_p
    n_seq = tokens_p // seq_len
    seqs = tile_s // steps

    tile_p_of = lambda i: jnp.clip(i - 1, 0, n_p - 1)
    tile_s_of = lambda i: jnp.clip(i - 1 - n_p, 0, n_s - 1)
    row_p = lambda width: pl.BlockSpec((tile_p, width), lambda i: (tile_p_of(i), 0))
    seq_p = lambda rows: pl.BlockSpec((1, rows, POOL_WIDTH),
                                      lambda i: (tile_p_of(i) // tiles_per_seq, 0, 0))
    row_s_in = lambda width: pl.BlockSpec((seqs, steps, width), lambda i: (tile_s_of(i), 0, 0),
                                          pipeline_mode=pl.Buffered(1))
    hist_in = lambda rows: pl.BlockSpec((rows, seqs, POOL_WIDTH), lambda i: (0, tile_s_of(i), 0),
                                        pipeline_mode=pl.Buffered(1))
    hist_out = lambda rows: pl.BlockSpec((rows, seqs, POOL_WIDTH), lambda i: (0, tile_s_of(i), 0),
                                         pipeline_mode=pl.Buffered(1))

    f32 = jnp.float32
    out_shape = [
        jax.ShapeDtypeStruct((tokens_p, D_MODEL), f32),
        jax.ShapeDtypeStruct((n_seq, POOL_PAD, POOL_WIDTH), f32),
        jax.ShapeDtypeStruct((n_seq, CONV_PAD, CONV_WIDTH), f32),
        jax.ShapeDtypeStruct((tokens_s // steps, steps, D_MODEL), f32),
        jax.ShapeDtypeStruct((POOL_HIST, tokens_s // steps, POOL_WIDTH), f32),
        jax.ShapeDtypeStruct((CONV_HIST, tokens_s // steps, CONV_WIDTH), f32),
    ]
    out_specs = [row_p(D_MODEL), seq_p(POOL_PAD), seq_p(CONV_PAD),
                 pl.BlockSpec((seqs, steps, D_MODEL), lambda i: (tile_s_of(i), 0, 0),
                              pipeline_mode=pl.Buffered(1)),
                 hist_out(POOL_HIST), hist_out(CONV_HIST)]
    scratch = [pltpu.VMEM((POOL_PAD, POOL_WIDTH), f32), pltpu.VMEM((CONV_PAD, CONV_WIDTH), f32)]
    scratch += [pltpu.VMEM(w.shape, jnp.bfloat16) for w in matrices]

    kernel = functools.partial(_layer_kernel, n_prompt_tiles=n_p, tile_prompt=tile_p,
                               tiles_per_seq=tiles_per_seq, seqs=seqs, steps=steps)
    return pl.pallas_call(
        kernel,
        grid=(1 + n_p + n_s,),
        in_specs=([row_p(D_MODEL), row_p(D_PLE), row_s_in(D_MODEL), row_s_in(D_PLE),
                   hist_in(POOL_HIST), hist_in(CONV_HIST)]
                  + [_resident(v.shape) for v in vectors]
                  + [pl.BlockSpec(memory_space=pl.ANY)] * N_MATRICES),
        out_specs=out_specs,
        out_shape=out_shape,
        scratch_shapes=scratch,
        compiler_params=pltpu.CompilerParams(
            dimension_semantics=("arbitrary",),
            vmem_limit_bytes=VMEM_LIMIT_BYTES),
        name="hybrid_layer",
    )(x_p, p_p, x_s, p_s, pool_hist, conv_hist, *vectors, *matrices)


def kernel(x_prompt, x_sample, state_pool, state_conv, p_prompt, p_sample, g_mix, w_in, w_pool_group, pool_scale, w_pool_up, w_conv, w_conv_out, w_o, g_ffn, w_ffn_in, w_ffn_out, g_ple, w_ple, w_ple_gate, g_final):
    depth = g_mix.shape[0]
    assert depth == 1, "single-layer step"
    vec = lambda a: a.reshape(1, -1)
    vectors = (vec(g_mix[0]), vec(pool_scale[0]), w_conv[0], vec(g_ffn[0]), vec(g_ple[0]),
               vec(g_final))
    matrices = (w_in[0], w_pool_group[0].reshape(POOL_WIDTH, POOL_GROUP), w_pool_up[0],
                w_conv_out[0], w_o[0], w_ffn_in[0], w_ffn_out[0], w_ple[0], w_ple_gate[0])

    batch, seq, _ = x_prompt.shape
    dec_seq = x_sample.shape[1]

    y_p, u_tail, v_tail, y_s, pool_new, conv_new = _run_layer(
        x_prompt.reshape(batch * seq, D_MODEL), p_prompt[0].reshape(batch * seq, D_PLE),
        x_sample, p_sample[0],
        jnp.swapaxes(state_pool[0], 0, 1), jnp.swapaxes(state_conv[0], 0, 1),
        vectors, matrices, seq_len=seq, steps=dec_seq)

    new_pool_prompt = u_tail[:, POOL_PAD - POOL_HIST:]
    new_conv_prompt = v_tail[:, CONV_PAD - CONV_HIST:]
    new_pool_sample = jnp.swapaxes(pool_new, 0, 1)
    new_conv_sample = jnp.swapaxes(conv_new, 0, 1)

    return (y_p.reshape(batch, seq, D_MODEL), y_s,
            new_pool_prompt[None], new_conv_prompt[None],
            new_pool_sample[None], new_conv_sample[None])
```

```python
import functools

import jax
import jax.numpy as jnp
from jax.experimental import pallas as pl
from jax.experimental.pallas import tpu as pltpu

D_MODEL = 1024
POOL_WINDOWS = (2, 4, 8, 16)
POOL_GROUP = 128
POOL_WIDTH = 512
POOL_HIST = 15
CONV_WIDTH = 512
CONV_K = 3
CONV_HIST = 2
D_FF = 2816
D_PLE = 256
EPS = 1e-6

OFF_U = 0
OFF_B = OFF_U + POOL_WIDTH
OFF_C = OFF_B + CONV_WIDTH
OFF_H = OFF_C + CONV_WIDTH
OFF_GP = OFF_H + CONV_WIDTH
OFF_GC = OFF_GP + D_MODEL

SUBLANES = 8
POOL_PAD = 16
CONV_PAD = 8
FF_CHUNK = 256
TILE_PROMPT = 512
TILE_SAMPLE = 256
PROMPT_CHAINS = 2
SAMPLE_CHAINS = 1
PREP_ROWS = 256
PREP_COLS = 1024
PREP_BUFS = 5
VMEM_LIMIT_BYTES = 60 * 1024 * 1024

N_VECTORS = 6
N_MATRICES = 9
N_INPUTS = 6
N_OUTPUTS = 6


def _rmsnorm(x, g):
    ms = jnp.mean(x * x, axis=-1, keepdims=True)
    return x * jax.lax.rsqrt(ms + EPS) * g


def _dot(a, b):
    return jnp.dot(a, b, preferred_element_type=jnp.float32)


def _conv_tap(w_conv_ref, k):
    return w_conv_ref[:, k * CONV_WIDTH:(k + 1) * CONV_WIDTH]


def _prep_weights(w_hbm, w_vmem, stage_ref, sem):
    chunks = []
    for src, dst in zip(w_hbm, w_vmem):
        n_rows, n_cols = dst.shape
        for r0 in range(0, n_rows, PREP_ROWS):
            for c0 in range(0, n_cols, PREP_COLS):
                chunks.append((src, dst, r0, min(PREP_ROWS, n_rows - r0),
                               c0, min(PREP_COLS, n_cols - c0)))

    def chunk_copy(k):
        src, _, r0, nr, c0, nc = chunks[k]
        slot = k % PREP_BUFS
        return pltpu.make_async_copy(src.at[r0:r0 + nr, c0:c0 + nc],
                                     stage_ref.at[slot, :nr, :nc], sem.at[slot])

    for k in range(min(PREP_BUFS, len(chunks))):
        chunk_copy(k).start()
    for k, (_, dst, r0, nr, c0, nc) in enumerate(chunks):
        chunk_copy(k).wait()
        dst[r0:r0 + nr, c0:c0 + nc] = stage_ref[k % PREP_BUFS, :nr, :nc].astype(jnp.bfloat16)
        if k + PREP_BUFS < len(chunks):
            chunk_copy(k + PREP_BUFS).start()


def _layer_stages(x, p, vectors, w_vmem, act_ref, pool_conv):
    g_mix_ref, pool_scale_ref, _, g_ffn_ref, g_ple_ref, g_final_ref = vectors
    (w_in_ref, w_group_ref, w_pool_up_ref, w_conv_out_ref, w_o_ref, w_ffn_in_ref,
     w_ffn_out_ref, w_ple_ref, w_ple_gate_ref) = w_vmem
    bf16 = jnp.bfloat16

    hn = _rmsnorm(x, g_mix_ref[...]).astype(bf16)

    def proj(off, width):
        return _dot(hn, w_in_ref[:, off:off + width])

    u = proj(OFF_U, POOL_WIDTH)
    yield
    v = proj(OFF_C, CONV_WIDTH) * proj(OFF_H, CONV_WIDTH)
    yield
    pooled, yconv = pool_conv(u, v)
    mixed = [_dot(d, w_group_ref[gi * POOL_GROUP:(gi + 1) * POOL_GROUP, :])
             for gi, d in enumerate(pooled)]
    pool_out = (jnp.concatenate(mixed, axis=-1) * pool_scale_ref[...]).astype(bf16)
    yield
    pool_up = _dot(pool_out, w_pool_up_ref[...])
    yield
    conv_out = (proj(OFF_B, CONV_WIDTH) * yconv).astype(bf16)
    conv_up = _dot(conv_out, w_conv_out_ref[...])
    yield
    gated_pool = jax.nn.sigmoid(proj(OFF_GP, D_MODEL)) * pool_up
    yield
    merged = (gated_pool + jax.nn.sigmoid(proj(OFF_GC, D_MODEL)) * conv_up).astype(bf16)
    yield
    x = x + _dot(merged, w_o_ref[...])
    yield

    hn = _rmsnorm(x, g_ffn_ref[...]).astype(bf16)
    for c0 in range(0, D_FF, FF_CHUNK):
        gate = _dot(hn, w_ffn_in_ref[:, c0:c0 + FF_CHUNK])
        up = _dot(hn, w_ffn_in_ref[:, D_FF + c0:D_FF + c0 + FF_CHUNK])
        act_ref[:, c0:c0 + FF_CHUNK] = (jax.nn.silu(gate) * up).astype(bf16)
        yield
    x = x + _dot(act_ref[...], w_ffn_out_ref[...])
    yield

    hn = _rmsnorm(x, g_ple_ref[...]).astype(bf16)
    ple_gate = jax.nn.sigmoid(_dot(hn, w_ple_gate_ref[...]))
    yield
    x = x + ple_gate * _dot(p.astype(bf16), w_ple_ref[...])
    return _rmsnorm(x, g_final_ref[...])


def _run_interleaved(chains):
    results = [None] * len(chains)
    live = list(range(len(chains)))
    while live:
        for i in list(live):
            try:
                next(chains[i])
            except StopIteration as done:
                results[i] = done.value
                live.remove(i)
    return results


def _prompt_tile(t, x_ref, p_ref, vectors, w_vmem, y_ref, pool_state_ref, conv_state_ref,
                 u_carry_ref, v_carry_ref, ext_ref, vext_ref, act_ref, *, tile, tiles_per_seq):
    wc = functools.partial(_conv_tap, vectors[2])
    rows = tile // PROMPT_CHAINS

    @pl.when(t % tiles_per_seq == 0)
    def _():
        u_carry_ref[...] = jnp.zeros((POOL_PAD, POOL_WIDTH), jnp.float32)
        v_carry_ref[...] = jnp.zeros((CONV_PAD, CONV_WIDTH), jnp.float32)

    ext_ref[:POOL_PAD, :] = u_carry_ref[...]
    vext_ref[:CONV_PAD, :] = v_carry_ref[...]

    def pool_conv_at(r0):
        def pool_conv(u, v):
            ext_ref[POOL_PAD + r0:POOL_PAD + r0 + rows, :] = u
            vext_ref[CONV_PAD + r0:CONV_PAD + r0 + rows, :] = v

            def ext_rows(back, c0, c1):
                return ext_ref[POOL_PAD + r0 - back:POOL_PAD + r0 - back + rows, c0:c1]

            def vext_rows(back):
                return vext_ref[CONV_PAD + r0 - back:CONV_PAD + r0 - back + rows, :]

            pos = (t % tiles_per_seq) * tile + r0 + jax.lax.broadcasted_iota(
                jnp.int32, (rows, POOL_GROUP), 0)
            valid = (pos + 1).astype(jnp.float32)
            pooled = []
            for gi, w in enumerate(POOL_WINDOWS):
                c0, c1 = gi * POOL_GROUP, (gi + 1) * POOL_GROUP
                win = ext_rows(0, c0, c1)
                for back in range(1, w):
                    win = win + ext_rows(back, c0, c1)
                mean = win / jnp.minimum(valid, float(w))
                pooled.append((mean - ext_rows(0, c0, c1)).astype(jnp.bfloat16))

            yconv = wc(0) * vext_rows(2) + wc(1) * vext_rows(1) + wc(2) * vext_rows(0)
            return pooled, yconv
        return pool_conv

    starts = range(0, tile, rows)
    ys = _run_interleaved([
        _layer_stages(x_ref[r0:r0 + rows, :], p_ref[r0:r0 + rows, :], vectors, w_vmem,
                      act_ref.at[r0:r0 + rows, :], pool_conv_at(r0)) for r0 in starts])
    for r0, y in zip(starts, ys):
        y_ref[r0:r0 + rows, :] = y

    u_tail = ext_ref[tile:tile + POOL_PAD, :]
    v_tail = vext_ref[tile:tile + CONV_PAD, :]
    u_carry_ref[...] = u_tail
    v_carry_ref[...] = v_tail
    seq = t // tiles_per_seq
    pool_state_ref[:, pl.ds(seq, 1), :] = u_tail[POOL_PAD - POOL_HIST:][:, None, :]
    conv_state_ref[0] = v_tail[CONV_PAD - CONV_HIST:]


def _sample_tile(x_ref, p_ref, pool_hist_ref, conv_hist_ref, vectors, w_vmem,
                 y_ref, pool_new_ref, conv_new_ref, act_ref, *, seqs, steps):
    wc = functools.partial(_conv_tap, vectors[2])
    n = seqs // SAMPLE_CHAINS

    def chain(q0):
        qs = slice(q0, q0 + n)
        by_step = lambda ref: jnp.concatenate([ref[qs, s, :] for s in range(steps)], axis=0)
        slab = lambda a, s: a[s * n:(s + 1) * n]

        def pool_conv(u, v):
            line = ([pool_hist_ref[s, qs, :] for s in range(POOL_HIST)]
                    + [slab(u, s) for s in range(steps)])
            pooled = []
            for gi, w in enumerate(POOL_WINDOWS):
                c0, c1 = gi * POOL_GROUP, (gi + 1) * POOL_GROUP
                per_step = []
                for s in range(steps):
                    now = POOL_HIST + s
                    win = line[now][:, c0:c1]
                    for back in range(1, w):
                        win = win + line[now - back][:, c0:c1]
                    per_step.append(win * (1.0 / w) - line[now][:, c0:c1])
                pooled.append(jnp.concatenate(per_step, axis=0).astype(jnp.bfloat16))
            for s in range(POOL_HIST):
                pool_new_ref[s, qs, :] = line[steps + s]

            vline = ([conv_hist_ref[qs, s, :] for s in range(CONV_HIST)]
                     + [slab(v, s) for s in range(steps)])
            yconv = jnp.concatenate(
                [wc(0) * vline[s] + wc(1) * vline[s + 1] + wc(2) * vline[s + 2]
                 for s in range(steps)], axis=0)
            for s in range(CONV_HIST):
                conv_new_ref[qs, s, :] = vline[steps + s]
            return pooled, yconv

        return _layer_stages(by_step(x_ref), by_step(p_ref), vectors, w_vmem,
                             act_ref.at[q0 * steps:(q0 + n) * steps, :], pool_conv)

    starts = range(0, seqs, n)
    for q0, y in zip(starts, _run_interleaved([chain(q0) for q0 in starts])):
        for s in range(steps):
            y_ref[q0:q0 + n, s, :] = y[s * n:(s + 1) * n]


def _layer_kernel(*refs, n_prompt_tiles, tile_prompt, tiles_per_seq, seqs, steps):
    xp_ref, pp_ref, xs_ref, ps_ref, pool_hist_ref, conv_hist_ref = refs[:N_INPUTS]
    refs = refs[N_INPUTS:]
    vectors, refs = refs[:N_VECTORS], refs[N_VECTORS:]
    w_hbm, refs = refs[:N_MATRICES], refs[N_MATRICES:]
    (yp_ref, pool_state_ref, conv_state_ref, ys_ref, pool_new_ref,
     conv_new_ref) = refs[:N_OUTPUTS]
    refs = refs[N_OUTPUTS:]
    u_carry_ref, v_carry_ref = refs[:2]
    w_vmem = refs[2:]
    f32, bf16 = jnp.float32, jnp.bfloat16

    step = pl.program_id(0)

    @pl.when(step == 0)
    def _():
        pl.run_scoped(
            functools.partial(_prep_weights, w_hbm, w_vmem),
            pltpu.VMEM((PREP_BUFS, PREP_ROWS, PREP_COLS), f32),
            pltpu.SemaphoreType.DMA((PREP_BUFS,)))

    @pl.when(jnp.logical_and(step >= 1, step <= n_prompt_tiles))
    def _():
        pl.run_scoped(
            functools.partial(_prompt_tile, step - 1, xp_ref, pp_ref, vectors, w_vmem, yp_ref,
                              pool_state_ref, conv_state_ref, u_carry_ref, v_carry_ref,
                              tile=tile_prompt, tiles_per_seq=tiles_per_seq),
            pltpu.VMEM((POOL_PAD + tile_prompt, POOL_WIDTH), f32),
            pltpu.VMEM((CONV_PAD + tile_prompt, CONV_WIDTH), f32),
            pltpu.VMEM((tile_prompt, D_FF), bf16))

    @pl.when(step > n_prompt_tiles)
    def _():
        pl.run_scoped(
            functools.partial(_sample_tile, xs_ref, ps_ref, pool_hist_ref, conv_hist_ref, vectors,
                              w_vmem, ys_ref, pool_new_ref, conv_new_ref, seqs=seqs, steps=steps),
            pltpu.VMEM((seqs * steps, D_FF), bf16))


def _resident(shape):
    zeros = (0,) * len(shape)
    return pl.BlockSpec(shape, lambda i: zeros, pipeline_mode=pl.Buffered(1))


def _run_layer(x_p, p_p, x_s, p_s, pool_hist, conv_hist, vectors, matrices, *, seq_len, steps):
    tokens_p, tokens_s = x_p.shape[0], x_s.shape[0] * steps
    assert x_s.shape[1] == steps
    tile_p, tile_s = TILE_PROMPT, TILE_SAMPLE
    assert tokens_p % tile_p == 0 and seq_len % tile_p == 0
    assert tile_p % (PROMPT_CHAINS * SUBLANES) == 0 and tile_p // PROMPT_CHAINS >= POOL_PAD
    assert tokens_s % tile_s == 0 and tile_s % (SAMPLE_CHAINS * steps * SUBLANES) == 0
    assert len(vectors) == N_VECTORS and len(matrices) == N_MATRICES
    n_p, n_s = tokens_p // tile_p, tokens_s // tile_s
    tiles_per_seq = seq_len // tile_p
    n_seq = tokens_p // seq_len
    seqs = tile_s // steps

    tile_p_of = lambda i: jnp.clip(i - 1, 0, n_p - 1)
    tile_s_of = lambda i: jnp.clip(i - 1 - n_p, 0, n_s - 1)
    row_p = lambda width: pl.BlockSpec((tile_p, width), lambda i: (tile_p_of(i), 0))
    pool_state_p = pl.BlockSpec((POOL_HIST, n_seq, POOL_WIDTH), lambda i: (0, 0, 0))
    conv_state_p = pl.BlockSpec((1, CONV_HIST, CONV_WIDTH),
                                lambda i: (tile_p_of(i) // tiles_per_seq, 0, 0))
    row_s_in = lambda width: pl.BlockSpec((seqs, steps, width), lambda i: (tile_s_of(i), 0, 0),
                                          pipeline_mode=pl.Buffered(1))
    pool_s = pl.BlockSpec((POOL_HIST, seqs, POOL_WIDTH), lambda i: (0, tile_s_of(i), 0),
                          pipeline_mode=pl.Buffered(1))
    conv_s = pl.BlockSpec((seqs, CONV_HIST, CONV_WIDTH), lambda i: (tile_s_of(i), 0, 0),
                          pipeline_mode=pl.Buffered(1))

    f32 = jnp.float32
    out_shape = [
        jax.ShapeDtypeStruct((tokens_p, D_MODEL), f32),
        jax.ShapeDtypeStruct((POOL_HIST, n_seq, POOL_WIDTH), f32),
        jax.ShapeDtypeStruct((n_seq, CONV_HIST, CONV_WIDTH), f32),
        jax.ShapeDtypeStruct((tokens_s // steps, steps, D_MODEL), f32),
        jax.ShapeDtypeStruct((POOL_HIST, tokens_s // steps, POOL_WIDTH), f32),
        jax.ShapeDtypeStruct((tokens_s // steps, CONV_HIST, CONV_WIDTH), f32),
    ]
    out_specs = [row_p(D_MODEL), pool_state_p, conv_state_p,
                 pl.BlockSpec((seqs, steps, D_MODEL), lambda i: (tile_s_of(i), 0, 0),
                              pipeline_mode=pl.Buffered(1)),
                 pool_s, conv_s]
    scratch = [pltpu.VMEM((POOL_PAD, POOL_WIDTH), f32), pltpu.VMEM((CONV_PAD, CONV_WIDTH), f32)]
    scratch += [pltpu.VMEM(w.shape, jnp.bfloat16) for w in matrices]

    kernel = functools.partial(_layer_kernel, n_prompt_tiles=n_p, tile_prompt=tile_p,
                               tiles_per_seq=tiles_per_seq, seqs=seqs, steps=steps)
    return pl.pallas_call(
        kernel,
        grid=(1 + n_p + n_s,),
        in_specs=([row_p(D_MODEL), row_p(D_PLE), row_s_in(D_MODEL), row_s_in(D_PLE),
                   pool_s, conv_s]
                  + [_resident(v.shape) for v in vectors]
                  + [pl.BlockSpec(memory_space=pl.ANY)] * N_MATRICES),
        out_specs=out_specs,
        out_shape=out_shape,
        scratch_shapes=scratch,
        compiler_params=pltpu.CompilerParams(
            dimension_semantics=("arbitrary",),
            vmem_limit_bytes=VMEM_LIMIT_BYTES),
        name="hybrid_layer",
    )(x_p, p_p, x_s, p_s, pool_hist, conv_hist, *vectors, *matrices)


def kernel(x_prompt, x_sample, state_pool, state_conv, p_prompt, p_sample, g_mix, w_in, w_pool_group, pool_scale, w_pool_up, w_conv, w_conv_out, w_o, g_ffn, w_ffn_in, w_ffn_out, g_ple, w_ple, w_ple_gate, g_final):
    depth = g_mix.shape[0]
    assert depth == 1, "single-layer step"
    vec = lambda a: a.reshape(1, -1)
    vectors = (vec(g_mix[0]), vec(pool_scale[0]), vec(w_conv[0]), vec(g_ffn[0]), vec(g_ple[0]),
               vec(g_final))
    matrices = (w_in[0], w_pool_group[0].reshape(POOL_WIDTH, POOL_GROUP), w_pool_up[0],
                w_conv_out[0], w_o[0], w_ffn_in[0], w_ffn_out[0], w_ple[0], w_ple_gate[0])

    batch, seq, _ = x_prompt.shape
    dec_seq = x_sample.shape[1]

    y_p, pool_state_p, conv_state_p, y_s, pool_state_s, conv_state_s = _run_layer(
        x_prompt.reshape(batch * seq, D_MODEL), p_prompt[0].reshape(batch * seq, D_PLE),
        x_sample, p_sample[0],
        jnp.swapaxes(state_pool[0], 0, 1), state_conv[0],
        vectors, matrices, seq_len=seq, steps=dec_seq)

    return (y_p.reshape(batch, seq, D_MODEL), y_s,
            jnp.swapaxes(pool_state_p, 0, 1)[None], conv_state_p[None],
            jnp.swapaxes(pool_state_s, 0, 1)[None], conv_state_s[None])
```

```python
import functools

import jax
import jax.numpy as jnp
from jax.experimental import pallas as pl
from jax.experimental.pallas import tpu as pltpu

D_MODEL = 1024
POOL_WINDOWS = (2, 4, 8, 16)
POOL_GROUP = 128
POOL_WIDTH = 512
POOL_HIST = 15
CONV_WIDTH = 512
CONV_K = 3
CONV_HIST = 2
D_FF = 2816
D_PLE = 256
EPS = 1e-6

OFF_U = 0
OFF_B = OFF_U + POOL_WIDTH
OFF_C = OFF_B + CONV_WIDTH
OFF_H = OFF_C + CONV_WIDTH
OFF_GP = OFF_H + CONV_WIDTH
OFF_GC = OFF_GP + D_MODEL

SUBLANES = 8
POOL_PAD = 16
CONV_PAD = 8
FF_CHUNK = 256
TILE_PROMPT = 512
TILE_SAMPLE = 256
PROMPT_CHAINS = 2
SAMPLE_CHAINS = 2
PREP_ROWS = 256
PREP_COLS = 1024
PREP_BUFS = 8
VMEM_LIMIT_BYTES = 60 * 1024 * 1024

N_VECTORS = 6
N_MATRICES = 9
N_INPUTS = 6
N_OUTPUTS = 6


def _rmsnorm(x, g):
    ms = jnp.mean(x * x, axis=-1, keepdims=True)
    return x * jax.lax.rsqrt(ms + EPS) * g


def _dot(a, b):
    return jnp.dot(a, b, preferred_element_type=jnp.float32)


def _conv_tap(w_conv_ref, k):
    return w_conv_ref[:, k * CONV_WIDTH:(k + 1) * CONV_WIDTH]


def _prep_weights(w_hbm, w_vmem, stage_ref, sem):
    chunks = []
    for src, dst in zip(w_hbm, w_vmem):
        n_rows, n_cols = dst.shape
        for r0 in range(0, n_rows, PREP_ROWS):
            for c0 in range(0, n_cols, PREP_COLS):
                chunks.append((src, dst, r0, min(PREP_ROWS, n_rows - r0),
                               c0, min(PREP_COLS, n_cols - c0)))

    def chunk_copy(k):
        src, _, r0, nr, c0, nc = chunks[k]
        slot = k % PREP_BUFS
        return pltpu.make_async_copy(src.at[r0:r0 + nr, c0:c0 + nc],
                                     stage_ref.at[slot, :nr, :nc], sem.at[slot])

    for k in range(min(PREP_BUFS, len(chunks))):
        chunk_copy(k).start()
    for k, (_, dst, r0, nr, c0, nc) in enumerate(chunks):
        chunk_copy(k).wait()
        dst[r0:r0 + nr, c0:c0 + nc] = stage_ref[k % PREP_BUFS, :nr, :nc].astype(jnp.bfloat16)
        if k + PREP_BUFS < len(chunks):
            chunk_copy(k + PREP_BUFS).start()


def _layer_stages(x, p, vectors, w_vmem, act_ref, pool_conv):
    g_mix_ref, pool_scale_ref, _, g_ffn_ref, g_ple_ref, g_final_ref = vectors
    (w_in_ref, w_group_ref, w_pool_up_ref, w_conv_out_ref, w_o_ref, w_ffn_in_ref,
     w_ffn_out_ref, w_ple_ref, w_ple_gate_ref) = w_vmem
    bf16 = jnp.bfloat16

    hn = _rmsnorm(x, g_mix_ref[...]).astype(bf16)

    def proj(off, width):
        return _dot(hn, w_in_ref[:, off:off + width])

    u = proj(OFF_U, POOL_WIDTH)
    yield
    v = proj(OFF_C, CONV_WIDTH) * proj(OFF_H, CONV_WIDTH)
    yield
    pooled, yconv = pool_conv(u, v)
    mixed = [_dot(d, w_group_ref[gi * POOL_GROUP:(gi + 1) * POOL_GROUP, :])
             for gi, d in enumerate(pooled)]
    pool_out = (jnp.concatenate(mixed, axis=-1) * pool_scale_ref[...]).astype(bf16)
    yield
    pool_up = _dot(pool_out, w_pool_up_ref[...])
    yield
    conv_out = (proj(OFF_B, CONV_WIDTH) * yconv).astype(bf16)
    conv_up = _dot(conv_out, w_conv_out_ref[...])
    yield
    gated_pool = jax.nn.sigmoid(proj(OFF_GP, D_MODEL)) * pool_up
    yield
    merged = (gated_pool + jax.nn.sigmoid(proj(OFF_GC, D_MODEL)) * conv_up).astype(bf16)
    yield
    x = x + _dot(merged, w_o_ref[...])
    yield

    hn = _rmsnorm(x, g_ffn_ref[...]).astype(bf16)
    for c0 in range(0, D_FF, FF_CHUNK):
        gate = _dot(hn, w_ffn_in_ref[:, c0:c0 + FF_CHUNK])
        up = _dot(hn, w_ffn_in_ref[:, D_FF + c0:D_FF + c0 + FF_CHUNK])
        act_ref[:, c0:c0 + FF_CHUNK] = (jax.nn.silu(gate) * up).astype(bf16)
        yield
    x = x + _dot(act_ref[...], w_ffn_out_ref[...])
    yield

    hn = _rmsnorm(x, g_ple_ref[...]).astype(bf16)
    ple_gate = jax.nn.sigmoid(_dot(hn, w_ple_gate_ref[...]))
    yield
    x = x + ple_gate * _dot(p.astype(bf16), w_ple_ref[...])
    return _rmsnorm(x, g_final_ref[...])


def _run_interleaved(chains):
    results = [None] * len(chains)
    live = list(range(len(chains)))
    while live:
        for i in list(live):
            try:
                next(chains[i])
            except StopIteration as done:
                results[i] = done.value
                live.remove(i)
    return results


def _prompt_tile(t, x_ref, p_ref, vectors, w_vmem, y_ref, pool_state_ref, conv_state_ref,
                 u_carry_ref, v_carry_ref, ext_ref, vext_ref, act_ref, *, tile, tiles_per_seq):
    wc = functools.partial(_conv_tap, vectors[2])
    rows = tile // PROMPT_CHAINS

    @pl.when(t % tiles_per_seq == 0)
    def _():
        u_carry_ref[...] = jnp.zeros((POOL_PAD, POOL_WIDTH), jnp.float32)
        v_carry_ref[...] = jnp.zeros((CONV_PAD, CONV_WIDTH), jnp.float32)

    ext_ref[:POOL_PAD, :] = u_carry_ref[...]
    vext_ref[:CONV_PAD, :] = v_carry_ref[...]

    def pool_conv_at(r0):
        def pool_conv(u, v):
            ext_ref[POOL_PAD + r0:POOL_PAD + r0 + rows, :] = u
            vext_ref[CONV_PAD + r0:CONV_PAD + r0 + rows, :] = v

            def ext_rows(back, c0, c1):
                return ext_ref[POOL_PAD + r0 - back:POOL_PAD + r0 - back + rows, c0:c1]

            def vext_rows(back):
                return vext_ref[CONV_PAD + r0 - back:CONV_PAD + r0 - back + rows, :]

            pos = (t % tiles_per_seq) * tile + r0 + jax.lax.broadcasted_iota(
                jnp.int32, (rows, POOL_GROUP), 0)
            valid = (pos + 1).astype(jnp.float32)
            pooled = []
            for gi, w in enumerate(POOL_WINDOWS):
                c0, c1 = gi * POOL_GROUP, (gi + 1) * POOL_GROUP
                win = ext_rows(0, c0, c1)
                for back in range(1, w):
                    win = win + ext_rows(back, c0, c1)
                mean = win / jnp.minimum(valid, float(w))
                pooled.append((mean - ext_rows(0, c0, c1)).astype(jnp.bfloat16))

            yconv = wc(0) * vext_rows(2) + wc(1) * vext_rows(1) + wc(2) * vext_rows(0)
            return pooled, yconv
        return pool_conv

    starts = range(0, tile, rows)
    ys = _run_interleaved([
        _layer_stages(x_ref[r0:r0 + rows, :], p_ref[r0:r0 + rows, :], vectors, w_vmem,
                      act_ref.at[r0:r0 + rows, :], pool_conv_at(r0)) for r0 in starts])
    for r0, y in zip(starts, ys):
        y_ref[r0:r0 + rows, :] = y

    u_tail = ext_ref[tile:tile + POOL_PAD, :]
    v_tail = vext_ref[tile:tile + CONV_PAD, :]
    u_carry_ref[...] = u_tail
    v_carry_ref[...] = v_tail
    seq = t // tiles_per_seq
    pool_state_ref[:, pl.ds(seq, 1), :] = u_tail[POOL_PAD - POOL_HIST:][:, None, :]
    conv_state_ref[0] = v_tail[CONV_PAD - CONV_HIST:]


def _sample_tile(x_ref, p_ref, pool_hist_ref, conv_hist_ref, vectors, w_vmem,
                 y_ref, pool_new_ref, conv_new_ref, act_ref, *, seqs, steps):
    wc = functools.partial(_conv_tap, vectors[2])
    n = seqs // SAMPLE_CHAINS

    def chain(q0):
        qs = slice(q0, q0 + n)
        by_step = lambda ref: jnp.concatenate([ref[qs, s, :] for s in range(steps)], axis=0)
        slab = lambda a, s: a[s * n:(s + 1) * n]

        def pool_conv(u, v):
            line = ([pool_hist_ref[s, qs, :] for s in range(POOL_HIST)]
                    + [slab(u, s) for s in range(steps)])
            pooled = []
            for gi, w in enumerate(POOL_WINDOWS):
                c0, c1 = gi * POOL_GROUP, (gi + 1) * POOL_GROUP
                per_step = []
                for s in range(steps):
                    now = POOL_HIST + s
                    win = line[now][:, c0:c1]
                    for back in range(1, w):
                        win = win + line[now - back][:, c0:c1]
                    per_step.append(win * (1.0 / w) - line[now][:, c0:c1])
                pooled.append(jnp.concatenate(per_step, axis=0).astype(jnp.bfloat16))
            for s in range(POOL_HIST):
                pool_new_ref[s, qs, :] = line[steps + s]

            vline = ([conv_hist_ref[qs, s, :] for s in range(CONV_HIST)]
                     + [slab(v, s) for s in range(steps)])
            yconv = jnp.concatenate(
                [wc(0) * vline[s] + wc(1) * vline[s + 1] + wc(2) * vline[s + 2]
                 for s in range(steps)], axis=0)
            for s in range(CONV_HIST):
                conv_new_ref[qs, s, :] = vline[steps + s]
            return pooled, yconv

        return _layer_stages(by_step(x_ref), by_step(p_ref), vectors, w_vmem,
                             act_ref.at[q0 * steps:(q0 + n) * steps, :], pool_conv)

    starts = range(0, seqs, n)
    for q0, y in zip(starts, _run_interleaved([chain(q0) for q0 in starts])):
        for s in range(steps):
            y_ref[q0:q0 + n, s, :] = y[s * n:(s + 1) * n]


def _layer_kernel(*refs, n_prompt_tiles, tile_prompt, tiles_per_seq, seqs, steps):
    xp_ref, pp_ref, xs_ref, ps_ref, pool_hist_ref, conv_hist_ref = refs[:N_INPUTS]
    refs = refs[N_INPUTS:]
    vectors, refs = refs[:N_VECTORS], refs[N_VECTORS:]
    w_hbm, refs = refs[:N_MATRICES], refs[N_MATRICES:]
    (yp_ref, pool_state_ref, conv_state_ref, ys_ref, pool_new_ref,
     conv_new_ref) = refs[:N_OUTPUTS]
    refs = refs[N_OUTPUTS:]
    u_carry_ref, v_carry_ref = refs[:2]
    w_vmem = refs[2:]
    f32, bf16 = jnp.float32, jnp.bfloat16

    step = pl.program_id(0)

    @pl.when(step == 0)
    def _():
        pl.run_scoped(
            functools.partial(_prep_weights, w_hbm, w_vmem),
            pltpu.VMEM((PREP_BUFS, PREP_ROWS, PREP_COLS), f32),
            pltpu.SemaphoreType.DMA((PREP_BUFS,)))

    @pl.when(jnp.logical_and(step >= 1, step <= n_prompt_tiles))
    def _():
        pl.run_scoped(
            functools.partial(_prompt_tile, step - 1, xp_ref, pp_ref, vectors, w_vmem, yp_ref,
                              pool_state_ref, conv_state_ref, u_carry_ref, v_carry_ref,
                              tile=tile_prompt, tiles_per_seq=tiles_per_seq),
            pltpu.VMEM((POOL_PAD + tile_prompt, POOL_WIDTH), f32),
            pltpu.VMEM((CONV_PAD + tile_prompt, CONV_WIDTH), f32),
            pltpu.VMEM((tile_prompt, D_FF), bf16))

    @pl.when(step > n_prompt_tiles)
    def _():
        pl.run_scoped(
            functools.partial(_sample_tile, xs_ref, ps_ref, pool_hist_ref, conv_hist_ref, vectors,
                              w_vmem, ys_ref, pool_new_ref, conv_new_ref, seqs=seqs, steps=steps),
            pltpu.VMEM((seqs * steps, D_FF), bf16))


def _resident(shape):
    zeros = (0,) * len(shape)
    return pl.BlockSpec(shape, lambda i: zeros, pipeline_mode=pl.Buffered(1))


def _run_layer(x_p, p_p, x_s, p_s, pool_hist, conv_hist, vectors, matrices, *, seq_len, steps):
    tokens_p, tokens_s = x_p.shape[0], x_s.shape[0] * steps
    assert x_s.shape[1] == steps
    tile_p, tile_s = TILE_PROMPT, TILE_SAMPLE
    assert tokens_p % tile_p == 0 and seq_len % tile_p == 0
    assert tile_p % (PROMPT_CHAINS * SUBLANES) == 0 and tile_p // PROMPT_CHAINS >= POOL_PAD
    assert tokens_s % tile_s == 0 and tile_s % (SAMPLE_CHAINS * steps * SUBLANES) == 0
    assert len(vectors) == N_VECTORS and len(matrices) == N_MATRICES
    n_p, n_s = tokens_p // tile_p, tokens_s // tile_s
    tiles_per_seq = seq_len // tile_p
    n_seq = tokens_p // seq_len
    seqs = tile_s // steps

    tile_p_of = lambda i: jnp.clip(i - 1, 0, n_p - 1)
    tile_s_of = lambda i: jnp.clip(i - 1 - n_p, 0, n_s - 1)
    row_p = lambda width: pl.BlockSpec((tile_p, width), lambda i: (tile_p_of(i), 0))
    pool_state_p = pl.BlockSpec((POOL_HIST, n_seq, POOL_WIDTH), lambda i: (0, 0, 0))
    conv_state_p = pl.BlockSpec((1, CONV_HIST, CONV_WIDTH),
                                lambda i: (tile_p_of(i) // tiles_per_seq, 0, 0))
    row_s_in = lambda width: pl.BlockSpec((seqs, steps, width), lambda i: (tile_s_of(i), 0, 0),
                                          pipeline_mode=pl.Buffered(1))
    pool_s = pl.BlockSpec((POOL_HIST, seqs, POOL_WIDTH), lambda i: (0, tile_s_of(i), 0),
                          pipeline_mode=pl.Buffered(1))
    conv_s = pl.BlockSpec((seqs, CONV_HIST, CONV_WIDTH), lambda i: (tile_s_of(i), 0, 0),
                          pipeline_mode=pl.Buffered(1))

    f32 = jnp.float32
    out_shape = [
        jax.ShapeDtypeStruct((tokens_p, D_MODEL), f32),
        jax.ShapeDtypeStruct((POOL_HIST, n_seq, POOL_WIDTH), f32),
        jax.ShapeDtypeStruct((n_seq, CONV_HIST, CONV_WIDTH), f32),
        jax.ShapeDtypeStruct((tokens_s // steps, steps, D_MODEL), f32),
        jax.ShapeDtypeStruct((POOL_HIST, tokens_s // steps, POOL_WIDTH), f32),
        jax.ShapeDtypeStruct((tokens_s // steps, CONV_HIST, CONV_WIDTH), f32),
    ]
    out_specs = [row_p(D_MODEL), pool_state_p, conv_state_p,
                 pl.BlockSpec((seqs, steps, D_MODEL), lambda i: (tile_s_of(i), 0, 0),
                              pipeline_mode=pl.Buffered(1)),
                 pool_s, conv_s]
    scratch = [pltpu.VMEM((POOL_PAD, POOL_WIDTH), f32), pltpu.VMEM((CONV_PAD, CONV_WIDTH), f32)]
    scratch += [pltpu.VMEM(w.shape, jnp.bfloat16) for w in matrices]

    kernel = functools.partial(_layer_kernel, n_prompt_tiles=n_p, tile_prompt=tile_p,
                               tiles_per_seq=tiles_per_seq, seqs=seqs, steps=steps)
    return pl.pallas_call(
        kernel,
        grid=(1 + n_p + n_s,),
        in_specs=([row_p(D_MODEL), row_p(D_PLE), row_s_in(D_MODEL), row_s_in(D_PLE),
                   pool_s, conv_s]
                  + [_resident(v.shape) for v in vectors]
                  + [pl.BlockSpec(memory_space=pl.ANY)] * N_MATRICES),
        out_specs=out_specs,
        out_shape=out_shape,
        scratch_shapes=scratch,
        compiler_params=pltpu.CompilerParams(
            dimension_semantics=("arbitrary",),
            vmem_limit_bytes=VMEM_LIMIT_BYTES),
        name="hybrid_layer",
    )(x_p, p_p, x_s, p_s, pool_hist, conv_hist, *vectors, *matrices)


def kernel(x_prompt, x_sample, state_pool, state_conv, p_prompt, p_sample, g_mix, w_in, w_pool_group, pool_scale, w_pool_up, w_conv, w_conv_out, w_o, g_ffn, w_ffn_in, w_ffn_out, g_ple, w_ple, w_ple_gate, g_final):
    depth = g_mix.shape[0]
    assert depth == 1, "single-layer step"
    vec = lambda a: a.reshape(1, -1)
    vectors = (vec(g_mix[0]), vec(pool_scale[0]), vec(w_conv[0]), vec(g_ffn[0]), vec(g_ple[0]),
               vec(g_final))
    matrices = (w_in[0], w_pool_group[0].reshape(POOL_WIDTH, POOL_GROUP), w_pool_up[0],
                w_conv_out[0], w_o[0], w_ffn_in[0], w_ffn_out[0], w_ple[0], w_ple_gate[0])

    batch, seq, _ = x_prompt.shape
    dec_seq = x_sample.shape[1]

    y_p, pool_state_p, conv_state_p, y_s, pool_state_s, conv_state_s = _run_layer(
        x_prompt.reshape(batch * seq, D_MODEL), p_prompt[0].reshape(batch * seq, D_PLE),
        x_sample, p_sample[0],
        jnp.swapaxes(state_pool[0], 0, 1), state_conv[0],
        vectors, matrices, seq_len=seq, steps=dec_seq)

    return (y_p.reshape(batch, seq, D_MODEL), y_s,
            jnp.swapaxes(pool_state_p, 0, 1)[None], conv_state_p[None],
            jnp.swapaxes(pool_state_s, 0, 1)[None], conv_state_s[None])
```

```python
import functools

import jax
import jax.numpy as jnp
from jax.experimental import pallas as pl
from jax.experimental.pallas import tpu as pltpu

D_MODEL = 1024
POOL_WINDOWS = (2, 4, 8, 16)
POOL_GROUP = 128
POOL_WIDTH = 512
POOL_HIST = 15
CONV_WIDTH = 512
CONV_K = 3
CONV_HIST = 2
D_FF = 2816
D_PLE = 256
EPS = 1e-6

OFF_U = 0
OFF_B = OFF_U + POOL_WIDTH
OFF_C = OFF_B + CONV_WIDTH
OFF_H = OFF_C + CONV_WIDTH
OFF_GP = OFF_H + CONV_WIDTH
OFF_GC = OFF_GP + D_MODEL

SUBLANES = 8
POOL_PAD = 16
CONV_PAD = 8
FF_CHUNK = 256
TILE_PROMPT = 512
TILE_SAMPLE = 256
PROMPT_CHAINS = 2
SAMPLE_CHAINS = 2
PROMPT_LAG = 1
SAMPLE_LAG = 0
PREP_ROWS = 256
PREP_COLS = 1024
PREP_BUFS = 8
VMEM_LIMIT_BYTES = 60 * 1024 * 1024

N_VECTORS = 6
N_MATRICES = 9
N_INPUTS = 6
N_OUTPUTS = 6


def _rmsnorm(x, g):
    ms = jnp.mean(x * x, axis=-1, keepdims=True)
    return x * jax.lax.rsqrt(ms + EPS) * g


def _dot(a, b):
    return jnp.dot(a, b, preferred_element_type=jnp.float32)


def _conv_tap(w_conv_ref, k):
    return w_conv_ref[:, k * CONV_WIDTH:(k + 1) * CONV_WIDTH]


def _prep_weights(w_hbm, w_vmem, stage_ref, sem):
    chunks = []
    for src, dst in zip(w_hbm, w_vmem):
        n_rows, n_cols = dst.shape
        for r0 in range(0, n_rows, PREP_ROWS):
            for c0 in range(0, n_cols, PREP_COLS):
                chunks.append((src, dst, r0, min(PREP_ROWS, n_rows - r0),
                               c0, min(PREP_COLS, n_cols - c0)))

    def chunk_copy(k):
        src, _, r0, nr, c0, nc = chunks[k]
        slot = k % PREP_BUFS
        return pltpu.make_async_copy(src.at[r0:r0 + nr, c0:c0 + nc],
                                     stage_ref.at[slot, :nr, :nc], sem.at[slot])

    for k in range(min(PREP_BUFS, len(chunks))):
        chunk_copy(k).start()
    for k, (_, dst, r0, nr, c0, nc) in enumerate(chunks):
        chunk_copy(k).wait()
        dst[r0:r0 + nr, c0:c0 + nc] = stage_ref[k % PREP_BUFS, :nr, :nc].astype(jnp.bfloat16)
        if k + PREP_BUFS < len(chunks):
            chunk_copy(k + PREP_BUFS).start()


def _layer_stages(x, p, vectors, w_vmem, act_ref, pool_conv):
    g_mix_ref, pool_scale_ref, _, g_ffn_ref, g_ple_ref, g_final_ref = vectors
    (w_in_ref, w_group_ref, w_pool_up_ref, w_conv_out_ref, w_o_ref, w_ffn_in_ref,
     w_ffn_out_ref, w_ple_ref, w_ple_gate_ref) = w_vmem
    bf16 = jnp.bfloat16

    hn = _rmsnorm(x, g_mix_ref[...]).astype(bf16)

    def proj(off, width):
        return _dot(hn, w_in_ref[:, off:off + width])

    u = proj(OFF_U, POOL_WIDTH)
    yield
    v = proj(OFF_C, CONV_WIDTH) * proj(OFF_H, CONV_WIDTH)
    yield
    pooled, yconv = pool_conv(u, v)
    conv_out = (proj(OFF_B, CONV_WIDTH) * yconv).astype(bf16)
    yield
    gate_pool = jax.nn.sigmoid(proj(OFF_GP, D_MODEL))
    yield
    gate_conv = jax.nn.sigmoid(proj(OFF_GC, D_MODEL))
    yield
    mixed = [_dot(d, w_group_ref[gi * POOL_GROUP:(gi + 1) * POOL_GROUP, :])
             for gi, d in enumerate(pooled)]
    pool_out = (jnp.concatenate(mixed, axis=-1) * pool_scale_ref[...]).astype(bf16)
    yield
    gated_pool = gate_pool * _dot(pool_out, w_pool_up_ref[...])
    yield
    merged = (gated_pool + gate_conv * _dot(conv_out, w_conv_out_ref[...])).astype(bf16)
    yield
    x = x + _dot(merged, w_o_ref[...])
    yield

    hn = _rmsnorm(x, g_ffn_ref[...]).astype(bf16)
    for c0 in range(0, D_FF, FF_CHUNK):
        gate = _dot(hn, w_ffn_in_ref[:, c0:c0 + FF_CHUNK])
        up = _dot(hn, w_ffn_in_ref[:, D_FF + c0:D_FF + c0 + FF_CHUNK])
        act_ref[:, c0:c0 + FF_CHUNK] = (jax.nn.silu(gate) * up).astype(bf16)
        yield
    x = x + _dot(act_ref[...], w_ffn_out_ref[...])
    yield

    hn = _rmsnorm(x, g_ple_ref[...]).astype(bf16)
    ple_gate = jax.nn.sigmoid(_dot(hn, w_ple_gate_ref[...]))
    yield
    x = x + ple_gate * _dot(p.astype(bf16), w_ple_ref[...])
    return _rmsnorm(x, g_final_ref[...])


def _run_interleaved(chains, lag):
    results = [None] * len(chains)
    live = list(range(len(chains)))
    rounds = 0
    while live:
        for i in list(live):
            if rounds < i * lag:
                continue
            try:
                next(chains[i])
            except StopIteration as done:
                results[i] = done.value
                live.remove(i)
        rounds += 1
    return results


def _prompt_tile(t, x_ref, p_ref, vectors, w_vmem, y_ref, pool_state_ref, conv_state_ref,
                 u_carry_ref, v_carry_ref, ext_ref, vext_ref, act_ref, *, tile, tiles_per_seq):
    wc = functools.partial(_conv_tap, vectors[2])
    rows = tile // PROMPT_CHAINS

    @pl.when(t % tiles_per_seq == 0)
    def _():
        u_carry_ref[...] = jnp.zeros((POOL_PAD, POOL_WIDTH), jnp.float32)
        v_carry_ref[...] = jnp.zeros((CONV_PAD, CONV_WIDTH), jnp.float32)

    ext_ref[:POOL_PAD, :] = u_carry_ref[...]
    vext_ref[:CONV_PAD, :] = v_carry_ref[...]

    def pool_conv_at(r0):
        def pool_conv(u, v):
            ext_ref[POOL_PAD + r0:POOL_PAD + r0 + rows, :] = u
            vext_ref[CONV_PAD + r0:CONV_PAD + r0 + rows, :] = v

            def ext_rows(back, c0, c1):
                return ext_ref[POOL_PAD + r0 - back:POOL_PAD + r0 - back + rows, c0:c1]

            def vext_rows(back):
                return vext_ref[CONV_PAD + r0 - back:CONV_PAD + r0 - back + rows, :]

            pos = (t % tiles_per_seq) * tile + r0 + jax.lax.broadcasted_iota(
                jnp.int32, (rows, POOL_GROUP), 0)
            valid = (pos + 1).astype(jnp.float32)
            pooled = []
            for gi, w in enumerate(POOL_WINDOWS):
                c0, c1 = gi * POOL_GROUP, (gi + 1) * POOL_GROUP
                win = ext_rows(0, c0, c1)
                for back in range(1, w):
                    win = win + ext_rows(back, c0, c1)
                mean = win / jnp.minimum(valid, float(w))
                pooled.append((mean - ext_rows(0, c0, c1)).astype(jnp.bfloat16))

            yconv = wc(0) * vext_rows(2) + wc(1) * vext_rows(1) + wc(2) * vext_rows(0)
            return pooled, yconv
        return pool_conv

    starts = range(0, tile, rows)
    ys = _run_interleaved([
        _layer_stages(x_ref[r0:r0 + rows, :], p_ref[r0:r0 + rows, :], vectors, w_vmem,
                      act_ref.at[r0:r0 + rows, :], pool_conv_at(r0)) for r0 in starts],
        PROMPT_LAG)
    for r0, y in zip(starts, ys):
        y_ref[r0:r0 + rows, :] = y

    u_tail = ext_ref[tile:tile + POOL_PAD, :]
    v_tail = vext_ref[tile:tile + CONV_PAD, :]
    u_carry_ref[...] = u_tail
    v_carry_ref[...] = v_tail
    seq = t // tiles_per_seq
    pool_state_ref[:, pl.ds(seq, 1), :] = u_tail[POOL_PAD - POOL_HIST:][:, None, :]
    conv_state_ref[0] = v_tail[CONV_PAD - CONV_HIST:]


def _sample_tile(x_ref, p_ref, pool_hist_ref, conv_hist_ref, vectors, w_vmem,
                 y_ref, pool_new_ref, conv_new_ref, act_ref, *, seqs, steps):
    wc = functools.partial(_conv_tap, vectors[2])
    n = seqs // SAMPLE_CHAINS

    def chain(q0):
        qs = slice(q0, q0 + n)
        by_step = lambda ref: jnp.concatenate([ref[qs, s, :] for s in range(steps)], axis=0)
        slab = lambda a, s: a[s * n:(s + 1) * n]

        def pool_conv(u, v):
            line = ([pool_hist_ref[s, qs, :] for s in range(POOL_HIST)]
                    + [slab(u, s) for s in range(steps)])
            pooled = []
            for gi, w in enumerate(POOL_WINDOWS):
                c0, c1 = gi * POOL_GROUP, (gi + 1) * POOL_GROUP
                per_step = []
                for s in range(steps):
                    now = POOL_HIST + s
                    win = line[now][:, c0:c1]
                    for back in range(1, w):
                        win = win + line[now - back][:, c0:c1]
                    per_step.append(win * (1.0 / w) - line[now][:, c0:c1])
                pooled.append(jnp.concatenate(per_step, axis=0).astype(jnp.bfloat16))
            for s in range(POOL_HIST):
                pool_new_ref[s, qs, :] = line[steps + s]

            vline = ([conv_hist_ref[qs, s, :] for s in range(CONV_HIST)]
                     + [slab(v, s) for s in range(steps)])
            yconv = jnp.concatenate(
                [wc(0) * vline[s] + wc(1) * vline[s + 1] + wc(2) * vline[s + 2]
                 for s in range(steps)], axis=0)
            for s in range(CONV_HIST):
                conv_new_ref[qs, s, :] = vline[steps + s]
            return pooled, yconv

        return _layer_stages(by_step(x_ref), by_step(p_ref), vectors, w_vmem,
                             act_ref.at[q0 * steps:(q0 + n) * steps, :], pool_conv)

    starts = range(0, seqs, n)
    for q0, y in zip(starts, _run_interleaved([chain(q0) for q0 in starts], SAMPLE_LAG)):
        for s in range(steps):
            y_ref[q0:q0 + n, s, :] = y[s * n:(s + 1) * n]


def _layer_kernel(*refs, n_prompt_tiles, tile_prompt, tiles_per_seq, seqs, steps):
    xp_ref, pp_ref, xs_ref, ps_ref, pool_hist_ref, conv_hist_ref = refs[:N_INPUTS]
    refs = refs[N_INPUTS:]
    vectors, refs = refs[:N_VECTORS], refs[N_VECTORS:]
    w_hbm, refs = refs[:N_MATRICES], refs[N_MATRICES:]
    (yp_ref, pool_state_ref, conv_state_ref, ys_ref, pool_new_ref,
     conv_new_ref) = refs[:N_OUTPUTS]
    refs = refs[N_OUTPUTS:]
    u_carry_ref, v_carry_ref = refs[:2]
    w_vmem = refs[2:]
    f32, bf16 = jnp.float32, jnp.bfloat16

    step = pl.program_id(0)

    @pl.when(step == 0)
    def _():
        pl.run_scoped(
            functools.partial(_prep_weights, w_hbm, w_vmem),
            pltpu.VMEM((PREP_BUFS, PREP_ROWS, PREP_COLS), f32),
            pltpu.SemaphoreType.DMA((PREP_BUFS,)))

    @pl.when(jnp.logical_and(step >= 1, step <= n_prompt_tiles))
    def _():
        pl.run_scoped(
            functools.partial(_prompt_tile, step - 1, xp_ref, pp_ref, vectors, w_vmem, yp_ref,
                              pool_state_ref, conv_state_ref, u_carry_ref, v_carry_ref,
                              tile=tile_prompt, tiles_per_seq=tiles_per_seq),
            pltpu.VMEM((POOL_PAD + tile_prompt, POOL_WIDTH), f32),
            pltpu.VMEM((CONV_PAD + tile_prompt, CONV_WIDTH), f32),
            pltpu.VMEM((tile_prompt, D_FF), bf16))

    @pl.when(step > n_prompt_tiles)
    def _():
        pl.run_scoped(
            functools.partial(_sample_tile, xs_ref, ps_ref, pool_hist_ref, conv_hist_ref, vectors,
                              w_vmem, ys_ref, pool_new_ref, conv_new_ref, seqs=seqs, steps=steps),
            pltpu.VMEM((seqs * steps, D_FF), bf16))


def _resident(shape):
    zeros = (0,) * len(shape)
    return pl.BlockSpec(shape, lambda i: zeros, pipeline_mode=pl.Buffered(1))


def _run_layer(x_p, p_p, x_s, p_s, pool_hist, conv_hist, vectors, matrices, *, seq_len, steps):
    tokens_p, tokens_s = x_p.shape[0], x_s.shape[0] * steps
    assert x_s.shape[1] == steps
    tile_p, tile_s = TILE_PROMPT, TILE_SAMPLE
    assert tokens_p % tile_p == 0 and seq_len % tile_p == 0
    assert tile_p % (PROMPT_CHAINS * SUBLANES) == 0 and tile_p // PROMPT_CHAINS >= POOL_PAD
    assert tokens_s % tile_s == 0 and tile_s % (SAMPLE_CHAINS * steps * SUBLANES) == 0
    assert len(vectors) == N_VECTORS and len(matrices) == N_MATRICES
    n_p, n_s = tokens_p // tile_p, tokens_s // tile_s
    tiles_per_seq = seq_len // tile_p
    n_seq = tokens_p // seq_len
    seqs = tile_s // steps

    tile_p_of = lambda i: jnp.clip(i - 1, 0, n_p - 1)
    tile_s_of = lambda i: jnp.clip(i - 1 - n_p, 0, n_s - 1)
    row_p = lambda width: pl.BlockSpec((tile_p, width), lambda i: (tile_p_of(i), 0))
    pool_state_p = pl.BlockSpec((POOL_HIST, n_seq, POOL_WIDTH), lambda i: (0, 0, 0))
    conv_state_p = pl.BlockSpec((1, CONV_HIST, CONV_WIDTH),
                                lambda i: (tile_p_of(i) // tiles_per_seq, 0, 0))
    row_s_in = lambda width: pl.BlockSpec((seqs, steps, width), lambda i: (tile_s_of(i), 0, 0),
                                          pipeline_mode=pl.Buffered(1))
    pool_s = pl.BlockSpec((POOL_HIST, seqs, POOL_WIDTH), lambda i: (0, tile_s_of(i), 0),
                          pipeline_mode=pl.Buffered(1))
    conv_s = pl.BlockSpec((seqs, CONV_HIST, CONV_WIDTH), lambda i: (tile_s_of(i), 0, 0),
                          pipeline_mode=pl.Buffered(1))

    f32 = jnp.float32
    out_shape = [
        jax.ShapeDtypeStruct((tokens_p, D_MODEL), f32),
        jax.ShapeDtypeStruct((POOL_HIST, n_seq, POOL_WIDTH), f32),
        jax.ShapeDtypeStruct((n_seq, CONV_HIST, CONV_WIDTH), f32),
        jax.ShapeDtypeStruct((tokens_s // steps, steps, D_MODEL), f32),
        jax.ShapeDtypeStruct((POOL_HIST, tokens_s // steps, POOL_WIDTH), f32),
        jax.ShapeDtypeStruct((tokens_s // steps, CONV_HIST, CONV_WIDTH), f32),
    ]
    out_specs = [row_p(D_MODEL), pool_state_p, conv_state_p,
                 pl.BlockSpec((seqs, steps, D_MODEL), lambda i: (tile_s_of(i), 0, 0),
                              pipeline_mode=pl.Buffered(1)),
                 pool_s, conv_s]
    scratch = [pltpu.VMEM((POOL_PAD, POOL_WIDTH), f32), pltpu.VMEM((CONV_PAD, CONV_WIDTH), f32)]
    scratch += [pltpu.VMEM(w.shape, jnp.bfloat16) for w in matrices]

    kernel = functools.partial(_layer_kernel, n_prompt_tiles=n_p, tile_prompt=tile_p,
                               tiles_per_seq=tiles_per_seq, seqs=seqs, steps=steps)
    return pl.pallas_call(
        kernel,
        grid=(1 + n_p + n_s,),
        in_specs=([row_p(D_MODEL), row_p(D_PLE), row_s_in(D_MODEL), row_s_in(D_PLE),
                   pool_s, conv_s]
                  + [_resident(v.shape) for v in vectors]
                  + [pl.BlockSpec(memory_space=pl.ANY)] * N_MATRICES),
        out_specs=out_specs,
        out_shape=out_shape,
        scratch_shapes=scratch,
        compiler_params=pltpu.CompilerParams(
            dimension_semantics=("arbitrary",),
            vmem_limit_bytes=VMEM_LIMIT_BYTES),
        name="hybrid_layer",
    )(x_p, p_p, x_s, p_s, pool_hist, conv_hist, *vectors, *matrices)


def kernel(x_prompt, x_sample, state_pool, state_conv, p_prompt, p_sample, g_mix, w_in, w_pool_group, pool_scale, w_pool_up, w_conv, w_conv_out, w_o, g_ffn, w_ffn_in, w_ffn_out, g_ple, w_ple, w_ple_gate, g_final):
    depth = g_mix.shape[0]
    assert depth == 1, "single-layer step"
    vec = lambda a: a.reshape(1, -1)
    vectors = (vec(g_mix[0]), vec(pool_scale[0]), vec(w_conv[0]), vec(g_ffn[0]), vec(g_ple[0]),
               vec(g_final))
    matrices = (w_in[0], w_pool_group[0].reshape(POOL_WIDTH, POOL_GROUP), w_pool_up[0],
                w_conv_out[0], w_o[0], w_ffn_in[0], w_ffn_out[0], w_ple[0], w_ple_gate[0])

    batch, seq, _ = x_prompt.shape
    dec_seq = x_sample.shape[1]

    y_p, pool_state_p, conv_state_p, y_s, pool_state_s, conv_state_s = _run_layer(
        x_prompt.reshape(batch * seq, D_MODEL), p_prompt[0].reshape(batch * seq, D_PLE),
        x_sample, p_sample[0],
        jnp.swapaxes(state_pool[0], 0, 1), state_conv[0],
        vectors, matrices, seq_len=seq, steps=dec_seq)

    return (y_p.reshape(batch, seq, D_MODEL), y_s,
            jnp.swapaxes(pool_state_p, 0, 1)[None], conv_state_p[None],
            jnp.swapaxes(pool_state_s, 0, 1)[None], conv_state_s[None])
```

```python
import functools

import jax
import jax.numpy as jnp
from jax.experimental import pallas as pl
from jax.experimental.pallas import tpu as pltpu

D_MODEL = 1024
POOL_WINDOWS = (2, 4, 8, 16)
POOL_GROUP = 128
POOL_WIDTH = 512
POOL_HIST = 15
CONV_WIDTH = 512
CONV_K = 3
CONV_HIST = 2
D_FF = 2816
D_PLE = 256
EPS = 1e-6

OFF_U = 0
OFF_B = OFF_U + POOL_WIDTH
OFF_C = OFF_B + CONV_WIDTH
OFF_H = OFF_C + CONV_WIDTH
OFF_GP = OFF_H + CONV_WIDTH
OFF_GC = OFF_GP + D_MODEL

V7X_VMEM_BYTES = 64 * 1024 * 1024
V7X_MXU_WIDTH = 256
SUBLANES = 8
DMA_PRIORITIES = 2

POOL_PAD = 16
CONV_PAD = 8
FF_CHUNK = V7X_MXU_WIDTH
TILE_PROMPT = 512
TILE_SAMPLE = 256
PROMPT_CHAINS = 2
SAMPLE_CHAINS = 2
PROMPT_LAG = 1
SAMPLE_LAG = 0
PREP_ROWS = 256
PREP_COLS = 1024
PREP_BUFS = 8
VMEM_RESERVE_BYTES = 4 * 1024 * 1024
VMEM_LIMIT_BYTES = V7X_VMEM_BYTES - VMEM_RESERVE_BYTES

N_VECTORS = 6
N_MATRICES = 9
N_INPUTS = 6
N_OUTPUTS = 6


def _rmsnorm(x, g):
    ms = jnp.mean(x * x, axis=-1, keepdims=True)
    return x * jax.lax.rsqrt(ms + EPS) * g


def _dot(a, b):
    return jnp.dot(a, b, preferred_element_type=jnp.float32)


def _conv_tap(w_conv_ref, k):
    return w_conv_ref[:, k * CONV_WIDTH:(k + 1) * CONV_WIDTH]


def _prep_weights(w_hbm, w_vmem, stage_ref, sem):
    chunks = []
    for src, dst in zip(w_hbm, w_vmem):
        n_rows, n_cols = dst.shape
        for r0 in range(0, n_rows, PREP_ROWS):
            for c0 in range(0, n_cols, PREP_COLS):
                chunks.append((src, dst, r0, min(PREP_ROWS, n_rows - r0),
                               c0, min(PREP_COLS, n_cols - c0)))

    def chunk_copy(k):
        src, _, r0, nr, c0, nc = chunks[k]
        slot = k % PREP_BUFS
        return pltpu.make_async_copy(src.at[r0:r0 + nr, c0:c0 + nc],
                                     stage_ref.at[slot, :nr, :nc], sem.at[slot])

    for k in range(min(PREP_BUFS, len(chunks))):
        chunk_copy(k).start(priority=k % DMA_PRIORITIES)
    for k, (_, dst, r0, nr, c0, nc) in enumerate(chunks):
        chunk_copy(k).wait()
        dst[r0:r0 + nr, c0:c0 + nc] = stage_ref[k % PREP_BUFS, :nr, :nc].astype(jnp.bfloat16)
        if k + PREP_BUFS < len(chunks):
            chunk_copy(k + PREP_BUFS).start(priority=(k + PREP_BUFS) % DMA_PRIORITIES)


def _layer_stages(x, p, vectors, w_vmem, act_ref, pool_conv):
    g_mix_ref, pool_scale_ref, _, g_ffn_ref, g_ple_ref, g_final_ref = vectors
    (w_in_ref, w_group_ref, w_pool_up_ref, w_conv_out_ref, w_o_ref, w_ffn_in_ref,
     w_ffn_out_ref, w_ple_ref, w_ple_gate_ref) = w_vmem
    bf16 = jnp.bfloat16

    hn = _rmsnorm(x, g_mix_ref[...]).astype(bf16)

    def proj(off, width):
        return _dot(hn, w_in_ref[:, off:off + width])

    u = proj(OFF_U, POOL_WIDTH)
    yield
    v = proj(OFF_C, CONV_WIDTH) * proj(OFF_H, CONV_WIDTH)
    yield
    pooled, yconv = pool_conv(u, v)
    conv_out = (proj(OFF_B, CONV_WIDTH) * yconv).astype(bf16)
    yield
    gate_pool = jax.nn.sigmoid(proj(OFF_GP, D_MODEL))
    yield
    gate_conv = jax.nn.sigmoid(proj(OFF_GC, D_MODEL))
    yield
    mixed = [_dot(d, w_group_ref[gi * POOL_GROUP:(gi + 1) * POOL_GROUP, :])
             for gi, d in enumerate(pooled)]
    pool_out = (jnp.concatenate(mixed, axis=-1) * pool_scale_ref[...]).astype(bf16)
    yield
    gated_pool = gate_pool * _dot(pool_out, w_pool_up_ref[...])
    yield
    merged = (gated_pool + gate_conv * _dot(conv_out, w_conv_out_ref[...])).astype(bf16)
    yield
    x = x + _dot(merged, w_o_ref[...])
    yield

    hn = _rmsnorm(x, g_ffn_ref[...]).astype(bf16)
    for c0 in range(0, D_FF, FF_CHUNK):
        gate = _dot(hn, w_ffn_in_ref[:, c0:c0 + FF_CHUNK])
        up = _dot(hn, w_ffn_in_ref[:, D_FF + c0:D_FF + c0 + FF_CHUNK])
        act_ref[:, c0:c0 + FF_CHUNK] = (jax.nn.silu(gate) * up).astype(bf16)
        yield
    x = x + _dot(act_ref[...], w_ffn_out_ref[...])
    yield

    hn = _rmsnorm(x, g_ple_ref[...]).astype(bf16)
    ple_gate = jax.nn.sigmoid(_dot(hn, w_ple_gate_ref[...]))
    yield
    x = x + ple_gate * _dot(p.astype(bf16), w_ple_ref[...])
    return _rmsnorm(x, g_final_ref[...])


def _run_interleaved(chains, lag):
    results = [None] * len(chains)
    live = list(range(len(chains)))
    rounds = 0
    while live:
        for i in list(live):
            if rounds < i * lag:
                continue
            try:
                next(chains[i])
            except StopIteration as done:
                results[i] = done.value
                live.remove(i)
        rounds += 1
    return results


def _prompt_tile(t, x_ref, p_ref, vectors, w_vmem, y_ref, pool_state_ref, conv_state_ref,
                 u_carry_ref, v_carry_ref, ext_ref, vext_ref, act_ref, *, tile, tiles_per_seq):
    wc = functools.partial(_conv_tap, vectors[2])
    rows = tile // PROMPT_CHAINS

    @pl.when(t % tiles_per_seq == 0)
    def _():
        u_carry_ref[...] = jnp.zeros((POOL_PAD, POOL_WIDTH), jnp.float32)
        v_carry_ref[...] = jnp.zeros((CONV_PAD, CONV_WIDTH), jnp.float32)

    ext_ref[:POOL_PAD, :] = u_carry_ref[...]
    vext_ref[:CONV_PAD, :] = v_carry_ref[...]

    def pool_conv_at(r0):
        def pool_conv(u, v):
            ext_ref[POOL_PAD + r0:POOL_PAD + r0 + rows, :] = u
            vext_ref[CONV_PAD + r0:CONV_PAD + r0 + rows, :] = v

            def ext_rows(back, c0, c1):
                return ext_ref[POOL_PAD + r0 - back:POOL_PAD + r0 - back + rows, c0:c1]

            def vext_rows(back):
                return vext_ref[CONV_PAD + r0 - back:CONV_PAD + r0 - back + rows, :]

            pos = (t % tiles_per_seq) * tile + r0 + jax.lax.broadcasted_iota(
                jnp.int32, (rows, POOL_GROUP), 0)
            valid = (pos + 1).astype(jnp.float32)
            pooled = []
            for gi, w in enumerate(POOL_WINDOWS):
                c0, c1 = gi * POOL_GROUP, (gi + 1) * POOL_GROUP
                win = ext_rows(0, c0, c1)
                for back in range(1, w):
                    win = win + ext_rows(back, c0, c1)
                mean = win / jnp.minimum(valid, float(w))
                pooled.append((mean - ext_rows(0, c0, c1)).astype(jnp.bfloat16))

            yconv = wc(0) * vext_rows(2) + wc(1) * vext_rows(1) + wc(2) * vext_rows(0)
            return pooled, yconv
        return pool_conv

    starts = range(0, tile, rows)
    ys = _run_interleaved([
        _layer_stages(x_ref[r0:r0 + rows, :], p_ref[r0:r0 + rows, :], vectors, w_vmem,
                      act_ref.at[r0:r0 + rows, :], pool_conv_at(r0)) for r0 in starts],
        PROMPT_LAG)
    for r0, y in zip(starts, ys):
        y_ref[r0:r0 + rows, :] = y

    u_tail = ext_ref[tile:tile + POOL_PAD, :]
    v_tail = vext_ref[tile:tile + CONV_PAD, :]
    u_carry_ref[...] = u_tail
    v_carry_ref[...] = v_tail
    seq = t // tiles_per_seq
    pool_state_ref[:, pl.ds(seq, 1), :] = u_tail[POOL_PAD - POOL_HIST:][:, None, :]
    conv_state_ref[0] = v_tail[CONV_PAD - CONV_HIST:]


def _sample_tile(x_ref, p_ref, pool_hist_ref, conv_hist_ref, vectors, w_vmem,
                 y_ref, pool_new_ref, conv_new_ref, act_ref, *, seqs, steps):
    wc = functools.partial(_conv_tap, vectors[2])
    n = seqs // SAMPLE_CHAINS

    def chain(q0):
        qs = slice(q0, q0 + n)
        by_step = lambda ref: jnp.concatenate([ref[qs, s, :] for s in range(steps)], axis=0)
        slab = lambda a, s: a[s * n:(s + 1) * n]

        def pool_conv(u, v):
            line = ([pool_hist_ref[s, qs, :] for s in range(POOL_HIST)]
                    + [slab(u, s) for s in range(steps)])
            pooled = []
            for gi, w in enumerate(POOL_WINDOWS):
                c0, c1 = gi * POOL_GROUP, (gi + 1) * POOL_GROUP
                per_step = []
                for s in range(steps):
                    now = POOL_HIST + s
                    win = line[now][:, c0:c1]
                    for back in range(1, w):
                        win = win + line[now - back][:, c0:c1]
                    per_step.append(win * (1.0 / w) - line[now][:, c0:c1])
                pooled.append(jnp.concatenate(per_step, axis=0).astype(jnp.bfloat16))
            for s in range(POOL_HIST):
                pool_new_ref[s, qs, :] = line[steps + s]

            vline = ([conv_hist_ref[qs, s, :] for s in range(CONV_HIST)]
                     + [slab(v, s) for s in range(steps)])
            yconv = jnp.concatenate(
                [wc(0) * vline[s] + wc(1) * vline[s + 1] + wc(2) * vline[s + 2]
                 for s in range(steps)], axis=0)
            for s in range(CONV_HIST):
                conv_new_ref[qs, s, :] = vline[steps + s]
            return pooled, yconv

        return _layer_stages(by_step(x_ref), by_step(p_ref), vectors, w_vmem,
                             act_ref.at[q0 * steps:(q0 + n) * steps, :], pool_conv)

    starts = range(0, seqs, n)
    for q0, y in zip(starts, _run_interleaved([chain(q0) for q0 in starts], SAMPLE_LAG)):
        for s in range(steps):
            y_ref[q0:q0 + n, s, :] = y[s * n:(s + 1) * n]


def _layer_kernel(*refs, n_prompt_tiles, tile_prompt, tiles_per_seq, seqs, steps):
    xp_ref, pp_ref, xs_ref, ps_ref, pool_hist_ref, conv_hist_ref = refs[:N_INPUTS]
    refs = refs[N_INPUTS:]
    vectors, refs = refs[:N_VECTORS], refs[N_VECTORS:]
    w_hbm, refs = refs[:N_MATRICES], refs[N_MATRICES:]
    (yp_ref, pool_state_ref, conv_state_ref, ys_ref, pool_new_ref,
     conv_new_ref) = refs[:N_OUTPUTS]
    refs = refs[N_OUTPUTS:]
    u_carry_ref, v_carry_ref = refs[:2]
    w_vmem = refs[2:]
    f32, bf16 = jnp.float32, jnp.bfloat16

    step = pl.program_id(0)

    @pl.when(step == 0)
    def _():
        pl.run_scoped(
            functools.partial(_prep_weights, w_hbm, w_vmem),
            pltpu.VMEM((PREP_BUFS, PREP_ROWS, PREP_COLS), f32),
            pltpu.SemaphoreType.DMA((PREP_BUFS,)))

    @pl.when(jnp.logical_and(step >= 1, step <= n_prompt_tiles))
    def _():
        pl.run_scoped(
            functools.partial(_prompt_tile, step - 1, xp_ref, pp_ref, vectors, w_vmem, yp_ref,
                              pool_state_ref, conv_state_ref, u_carry_ref, v_carry_ref,
                              tile=tile_prompt, tiles_per_seq=tiles_per_seq),
            pltpu.VMEM((POOL_PAD + tile_prompt, POOL_WIDTH), f32),
            pltpu.VMEM((CONV_PAD + tile_prompt, CONV_WIDTH), f32),
            pltpu.VMEM((tile_prompt, D_FF), bf16))

    @pl.when(step > n_prompt_tiles)
    def _():
        pl.run_scoped(
            functools.partial(_sample_tile, xs_ref, ps_ref, pool_hist_ref, conv_hist_ref, vectors,
                              w_vmem, ys_ref, pool_new_ref, conv_new_ref, seqs=seqs, steps=steps),
            pltpu.VMEM((seqs * steps, D_FF), bf16))


def _resident(shape):
    zeros = (0,) * len(shape)
    return pl.BlockSpec(shape, lambda i: zeros, pipeline_mode=pl.Buffered(1))


def _run_layer(x_p, p_p, x_s, p_s, pool_hist, conv_hist, vectors, matrices, *, seq_len, steps):
    tokens_p, tokens_s = x_p.shape[0], x_s.shape[0] * steps
    assert x_s.shape[1] == steps
    tile_p, tile_s = TILE_PROMPT, TILE_SAMPLE
    assert tokens_p % tile_p == 0 and seq_len % tile_p == 0
    assert tile_p % (PROMPT_CHAINS * SUBLANES) == 0 and tile_p // PROMPT_CHAINS >= POOL_PAD
    assert tokens_s % tile_s == 0 and tile_s % (SAMPLE_CHAINS * steps * SUBLANES) == 0
    assert len(vectors) == N_VECTORS and len(matrices) == N_MATRICES
    n_p, n_s = tokens_p // tile_p, tokens_s // tile_s
    tiles_per_seq = seq_len // tile_p
    n_seq = tokens_p // seq_len
    seqs = tile_s // steps

    tile_p_of = lambda i: jnp.clip(i - 1, 0, n_p - 1)
    tile_s_of = lambda i: jnp.clip(i - 1 - n_p, 0, n_s - 1)
    row_p = lambda width: pl.BlockSpec((tile_p, width), lambda i: (tile_p_of(i), 0))
    pool_state_p = pl.BlockSpec((POOL_HIST, n_seq, POOL_WIDTH), lambda i: (0, 0, 0))
    conv_state_p = pl.BlockSpec((1, CONV_HIST, CONV_WIDTH),
                                lambda i: (tile_p_of(i) // tiles_per_seq, 0, 0))
    row_s_in = lambda width: pl.BlockSpec((seqs, steps, width), lambda i: (tile_s_of(i), 0, 0),
                                          pipeline_mode=pl.Buffered(1))
    pool_s = pl.BlockSpec((POOL_HIST, seqs, POOL_WIDTH), lambda i: (0, tile_s_of(i), 0),
                          pipeline_mode=pl.Buffered(1))
    conv_s = pl.BlockSpec((seqs, CONV_HIST, CONV_WIDTH), lambda i: (tile_s_of(i), 0, 0),
                          pipeline_mode=pl.Buffered(1))

    f32 = jnp.float32
    out_shape = [
        jax.ShapeDtypeStruct((tokens_p, D_MODEL), f32),
        jax.ShapeDtypeStruct((POOL_HIST, n_seq, POOL_WIDTH), f32),
        jax.ShapeDtypeStruct((n_seq, CONV_HIST, CONV_WIDTH), f32),
        jax.ShapeDtypeStruct((tokens_s // steps, steps, D_MODEL), f32),
        jax.ShapeDtypeStruct((POOL_HIST, tokens_s // steps, POOL_WIDTH), f32),
        jax.ShapeDtypeStruct((tokens_s // steps, CONV_HIST, CONV_WIDTH), f32),
    ]
    out_specs = [row_p(D_MODEL), pool_state_p, conv_state_p,
                 pl.BlockSpec((seqs, steps, D_MODEL), lambda i: (tile_s_of(i), 0, 0),
                              pipeline_mode=pl.Buffered(1)),
                 pool_s, conv_s]
    scratch = [pltpu.VMEM((POOL_PAD, POOL_WIDTH), f32), pltpu.VMEM((CONV_PAD, CONV_WIDTH), f32)]
    scratch += [pltpu.VMEM(w.shape, jnp.bfloat16) for w in matrices]

    kernel = functools.partial(_layer_kernel, n_prompt_tiles=n_p, tile_prompt=tile_p,
                               tiles_per_seq=tiles_per_seq, seqs=seqs, steps=steps)
    return pl.pallas_call(
        kernel,
        grid=(1 + n_p + n_s,),
        in_specs=([row_p(D_MODEL), row_p(D_PLE), row_s_in(D_MODEL), row_s_in(D_PLE),
                   pool_s, conv_s]
                  + [_resident(v.shape) for v in vectors]
                  + [pl.BlockSpec(memory_space=pl.ANY)] * N_MATRICES),
        out_specs=out_specs,
        out_shape=out_shape,
        scratch_shapes=scratch,
        compiler_params=pltpu.CompilerParams(
            dimension_semantics=("arbitrary",),
            vmem_limit_bytes=VMEM_LIMIT_BYTES),
        name="hybrid_layer",
    )(x_p, p_p, x_s, p_s, pool_hist, conv_hist, *vectors, *matrices)


def kernel(x_prompt, x_sample, state_pool, state_conv, p_prompt, p_sample, g_mix, w_in, w_pool_group, pool_scale, w_pool_up, w_conv, w_conv_out, w_o, g_ffn, w_ffn_in, w_ffn_out, g_ple, w_ple, w_ple_gate, g_final):
    depth = g_mix.shape[0]
    assert depth == 1, "single-layer step"
    vec = lambda a: a.reshape(1, -1)
    vectors = (vec(g_mix[0]), vec(pool_scale[0]), vec(w_conv[0]), vec(g_ffn[0]), vec(g_ple[0]),
               vec(g_final))
    matrices = (w_in[0], w_pool_group[0].reshape(POOL_WIDTH, POOL_GROUP), w_pool_up[0],
                w_conv_out[0], w_o[0], w_ffn_in[0], w_ffn_out[0], w_ple[0], w_ple_gate[0])

    batch, seq, _ = x_prompt.shape
    dec_seq = x_sample.shape[1]

    y_p, pool_state_p, conv_state_p, y_s, pool_state_s, conv_state_s = _run_layer(
        x_prompt.reshape(batch * seq, D_MODEL), p_prompt[0].reshape(batch * seq, D_PLE),
        x_sample, p_sample[0],
        jnp.swapaxes(state_pool[0], 0, 1), state_conv[0],
        vectors, matrices, seq_len=seq, steps=dec_seq)

    return (y_p.reshape(batch, seq, D_MODEL), y_s,
            jnp.swapaxes(pool_state_p, 0, 1)[None], conv_state_p[None],
            jnp.swapaxes(pool_state_s, 0, 1)[None], conv_state_s[None])
```

```python
import functools

import jax
import jax.numpy as jnp
from jax.experimental import pallas as pl
from jax.experimental.pallas import tpu as pltpu

D_MODEL = 1024
POOL_WINDOWS = (2, 4, 8, 16)
POOL_GROUP = 128
POOL_WIDTH = 512
POOL_HIST = 15
CONV_WIDTH = 512
CONV_K = 3
CONV_HIST = 2
D_FF = 2816
D_PLE = 256
EPS = 1e-6

OFF_U = 0
OFF_B = OFF_U + POOL_WIDTH
OFF_C = OFF_B + CONV_WIDTH
OFF_H = OFF_C + CONV_WIDTH
OFF_GP = OFF_H + CONV_WIDTH
OFF_GC = OFF_GP + D_MODEL

V7X_VMEM_BYTES = 64 * 1024 * 1024
V7X_MXU_WIDTH = 256
SUBLANES = 8
DMA_PRIORITIES = 2

POOL_PAD = 16
CONV_PAD = 8
FF_CHUNK = V7X_MXU_WIDTH
TILE_PROMPT = 512
TILE_SAMPLE = 256
PROMPT_CHAINS = 2
SAMPLE_CHAINS = 2
PROMPT_LAG = 1
SAMPLE_LAG = 0
PREP_ROWS = 256
PREP_COLS = 1024
PREP_BUFS = 8
VMEM_RESERVE_BYTES = 4 * 1024 * 1024
VMEM_LIMIT_BYTES = V7X_VMEM_BYTES - VMEM_RESERVE_BYTES

N_VECTORS = 6
N_MATRICES = 9
N_INPUTS = 6
N_OUTPUTS = 6


def _rmsnorm(x, g):
    ms = jnp.mean(x * x, axis=-1, keepdims=True)
    return x * jax.lax.rsqrt(ms + EPS) * g


def _dot(a, b):
    return jnp.dot(a, b, preferred_element_type=jnp.float32)


def _conv_tap(w_conv_ref, k):
    return w_conv_ref[:, k * CONV_WIDTH:(k + 1) * CONV_WIDTH]


def _prep_weights(w_hbm, w_vmem, stage_ref, sem):
    chunks = []
    for src, dst in zip(w_hbm, w_vmem):
        n_rows, n_cols = dst.shape
        for r0 in range(0, n_rows, PREP_ROWS):
            for c0 in range(0, n_cols, PREP_COLS):
                chunks.append((src, dst, r0, min(PREP_ROWS, n_rows - r0),
                               c0, min(PREP_COLS, n_cols - c0)))

    def chunk_copy(k):
        src, _, r0, nr, c0, nc = chunks[k]
        slot = k % PREP_BUFS
        return pltpu.make_async_copy(src.at[r0:r0 + nr, c0:c0 + nc],
                                     stage_ref.at[slot, :nr, :nc], sem.at[slot])

    for k in range(min(PREP_BUFS, len(chunks))):
        chunk_copy(k).start(priority=k % DMA_PRIORITIES)
    for k, (_, dst, r0, nr, c0, nc) in enumerate(chunks):
        chunk_copy(k).wait()
        dst[r0:r0 + nr, c0:c0 + nc] = stage_ref[k % PREP_BUFS, :nr, :nc].astype(jnp.bfloat16)
        if k + PREP_BUFS < len(chunks):
            chunk_copy(k + PREP_BUFS).start(priority=(k + PREP_BUFS) % DMA_PRIORITIES)


def _fetch_vectors(v_hbm, v_vmem, sem):
    copies = [pltpu.make_async_copy(src, dst, sem.at[k])
              for k, (src, dst) in enumerate(zip(v_hbm, v_vmem))]
    for c in copies:
        c.start()
    for c in copies:
        c.wait()


def _layer_stages(x, p, vectors, w_vmem, act_ref, pool_conv):
    g_mix_ref, pool_scale_ref, _, g_ffn_ref, g_ple_ref, g_final_ref = vectors
    (w_in_ref, w_group_ref, w_pool_up_ref, w_conv_out_ref, w_o_ref, w_ffn_in_ref,
     w_ffn_out_ref, w_ple_ref, w_ple_gate_ref) = w_vmem
    bf16 = jnp.bfloat16

    hn = _rmsnorm(x, g_mix_ref[...]).astype(bf16)

    def proj(off, width):
        return _dot(hn, w_in_ref[:, off:off + width])

    u = proj(OFF_U, POOL_WIDTH)
    yield
    v = proj(OFF_C, CONV_WIDTH) * proj(OFF_H, CONV_WIDTH)
    yield
    pooled, yconv = pool_conv(u, v)
    conv_out = (proj(OFF_B, CONV_WIDTH) * yconv).astype(bf16)
    yield
    gate_pool = jax.nn.sigmoid(proj(OFF_GP, D_MODEL))
    yield
    gate_conv = jax.nn.sigmoid(proj(OFF_GC, D_MODEL))
    yield
    mixed = [_dot(d, w_group_ref[gi * POOL_GROUP:(gi + 1) * POOL_GROUP, :])
             for gi, d in enumerate(pooled)]
    pool_out = (jnp.concatenate(mixed, axis=-1) * pool_scale_ref[...]).astype(bf16)
    yield
    gated_pool = gate_pool * _dot(pool_out, w_pool_up_ref[...])
    yield
    merged = (gated_pool + gate_conv * _dot(conv_out, w_conv_out_ref[...])).astype(bf16)
    yield
    x = x + _dot(merged, w_o_ref[...])
    yield

    hn = _rmsnorm(x, g_ffn_ref[...]).astype(bf16)
    for c0 in range(0, D_FF, FF_CHUNK):
        gate = _dot(hn, w_ffn_in_ref[:, c0:c0 + FF_CHUNK])
        up = _dot(hn, w_ffn_in_ref[:, D_FF + c0:D_FF + c0 + FF_CHUNK])
        act_ref[:, c0:c0 + FF_CHUNK] = (jax.nn.silu(gate) * up).astype(bf16)
        yield
    x = x + _dot(act_ref[...], w_ffn_out_ref[...])
    yield

    hn = _rmsnorm(x, g_ple_ref[...]).astype(bf16)
    ple_gate = jax.nn.sigmoid(_dot(hn, w_ple_gate_ref[...]))
    yield
    x = x + ple_gate * _dot(p.astype(bf16), w_ple_ref[...])
    return _rmsnorm(x, g_final_ref[...])


def _run_interleaved(chains, lag):
    results = [None] * len(chains)
    live = list(range(len(chains)))
    rounds = 0
    while live:
        for i in list(live):
            if rounds < i * lag:
                continue
            try:
                next(chains[i])
            except StopIteration as done:
                results[i] = done.value
                live.remove(i)
        rounds += 1
    return results


def _prompt_tile(t, x_ref, p_ref, vectors, w_vmem, y_ref, pool_state_ref, conv_state_ref,
                 u_carry_ref, v_carry_ref, ext_ref, vext_ref, act_ref, *, tile, tiles_per_seq):
    wc = functools.partial(_conv_tap, vectors[2])
    rows = tile // PROMPT_CHAINS

    @pl.when(t % tiles_per_seq == 0)
    def _():
        u_carry_ref[...] = jnp.zeros((POOL_PAD, POOL_WIDTH), jnp.float32)
        v_carry_ref[...] = jnp.zeros((CONV_PAD, CONV_WIDTH), jnp.float32)

    ext_ref[:POOL_PAD, :] = u_carry_ref[...]
    vext_ref[:CONV_PAD, :] = v_carry_ref[...]

    def pool_conv_at(r0):
        def pool_conv(u, v):
            ext_ref[POOL_PAD + r0:POOL_PAD + r0 + rows, :] = u
            vext_ref[CONV_PAD + r0:CONV_PAD + r0 + rows, :] = v

            def ext_rows(back, c0, c1):
                return ext_ref[POOL_PAD + r0 - back:POOL_PAD + r0 - back + rows, c0:c1]

            def vext_rows(back):
                return vext_ref[CONV_PAD + r0 - back:CONV_PAD + r0 - back + rows, :]

            pos = (t % tiles_per_seq) * tile + r0 + jax.lax.broadcasted_iota(
                jnp.int32, (rows, POOL_GROUP), 0)
            valid = (pos + 1).astype(jnp.float32)
            pooled = []
            for gi, w in enumerate(POOL_WINDOWS):
                c0, c1 = gi * POOL_GROUP, (gi + 1) * POOL_GROUP
                win = ext_rows(0, c0, c1)
                for back in range(1, w):
                    win = win + ext_rows(back, c0, c1)
                mean = win / jnp.minimum(valid, float(w))
                pooled.append((mean - ext_rows(0, c0, c1)).astype(jnp.bfloat16))

            yconv = wc(0) * vext_rows(2) + wc(1) * vext_rows(1) + wc(2) * vext_rows(0)
            return pooled, yconv
        return pool_conv

    starts = range(0, tile, rows)
    ys = _run_interleaved([
        _layer_stages(x_ref[r0:r0 + rows, :], p_ref[r0:r0 + rows, :], vectors, w_vmem,
                      act_ref.at[r0:r0 + rows, :], pool_conv_at(r0)) for r0 in starts],
        PROMPT_LAG)
    for r0, y in zip(starts, ys):
        y_ref[r0:r0 + rows, :] = y

    u_tail = ext_ref[tile:tile + POOL_PAD, :]
    v_tail = vext_ref[tile:tile + CONV_PAD, :]
    u_carry_ref[...] = u_tail
    v_carry_ref[...] = v_tail
    seq = t // tiles_per_seq
    pool_state_ref[:, pl.ds(seq, 1), :] = u_tail[POOL_PAD - POOL_HIST:][:, None, :]
    conv_state_ref[0] = v_tail[CONV_PAD - CONV_HIST:]


def _sample_tile(x_ref, p_ref, pool_hist_ref, conv_hist_ref, vectors, w_vmem,
                 y_ref, pool_new_ref, conv_new_ref, act_ref, *, seqs, steps):
    wc = functools.partial(_conv_tap, vectors[2])
    n = seqs // SAMPLE_CHAINS

    def chain(q0):
        qs = slice(q0, q0 + n)
        by_step = lambda ref: jnp.concatenate([ref[qs, s, :] for s in range(steps)], axis=0)
        slab = lambda a, s: a[s * n:(s + 1) * n]

        def pool_conv(u, v):
            line = ([pool_hist_ref[s, qs, :] for s in range(POOL_HIST)]
                    + [slab(u, s) for s in range(steps)])
            pooled = []
            for gi, w in enumerate(POOL_WINDOWS):
                c0, c1 = gi * POOL_GROUP, (gi + 1) * POOL_GROUP
                per_step = []
                for s in range(steps):
                    now = POOL_HIST + s
                    win = line[now][:, c0:c1]
                    for back in range(1, w):
                        win = win + line[now - back][:, c0:c1]
                    per_step.append(win * (1.0 / w) - line[now][:, c0:c1])
                pooled.append(jnp.concatenate(per_step, axis=0).astype(jnp.bfloat16))
            for s in range(POOL_HIST):
                pool_new_ref[s, qs, :] = line[steps + s]

            vline = ([conv_hist_ref[qs, s, :] for s in range(CONV_HIST)]
                     + [slab(v, s) for s in range(steps)])
            yconv = jnp.concatenate(
                [wc(0) * vline[s] + wc(1) * vline[s + 1] + wc(2) * vline[s + 2]
                 for s in range(steps)], axis=0)
            for s in range(CONV_HIST):
                conv_new_ref[qs, s, :] = vline[steps + s]
            return pooled, yconv

        return _layer_stages(by_step(x_ref), by_step(p_ref), vectors, w_vmem,
                             act_ref.at[q0 * steps:(q0 + n) * steps, :], pool_conv)

    starts = range(0, seqs, n)
    for q0, y in zip(starts, _run_interleaved([chain(q0) for q0 in starts], SAMPLE_LAG)):
        for s in range(steps):
            y_ref[q0:q0 + n, s, :] = y[s * n:(s + 1) * n]


def _layer_kernel(*refs, n_prompt_tiles, tile_prompt, tiles_per_seq, seqs, steps):
    xp_ref, pp_ref, xs_ref, ps_ref, pool_hist_ref, conv_hist_ref = refs[:N_INPUTS]
    refs = refs[N_INPUTS:]
    v_hbm, refs = refs[:N_VECTORS], refs[N_VECTORS:]
    w_hbm, refs = refs[:N_MATRICES], refs[N_MATRICES:]
    (yp_ref, pool_state_ref, conv_state_ref, ys_ref, pool_new_ref,
     conv_new_ref) = refs[:N_OUTPUTS]
    refs = refs[N_OUTPUTS:]
    u_carry_ref, v_carry_ref = refs[:2]
    vectors, w_vmem = refs[2:2 + N_VECTORS], refs[2 + N_VECTORS:]
    f32, bf16 = jnp.float32, jnp.bfloat16

    step = pl.program_id(0)

    @pl.when(step == 0)
    def _():
        pl.run_scoped(functools.partial(_fetch_vectors, v_hbm, vectors),
                      pltpu.SemaphoreType.DMA((N_VECTORS,)))
        pl.run_scoped(
            functools.partial(_prep_weights, w_hbm, w_vmem),
            pltpu.VMEM((PREP_BUFS, PREP_ROWS, PREP_COLS), f32),
            pltpu.SemaphoreType.DMA((PREP_BUFS,)))

    @pl.when(jnp.logical_and(step >= 1, step <= n_prompt_tiles))
    def _():
        pl.run_scoped(
            functools.partial(_prompt_tile, step - 1, xp_ref, pp_ref, vectors, w_vmem, yp_ref,
                              pool_state_ref, conv_state_ref, u_carry_ref, v_carry_ref,
                              tile=tile_prompt, tiles_per_seq=tiles_per_seq),
            pltpu.VMEM((POOL_PAD + tile_prompt, POOL_WIDTH), f32),
            pltpu.VMEM((CONV_PAD + tile_prompt, CONV_WIDTH), f32),
            pltpu.VMEM((tile_prompt, D_FF), bf16))

    @pl.when(step > n_prompt_tiles)
    def _():
        pl.run_scoped(
            functools.partial(_sample_tile, xs_ref, ps_ref, pool_hist_ref, conv_hist_ref, vectors,
                              w_vmem, ys_ref, pool_new_ref, conv_new_ref, seqs=seqs, steps=steps),
            pltpu.VMEM((seqs * steps, D_FF), bf16))


def _run_layer(x_p, p_p, x_s, p_s, pool_hist, conv_hist, vectors, matrices, *, seq_len, steps):
    tokens_p, tokens_s = x_p.shape[0], x_s.shape[0] * steps
    assert x_s.shape[1] == steps
    tile_p, tile_s = TILE_PROMPT, TILE_SAMPLE
    assert tokens_p % tile_p == 0 and seq_len % tile_p == 0
    assert tile_p % (PROMPT_CHAINS * SUBLANES) == 0 and tile_p // PROMPT_CHAINS >= POOL_PAD
    assert tokens_s % tile_s == 0 and tile_s % (SAMPLE_CHAINS * steps * SUBLANES) == 0
    assert len(vectors) == N_VECTORS and len(matrices) == N_MATRICES
    n_p, n_s = tokens_p // tile_p, tokens_s // tile_s
    tiles_per_seq = seq_len // tile_p
    n_seq = tokens_p // seq_len
    seqs = tile_s // steps

    tile_p_of = lambda i: jnp.clip(i - 1, 0, n_p - 1)
    tile_s_of = lambda i: jnp.clip(i - 1 - n_p, 0, n_s - 1)
    row_p = lambda width: pl.BlockSpec((tile_p, width), lambda i: (tile_p_of(i), 0))
    pool_state_p = pl.BlockSpec((POOL_HIST, n_seq, POOL_WIDTH), lambda i: (0, 0, 0))
    conv_state_p = pl.BlockSpec((1, CONV_HIST, CONV_WIDTH),
                                lambda i: (tile_p_of(i) // tiles_per_seq, 0, 0))
    row_s_in = lambda width: pl.BlockSpec((seqs, steps, width), lambda i: (tile_s_of(i), 0, 0),
                                          pipeline_mode=pl.Buffered(1))
    pool_s = pl.BlockSpec((POOL_HIST, seqs, POOL_WIDTH), lambda i: (0, tile_s_of(i), 0),
                          pipeline_mode=pl.Buffered(1))
    conv_s = pl.BlockSpec((seqs, CONV_HIST, CONV_WIDTH), lambda i: (tile_s_of(i), 0, 0),
                          pipeline_mode=pl.Buffered(1))

    f32 = jnp.float32
    out_shape = [
        jax.ShapeDtypeStruct((tokens_p, D_MODEL), f32),
        jax.ShapeDtypeStruct((POOL_HIST, n_seq, POOL_WIDTH), f32),
        jax.ShapeDtypeStruct((n_seq, CONV_HIST, CONV_WIDTH), f32),
        jax.ShapeDtypeStruct((tokens_s // steps, steps, D_MODEL), f32),
        jax.ShapeDtypeStruct((POOL_HIST, tokens_s // steps, POOL_WIDTH), f32),
        jax.ShapeDtypeStruct((tokens_s // steps, CONV_HIST, CONV_WIDTH), f32),
    ]
    out_specs = [row_p(D_MODEL), pool_state_p, conv_state_p,
                 pl.BlockSpec((seqs, steps, D_MODEL), lambda i: (tile_s_of(i), 0, 0),
                              pipeline_mode=pl.Buffered(1)),
                 pool_s, conv_s]
    scratch = [pltpu.VMEM((POOL_PAD, POOL_WIDTH), f32), pltpu.VMEM((CONV_PAD, CONV_WIDTH), f32)]
    scratch += [pltpu.VMEM(v.shape, f32) for v in vectors]
    scratch += [pltpu.VMEM(w.shape, jnp.bfloat16) for w in matrices]

    kernel = functools.partial(_layer_kernel, n_prompt_tiles=n_p, tile_prompt=tile_p,
                               tiles_per_seq=tiles_per_seq, seqs=seqs, steps=steps)
    return pl.pallas_call(
        kernel,
        grid=(1 + n_p + n_s,),
        in_specs=([row_p(D_MODEL), row_p(D_PLE), row_s_in(D_MODEL), row_s_in(D_PLE),
                   pool_s, conv_s]
                  + [pl.BlockSpec(memory_space=pl.ANY)] * (N_VECTORS + N_MATRICES)),
        out_specs=out_specs,
        out_shape=out_shape,
        scratch_shapes=scratch,
        compiler_params=pltpu.CompilerParams(
            dimension_semantics=("arbitrary",),
            vmem_limit_bytes=VMEM_LIMIT_BYTES),
        name="hybrid_layer",
    )(x_p, p_p, x_s, p_s, pool_hist, conv_hist, *vectors, *matrices)


def kernel(x_prompt, x_sample, state_pool, state_conv, p_prompt, p_sample, g_mix, w_in, w_pool_group, pool_scale, w_pool_up, w_conv, w_conv_out, w_o, g_ffn, w_ffn_in, w_ffn_out, g_ple, w_ple, w_ple_gate, g_final):
    depth = g_mix.shape[0]
    assert depth == 1, "single-layer step"
    vec = lambda a: a.reshape(1, -1)
    vectors = (vec(g_mix[0]), vec(pool_scale[0]), vec(w_conv[0]), vec(g_ffn[0]), vec(g_ple[0]),
               vec(g_final))
    matrices = (w_in[0], w_pool_group[0].reshape(POOL_WIDTH, POOL_GROUP), w_pool_up[0],
                w_conv_out[0], w_o[0], w_ffn_in[0], w_ffn_out[0], w_ple[0], w_ple_gate[0])

    batch, seq, _ = x_prompt.shape
    dec_seq = x_sample.shape[1]

    y_p, pool_state_p, conv_state_p, y_s, pool_state_s, conv_state_s = _run_layer(
        x_prompt.reshape(batch * seq, D_MODEL), p_prompt[0].reshape(batch * seq, D_PLE),
        x_sample, p_sample[0],
        jnp.swapaxes(state_pool[0], 0, 1), state_conv[0],
        vectors, matrices, seq_len=seq, steps=dec_seq)

    return (y_p.reshape(batch, seq, D_MODEL), y_s,
            jnp.swapaxes(pool_state_p, 0, 1)[None], conv_state_p[None],
            jnp.swapaxes(pool_state_s, 0, 1)[None], conv_state_s[None])
```

```python
import functools

import jax
import jax.numpy as jnp
from jax.experimental import pallas as pl
from jax.experimental.pallas import tpu as pltpu

D_MODEL = 1024
POOL_WINDOWS = (2, 4, 8, 16)
POOL_GROUP = 128
POOL_WIDTH = 512
POOL_HIST = 15
CONV_WIDTH = 512
CONV_K = 3
CONV_HIST = 2
D_FF = 2816
D_PLE = 256
EPS = 1e-6

OFF_U = 0
OFF_B = OFF_U + POOL_WIDTH
OFF_C = OFF_B + CONV_WIDTH
OFF_H = OFF_C + CONV_WIDTH
OFF_GP = OFF_H + CONV_WIDTH
OFF_GC = OFF_GP + D_MODEL

V7X_VMEM_BYTES = 64 * 1024 * 1024
V7X_MXU_WIDTH = 256
SUBLANES = 8
DMA_PRIORITIES = 2

POOL_PAD = 16
CONV_PAD = 8
FF_CHUNK = V7X_MXU_WIDTH
TILE_PROMPT = 512
TILE_SAMPLE = 256
PROMPT_CHAINS = 2
SAMPLE_CHAINS = 2
PROMPT_LAG = 2
SAMPLE_LAG = 0
PREP_ROWS = 256
PREP_COLS = 1024
PREP_BUFS = 8
VMEM_RESERVE_BYTES = 4 * 1024 * 1024
VMEM_LIMIT_BYTES = V7X_VMEM_BYTES - VMEM_RESERVE_BYTES

N_VECTORS = 6
N_MATRICES = 9
N_INPUTS = 6
N_OUTPUTS = 6


def _rmsnorm(x, g):
    ms = jnp.mean(x * x, axis=-1, keepdims=True)
    return x * jax.lax.rsqrt(ms + EPS) * g


def _dot(a, b):
    return jnp.dot(a, b, preferred_element_type=jnp.float32)


def _conv_tap(w_conv_ref, k):
    return w_conv_ref[:, k * CONV_WIDTH:(k + 1) * CONV_WIDTH]


def _prep_weights(w_hbm, w_vmem, stage_ref, sem):
    chunks = []
    for src, dst in zip(w_hbm, w_vmem):
        n_rows, n_cols = dst.shape
        for r0 in range(0, n_rows, PREP_ROWS):
            for c0 in range(0, n_cols, PREP_COLS):
                chunks.append((src, dst, r0, min(PREP_ROWS, n_rows - r0),
                               c0, min(PREP_COLS, n_cols - c0)))

    def chunk_copy(k):
        src, _, r0, nr, c0, nc = chunks[k]
        slot = k % PREP_BUFS
        return pltpu.make_async_copy(src.at[r0:r0 + nr, c0:c0 + nc],
                                     stage_ref.at[slot, :nr, :nc], sem.at[slot])

    for k in range(min(PREP_BUFS, len(chunks))):
        chunk_copy(k).start(priority=k % DMA_PRIORITIES)
    for k, (_, dst, r0, nr, c0, nc) in enumerate(chunks):
        chunk_copy(k).wait()
        dst[r0:r0 + nr, c0:c0 + nc] = stage_ref[k % PREP_BUFS, :nr, :nc].astype(jnp.bfloat16)
        if k + PREP_BUFS < len(chunks):
            chunk_copy(k + PREP_BUFS).start(priority=(k + PREP_BUFS) % DMA_PRIORITIES)


def _layer_stages(x, p, vectors, w_vmem, act_ref, pool_conv):
    g_mix_ref, pool_scale_ref, _, g_ffn_ref, g_ple_ref, g_final_ref = vectors
    (w_in_ref, w_group_ref, w_pool_up_ref, w_conv_out_ref, w_o_ref, w_ffn_in_ref,
     w_ffn_out_ref, w_ple_ref, w_ple_gate_ref) = w_vmem
    bf16 = jnp.bfloat16

    hn = _rmsnorm(x, g_mix_ref[...]).astype(bf16)

    def proj(off, width):
        return _dot(hn, w_in_ref[:, off:off + width])

    u = proj(OFF_U, POOL_WIDTH)
    yield
    v = proj(OFF_C, CONV_WIDTH) * proj(OFF_H, CONV_WIDTH)
    yield
    pooled, yconv = pool_conv(u, v)
    conv_out = (proj(OFF_B, CONV_WIDTH) * yconv).astype(bf16)
    yield
    gate_pool = jax.nn.sigmoid(proj(OFF_GP, D_MODEL))
    yield
    gate_conv = jax.nn.sigmoid(proj(OFF_GC, D_MODEL))
    yield
    mixed = [_dot(d, w_group_ref[gi * POOL_GROUP:(gi + 1) * POOL_GROUP, :])
             for gi, d in enumerate(pooled)]
    pool_out = (jnp.concatenate(mixed, axis=-1) * pool_scale_ref[...]).astype(bf16)
    yield
    gated_pool = gate_pool * _dot(pool_out, w_pool_up_ref[...])
    yield
    merged = (gated_pool + gate_conv * _dot(conv_out, w_conv_out_ref[...])).astype(bf16)
    yield
    x = x + _dot(merged, w_o_ref[...])
    yield

    hn = _rmsnorm(x, g_ffn_ref[...]).astype(bf16)
    for c0 in range(0, D_FF, FF_CHUNK):
        gate = _dot(hn, w_ffn_in_ref[:, c0:c0 + FF_CHUNK])
        up = _dot(hn, w_ffn_in_ref[:, D_FF + c0:D_FF + c0 + FF_CHUNK])
        act_ref[:, c0:c0 + FF_CHUNK] = (jax.nn.silu(gate) * up).astype(bf16)
        yield
    x = x + _dot(act_ref[...], w_ffn_out_ref[...])
    yield

    hn = _rmsnorm(x, g_ple_ref[...]).astype(bf16)
    ple_gate = jax.nn.sigmoid(_dot(hn, w_ple_gate_ref[...]))
    yield
    x = x + ple_gate * _dot(p.astype(bf16), w_ple_ref[...])
    return _rmsnorm(x, g_final_ref[...])


def _run_interleaved(chains, lag):
    results = [None] * len(chains)
    live = list(range(len(chains)))
    rounds = 0
    while live:
        for i in list(live):
            if rounds < i * lag:
                continue
            try:
                next(chains[i])
            except StopIteration as done:
                results[i] = done.value
                live.remove(i)
        rounds += 1
    return results


def _prompt_tile(t, x_ref, p_ref, vectors, w_vmem, y_ref, pool_state_ref, conv_state_ref,
                 u_carry_ref, v_carry_ref, ext_ref, vext_ref, act_ref, *, tile, tiles_per_seq):
    wc = functools.partial(_conv_tap, vectors[2])
    rows = tile // PROMPT_CHAINS

    @pl.when(t % tiles_per_seq == 0)
    def _():
        u_carry_ref[...] = jnp.zeros((POOL_PAD, POOL_WIDTH), jnp.float32)
        v_carry_ref[...] = jnp.zeros((CONV_PAD, CONV_WIDTH), jnp.float32)

    ext_ref[:POOL_PAD, :] = u_carry_ref[...]
    vext_ref[:CONV_PAD, :] = v_carry_ref[...]

    def pool_conv_at(r0):
        def pool_conv(u, v):
            ext_ref[POOL_PAD + r0:POOL_PAD + r0 + rows, :] = u
            vext_ref[CONV_PAD + r0:CONV_PAD + r0 + rows, :] = v

            def ext_rows(back, c0, c1):
                return ext_ref[POOL_PAD + r0 - back:POOL_PAD + r0 - back + rows, c0:c1]

            def vext_rows(back):
                return vext_ref[CONV_PAD + r0 - back:CONV_PAD + r0 - back + rows, :]

            pos = (t % tiles_per_seq) * tile + r0 + jax.lax.broadcasted_iota(
                jnp.int32, (rows, POOL_GROUP), 0)
            valid = (pos + 1).astype(jnp.float32)
            pooled = []
            for gi, w in enumerate(POOL_WINDOWS):
                c0, c1 = gi * POOL_GROUP, (gi + 1) * POOL_GROUP
                win = ext_rows(0, c0, c1)
                for back in range(1, w):
                    win = win + ext_rows(back, c0, c1)
                mean = win / jnp.minimum(valid, float(w))
                pooled.append((mean - ext_rows(0, c0, c1)).astype(jnp.bfloat16))

            yconv = wc(0) * vext_rows(2) + wc(1) * vext_rows(1) + wc(2) * vext_rows(0)
            return pooled, yconv
        return pool_conv

    starts = range(0, tile, rows)
    ys = _run_interleaved([
        _layer_stages(x_ref[r0:r0 + rows, :], p_ref[r0:r0 + rows, :], vectors, w_vmem,
                      act_ref.at[r0:r0 + rows, :], pool_conv_at(r0)) for r0 in starts],
        PROMPT_LAG)
    for r0, y in zip(starts, ys):
        y_ref[r0:r0 + rows, :] = y

    u_tail = ext_ref[tile:tile + POOL_PAD, :]
    v_tail = vext_ref[tile:tile + CONV_PAD, :]
    u_carry_ref[...] = u_tail
    v_carry_ref[...] = v_tail
    seq = t // tiles_per_seq
    pool_state_ref[:, pl.ds(seq, 1), :] = u_tail[POOL_PAD - POOL_HIST:][:, None, :]
    conv_state_ref[0] = v_tail[CONV_PAD - CONV_HIST:]


def _sample_tile(x_ref, p_ref, pool_hist_ref, conv_hist_ref, vectors, w_vmem,
                 y_ref, pool_new_ref, conv_new_ref, act_ref, *, seqs, steps):
    wc = functools.partial(_conv_tap, vectors[2])
    n = seqs // SAMPLE_CHAINS

    def chain(q0):
        qs = slice(q0, q0 + n)
        by_step = lambda ref: jnp.concatenate([ref[qs, s, :] for s in range(steps)], axis=0)
        slab = lambda a, s: a[s * n:(s + 1) * n]

        def pool_conv(u, v):
            line = ([pool_hist_ref[s, qs, :] for s in range(POOL_HIST)]
                    + [slab(u, s) for s in range(steps)])
            pooled = []
            for gi, w in enumerate(POOL_WINDOWS):
                c0, c1 = gi * POOL_GROUP, (gi + 1) * POOL_GROUP
                per_step = []
                for s in range(steps):
                    now = POOL_HIST + s
                    win = line[now][:, c0:c1]
                    for back in range(1, w):
                        win = win + line[now - back][:, c0:c1]
                    per_step.append(win * (1.0 / w) - line[now][:, c0:c1])
                pooled.append(jnp.concatenate(per_step, axis=0).astype(jnp.bfloat16))
            for s in range(POOL_HIST):
                pool_new_ref[s, qs, :] = line[steps + s]

            vline = ([conv_hist_ref[qs, s, :] for s in range(CONV_HIST)]
                     + [slab(v, s) for s in range(steps)])
            yconv = jnp.concatenate(
                [wc(0) * vline[s] + wc(1) * vline[s + 1] + wc(2) * vline[s + 2]
                 for s in range(steps)], axis=0)
            for s in range(CONV_HIST):
                conv_new_ref[qs, s, :] = vline[steps + s]
            return pooled, yconv

        return _layer_stages(by_step(x_ref), by_step(p_ref), vectors, w_vmem,
                             act_ref.at[q0 * steps:(q0 + n) * steps, :], pool_conv)

    starts = range(0, seqs, n)
    for q0, y in zip(starts, _run_interleaved([chain(q0) for q0 in starts], SAMPLE_LAG)):
        for s in range(steps):
            y_ref[q0:q0 + n, s, :] = y[s * n:(s + 1) * n]


def _layer_kernel(*refs, n_prompt_tiles, tile_prompt, tiles_per_seq, seqs, steps):
    xp_ref, pp_ref, xs_ref, ps_ref, pool_hist_ref, conv_hist_ref = refs[:N_INPUTS]
    refs = refs[N_INPUTS:]
    vectors, refs = refs[:N_VECTORS], refs[N_VECTORS:]
    w_hbm, refs = refs[:N_MATRICES], refs[N_MATRICES:]
    (yp_ref, pool_state_ref, conv_state_ref, ys_ref, pool_new_ref,
     conv_new_ref) = refs[:N_OUTPUTS]
    refs = refs[N_OUTPUTS:]
    u_carry_ref, v_carry_ref = refs[:2]
    w_vmem = refs[2:]
    f32, bf16 = jnp.float32, jnp.bfloat16

    step = pl.program_id(0)

    @pl.when(step == 0)
    def _():
        pl.run_scoped(
            functools.partial(_prep_weights, w_hbm, w_vmem),
            pltpu.VMEM((PREP_BUFS, PREP_ROWS, PREP_COLS), f32),
            pltpu.SemaphoreType.DMA((PREP_BUFS,)))

    @pl.when(jnp.logical_and(step >= 1, step <= n_prompt_tiles))
    def _():
        pl.run_scoped(
            functools.partial(_prompt_tile, step - 1, xp_ref, pp_ref, vectors, w_vmem, yp_ref,
                              pool_state_ref, conv_state_ref, u_carry_ref, v_carry_ref,
                              tile=tile_prompt, tiles_per_seq=tiles_per_seq),
            pltpu.VMEM((POOL_PAD + tile_prompt, POOL_WIDTH), f32),
            pltpu.VMEM((CONV_PAD + tile_prompt, CONV_WIDTH), f32),
            pltpu.VMEM((tile_prompt, D_FF), bf16))

    @pl.when(step > n_prompt_tiles)
    def _():
        pl.run_scoped(
            functools.partial(_sample_tile, xs_ref, ps_ref, pool_hist_ref, conv_hist_ref, vectors,
                              w_vmem, ys_ref, pool_new_ref, conv_new_ref, seqs=seqs, steps=steps),
            pltpu.VMEM((seqs * steps, D_FF), bf16))


def _resident(shape):
    zeros = (0,) * len(shape)
    return pl.BlockSpec(shape, lambda i: zeros, pipeline_mode=pl.Buffered(1))


def _run_layer(x_p, p_p, x_s, p_s, pool_hist, conv_hist, vectors, matrices, *, seq_len, steps):
    tokens_p, tokens_s = x_p.shape[0], x_s.shape[0] * steps
    assert x_s.shape[1] == steps
    tile_p, tile_s = TILE_PROMPT, TILE_SAMPLE
    assert tokens_p % tile_p == 0 and seq_len % tile_p == 0
    assert tile_p % (PROMPT_CHAINS * SUBLANES) == 0 and tile_p // PROMPT_CHAINS >= POOL_PAD
    assert tokens_s % tile_s == 0 and tile_s % (SAMPLE_CHAINS * steps * SUBLANES) == 0
    assert len(vectors) == N_VECTORS and len(matrices) == N_MATRICES
    n_p, n_s = tokens_p // tile_p, tokens_s // tile_s
    tiles_per_seq = seq_len // tile_p
    n_seq = tokens_p // seq_len
    seqs = tile_s // steps

    tile_p_of = lambda i: jnp.clip(i - 1, 0, n_p - 1)
    tile_s_of = lambda i: jnp.clip(i - 1 - n_p, 0, n_s - 1)
    row_p = lambda width: pl.BlockSpec((tile_p, width), lambda i: (tile_p_of(i), 0))
    pool_state_p = pl.BlockSpec((POOL_HIST, n_seq, POOL_WIDTH), lambda i: (0, 0, 0))
    conv_state_p = pl.BlockSpec((1, CONV_HIST, CONV_WIDTH),
                                lambda i: (tile_p_of(i) // tiles_per_seq, 0, 0))
    row_s_in = lambda width: pl.BlockSpec((seqs, steps, width), lambda i: (tile_s_of(i), 0, 0),
                                          pipeline_mode=pl.Buffered(1))
    pool_s = pl.BlockSpec((POOL_HIST, seqs, POOL_WIDTH), lambda i: (0, tile_s_of(i), 0),
                          pipeline_mode=pl.Buffered(1))
    conv_s = pl.BlockSpec((seqs, CONV_HIST, CONV_WIDTH), lambda i: (tile_s_of(i), 0, 0),
                          pipeline_mode=pl.Buffered(1))

    f32 = jnp.float32
    out_shape = [
        jax.ShapeDtypeStruct((tokens_p, D_MODEL), f32),
        jax.ShapeDtypeStruct((POOL_HIST, n_seq, POOL_WIDTH), f32),
        jax.ShapeDtypeStruct((n_seq, CONV_HIST, CONV_WIDTH), f32),
        jax.ShapeDtypeStruct((tokens_s // steps, steps, D_MODEL), f32),
        jax.ShapeDtypeStruct((POOL_HIST, tokens_s // steps, POOL_WIDTH), f32),
        jax.ShapeDtypeStruct((tokens_s // steps, CONV_HIST, CONV_WIDTH), f32),
    ]
    out_specs = [row_p(D_MODEL), pool_state_p, conv_state_p,
                 pl.BlockSpec((seqs, steps, D_MODEL), lambda i: (tile_s_of(i), 0, 0),
                              pipeline_mode=pl.Buffered(1)),
                 pool_s, conv_s]
    scratch = [pltpu.VMEM((POOL_PAD, POOL_WIDTH), f32), pltpu.VMEM((CONV_PAD, CONV_WIDTH), f32)]
    scratch += [pltpu.VMEM(w.shape, jnp.bfloat16) for w in matrices]

    kernel = functools.partial(_layer_kernel, n_prompt_tiles=n_p, tile_prompt=tile_p,
                               tiles_per_seq=tiles_per_seq, seqs=seqs, steps=steps)
    return pl.pallas_call(
        kernel,
        grid=(1 + n_p + n_s,),
        in_specs=([row_p(D_MODEL), row_p(D_PLE), row_s_in(D_MODEL), row_s_in(D_PLE),
                   pool_s, conv_s]
                  + [_resident(v.shape) for v in vectors]
                  + [pl.BlockSpec(memory_space=pl.ANY)] * N_MATRICES),
        out_specs=out_specs,
        out_shape=out_shape,
        scratch_shapes=scratch,
        compiler_params=pltpu.CompilerParams(
            dimension_semantics=("arbitrary",),
            vmem_limit_bytes=VMEM_LIMIT_BYTES),
        name="hybrid_layer",
    )(x_p, p_p, x_s, p_s, pool_hist, conv_hist, *vectors, *matrices)


def kernel(x_prompt, x_sample, state_pool, state_conv, p_prompt, p_sample, g_mix, w_in, w_pool_group, pool_scale, w_pool_up, w_conv, w_conv_out, w_o, g_ffn, w_ffn_in, w_ffn_out, g_ple, w_ple, w_ple_gate, g_final):
    depth = g_mix.shape[0]
    assert depth == 1, "single-layer step"
    vec = lambda a: a.reshape(1, -1)
    vectors = (vec(g_mix[0]), vec(pool_scale[0]), vec(w_conv[0]), vec(g_ffn[0]), vec(g_ple[0]),
               vec(g_final))
    matrices = (w_in[0], w_pool_group[0].reshape(POOL_WIDTH, POOL_GROUP), w_pool_up[0],
                w_conv_out[0], w_o[0], w_ffn_in[0], w_ffn_out[0], w_ple[0], w_ple_gate[0])

    batch, seq, _ = x_prompt.shape
    dec_seq = x_sample.shape[1]

    y_p, pool_state_p, conv_state_p, y_s, pool_state_s, conv_state_s = _run_layer(
        x_prompt.reshape(batch * seq, D_MODEL), p_prompt[0].reshape(batch * seq, D_PLE),
        x_sample, p_sample[0],
        jnp.swapaxes(state_pool[0], 0, 1), state_conv[0],
        vectors, matrices, seq_len=seq, steps=dec_seq)

    return (y_p.reshape(batch, seq, D_MODEL), y_s,
            jnp.swapaxes(pool_state_p, 0, 1)[None], conv_state_p[None],
            jnp.swapaxes(pool_state_s, 0, 1)[None], conv_state_s[None])
```

```python
import functools

import jax
import jax.numpy as jnp
from jax.experimental import pallas as pl
from jax.experimental.pallas import tpu as pltpu

D_MODEL = 1024
POOL_WINDOWS = (2, 4, 8, 16)
POOL_GROUP = 128
POOL_WIDTH = 512
POOL_HIST = 15
CONV_WIDTH = 512
CONV_K = 3
CONV_HIST = 2
D_FF = 2816
D_PLE = 256
EPS = 1e-6

OFF_U = 0
OFF_B = OFF_U + POOL_WIDTH
OFF_C = OFF_B + CONV_WIDTH
OFF_H = OFF_C + CONV_WIDTH
OFF_GP = OFF_H + CONV_WIDTH
OFF_GC = OFF_GP + D_MODEL

V7X_VMEM_BYTES = 64 * 1024 * 1024
V7X_MXU_WIDTH = 256
SUBLANES = 8
DMA_PRIORITIES = 2

POOL_PAD = 16
CONV_PAD = 8
FF_CHUNK = 2 * V7X_MXU_WIDTH
TILE_PROMPT = 512
TILE_SAMPLE = 256
PROMPT_CHAINS = 2
SAMPLE_CHAINS = 2
PROMPT_LAG = 1
SAMPLE_LAG = 0
PREP_ROWS = 256
PREP_COLS = 1024
PREP_BUFS = 8
VMEM_RESERVE_BYTES = 4 * 1024 * 1024
VMEM_LIMIT_BYTES = V7X_VMEM_BYTES - VMEM_RESERVE_BYTES

N_VECTORS = 6
N_MATRICES = 9
N_INPUTS = 6
N_OUTPUTS = 6


def _rmsnorm(x, g):
    ms = jnp.mean(x * x, axis=-1, keepdims=True)
    return x * jax.lax.rsqrt(ms + EPS) * g


def _dot(a, b):
    return jnp.dot(a, b, preferred_element_type=jnp.float32)


def _conv_tap(w_conv_ref, k):
    return w_conv_ref[:, k * CONV_WIDTH:(k + 1) * CONV_WIDTH]


def _prep_weights(w_hbm, w_vmem, stage_ref, sem):
    chunks = []
    for src, dst in zip(w_hbm, w_vmem):
        n_rows, n_cols = dst.shape
        for r0 in range(0, n_rows, PREP_ROWS):
            for c0 in range(0, n_cols, PREP_COLS):
                chunks.append((src, dst, r0, min(PREP_ROWS, n_rows - r0),
                               c0, min(PREP_COLS, n_cols - c0)))

    def chunk_copy(k):
        src, _, r0, nr, c0, nc = chunks[k]
        slot = k % PREP_BUFS
        return pltpu.make_async_copy(src.at[r0:r0 + nr, c0:c0 + nc],
                                     stage_ref.at[slot, :nr, :nc], sem.at[slot])

    for k in range(min(PREP_BUFS, len(chunks))):
        chunk_copy(k).start(priority=k % DMA_PRIORITIES)
    for k, (_, dst, r0, nr, c0, nc) in enumerate(chunks):
        chunk_copy(k).wait()
        dst[r0:r0 + nr, c0:c0 + nc] = stage_ref[k % PREP_BUFS, :nr, :nc].astype(jnp.bfloat16)
        if k + PREP_BUFS < len(chunks):
            chunk_copy(k + PREP_BUFS).start(priority=(k + PREP_BUFS) % DMA_PRIORITIES)


def _layer_stages(x, p, vectors, w_vmem, act_ref, pool_conv):
    g_mix_ref, pool_scale_ref, _, g_ffn_ref, g_ple_ref, g_final_ref = vectors
    (w_in_ref, w_group_ref, w_pool_up_ref, w_conv_out_ref, w_o_ref, w_ffn_in_ref,
     w_ffn_out_ref, w_ple_ref, w_ple_gate_ref) = w_vmem
    bf16 = jnp.bfloat16

    hn = _rmsnorm(x, g_mix_ref[...]).astype(bf16)

    def proj(off, width):
        return _dot(hn, w_in_ref[:, off:off + width])

    u = proj(OFF_U, POOL_WIDTH)
    yield
    v = proj(OFF_C, CONV_WIDTH) * proj(OFF_H, CONV_WIDTH)
    yield
    pooled, yconv = pool_conv(u, v)
    conv_out = (proj(OFF_B, CONV_WIDTH) * yconv).astype(bf16)
    yield
    gate_pool = jax.nn.sigmoid(proj(OFF_GP, D_MODEL))
    yield
    mixed = [_dot(d, w_group_ref[gi * POOL_GROUP:(gi + 1) * POOL_GROUP, :])
             for gi, d in enumerate(pooled)]
    pool_out = (jnp.concatenate(mixed, axis=-1) * pool_scale_ref[...]).astype(bf16)
    gate_conv = jax.nn.sigmoid(proj(OFF_GC, D_MODEL))
    yield
    gated_pool = gate_pool * _dot(pool_out, w_pool_up_ref[...])
    merged = (gated_pool + gate_conv * _dot(conv_out, w_conv_out_ref[...])).astype(bf16)
    yield
    x = x + _dot(merged, w_o_ref[...])
    yield

    hn = _rmsnorm(x, g_ffn_ref[...]).astype(bf16)
    widths = [FF_CHUNK] * (D_FF // FF_CHUNK)
    if D_FF % FF_CHUNK:
        widths.insert(1, D_FF % FF_CHUNK)
    edges = [sum(widths[:k]) for k in range(len(widths) + 1)]
    for c0, c1 in zip(edges[:-1], edges[1:]):
        gate = _dot(hn, w_ffn_in_ref[:, c0:c1])
        up = _dot(hn, w_ffn_in_ref[:, D_FF + c0:D_FF + c1])
        act_ref[:, c0:c1] = (jax.nn.silu(gate) * up).astype(bf16)
        yield
    x = x + _dot(act_ref[...], w_ffn_out_ref[...])
    yield

    hn = _rmsnorm(x, g_ple_ref[...]).astype(bf16)
    ple_gate = jax.nn.sigmoid(_dot(hn, w_ple_gate_ref[...]))
    yield
    x = x + ple_gate * _dot(p.astype(bf16), w_ple_ref[...])
    return _rmsnorm(x, g_final_ref[...])


def _run_interleaved(chains, lag):
    results = [None] * len(chains)
    live = list(range(len(chains)))
    rounds = 0
    while live:
        for i in list(live):
            if rounds < i * lag:
                continue
            try:
                next(chains[i])
            except StopIteration as done:
                results[i] = done.value
                live.remove(i)
        rounds += 1
    return results


def _prompt_tile(t, x_ref, p_ref, vectors, w_vmem, y_ref, pool_state_ref, conv_state_ref,
                 u_carry_ref, v_carry_ref, ext_ref, vext_ref, act_ref, *, tile, tiles_per_seq):
    wc = functools.partial(_conv_tap, vectors[2])
    rows = tile // PROMPT_CHAINS

    @pl.when(t % tiles_per_seq == 0)
    def _():
        u_carry_ref[...] = jnp.zeros((POOL_PAD, POOL_WIDTH), jnp.float32)
        v_carry_ref[...] = jnp.zeros((CONV_PAD, CONV_WIDTH), jnp.float32)

    ext_ref[:POOL_PAD, :] = u_carry_ref[...]
    vext_ref[:CONV_PAD, :] = v_carry_ref[...]

    def pool_conv_at(r0):
        def pool_conv(u, v):
            ext_ref[POOL_PAD + r0:POOL_PAD + r0 + rows, :] = u
            vext_ref[CONV_PAD + r0:CONV_PAD + r0 + rows, :] = v

            def ext_rows(back, c0, c1):
                return ext_ref[POOL_PAD + r0 - back:POOL_PAD + r0 - back + rows, c0:c1]

            def vext_rows(back):
                return vext_ref[CONV_PAD + r0 - back:CONV_PAD + r0 - back + rows, :]

            pos = (t % tiles_per_seq) * tile + r0 + jax.lax.broadcasted_iota(
                jnp.int32, (rows, POOL_GROUP), 0)
            valid = (pos + 1).astype(jnp.float32)
            pooled = []
            for gi, w in enumerate(POOL_WINDOWS):
                c0, c1 = gi * POOL_GROUP, (gi + 1) * POOL_GROUP
                win = ext_rows(0, c0, c1)
                for back in range(1, w):
                    win = win + ext_rows(back, c0, c1)
                mean = win / jnp.minimum(valid, float(w))
                pooled.append((mean - ext_rows(0, c0, c1)).astype(jnp.bfloat16))

            yconv = wc(0) * vext_rows(2) + wc(1) * vext_rows(1) + wc(2) * vext_rows(0)
            return pooled, yconv
        return pool_conv

    starts = range(0, tile, rows)
    ys = _run_interleaved([
        _layer_stages(x_ref[r0:r0 + rows, :], p_ref[r0:r0 + rows, :], vectors, w_vmem,
                      act_ref.at[r0:r0 + rows, :], pool_conv_at(r0)) for r0 in starts],
        PROMPT_LAG)
    for r0, y in zip(starts, ys):
        y_ref[r0:r0 + rows, :] = y

    u_tail = ext_ref[tile:tile + POOL_PAD, :]
    v_tail = vext_ref[tile:tile + CONV_PAD, :]
    u_carry_ref[...] = u_tail
    v_carry_ref[...] = v_tail
    seq = t // tiles_per_seq
    pool_state_ref[:, pl.ds(seq, 1), :] = u_tail[POOL_PAD - POOL_HIST:][:, None, :]
    conv_state_ref[0] = v_tail[CONV_PAD - CONV_HIST:]


def _sample_tile(x_ref, p_ref, pool_hist_ref, conv_hist_ref, vectors, w_vmem,
                 y_ref, pool_new_ref, conv_new_ref, act_ref, *, seqs, steps):
    wc = functools.partial(_conv_tap, vectors[2])
    n = seqs // SAMPLE_CHAINS

    def chain(q0):
        qs = slice(q0, q0 + n)
        by_step = lambda ref: jnp.concatenate([ref[qs, s, :] for s in range(steps)], axis=0)
        slab = lambda a, s: a[s * n:(s + 1) * n]

        def pool_conv(u, v):
            line = ([pool_hist_ref[s, qs, :] for s in range(POOL_HIST)]
                    + [slab(u, s) for s in range(steps)])
            pooled = []
            for gi, w in enumerate(POOL_WINDOWS):
                c0, c1 = gi * POOL_GROUP, (gi + 1) * POOL_GROUP
                per_step = []
                for s in range(steps):
                    now = POOL_HIST + s
                    win = line[now][:, c0:c1]
                    for back in range(1, w):
                        win = win + line[now - back][:, c0:c1]
                    per_step.append(win * (1.0 / w) - line[now][:, c0:c1])
                pooled.append(jnp.concatenate(per_step, axis=0).astype(jnp.bfloat16))
            for s in range(POOL_HIST):
                pool_new_ref[s, qs, :] = line[steps + s]

            vline = ([conv_hist_ref[qs, s, :] for s in range(CONV_HIST)]
                     + [slab(v, s) for s in range(steps)])
            yconv = jnp.concatenate(
                [wc(0) * vline[s] + wc(1) * vline[s + 1] + wc(2) * vline[s + 2]
                 for s in range(steps)], axis=0)
            for s in range(CONV_HIST):
                conv_new_ref[qs, s, :] = vline[steps + s]
            return pooled, yconv

        return _layer_stages(by_step(x_ref), by_step(p_ref), vectors, w_vmem,
                             act_ref.at[q0 * steps:(q0 + n) * steps, :], pool_conv)

    starts = range(0, seqs, n)
    for q0, y in zip(starts, _run_interleaved([chain(q0) for q0 in starts], SAMPLE_LAG)):
        for s in range(steps):
            y_ref[q0:q0 + n, s, :] = y[s * n:(s + 1) * n]


def _layer_kernel(*refs, n_prompt_tiles, tile_prompt, tiles_per_seq, seqs, steps):
    xp_ref, pp_ref, xs_ref, ps_ref, pool_hist_ref, conv_hist_ref = refs[:N_INPUTS]
    refs = refs[N_INPUTS:]
    vectors, refs = refs[:N_VECTORS], refs[N_VECTORS:]
    w_hbm, refs = refs[:N_MATRICES], refs[N_MATRICES:]
    (yp_ref, pool_state_ref, conv_state_ref, ys_ref, pool_new_ref,
     conv_new_ref) = refs[:N_OUTPUTS]
    refs = refs[N_OUTPUTS:]
    u_carry_ref, v_carry_ref = refs[:2]
    w_vmem = refs[2:]
    f32, bf16 = jnp.float32, jnp.bfloat16

    step = pl.program_id(0)

    @pl.when(step == 0)
    def _():
        pl.run_scoped(
            functools.partial(_prep_weights, w_hbm, w_vmem),
            pltpu.VMEM((PREP_BUFS, PREP_ROWS, PREP_COLS), f32),
            pltpu.SemaphoreType.DMA((PREP_BUFS,)))

    @pl.when(jnp.logical_and(step >= 1, step <= n_prompt_tiles))
    def _():
        pl.run_scoped(
            functools.partial(_prompt_tile, step - 1, xp_ref, pp_ref, vectors, w_vmem, yp_ref,
                              pool_state_ref, conv_state_ref, u_carry_ref, v_carry_ref,
                              tile=tile_prompt, tiles_per_seq=tiles_per_seq),
            pltpu.VMEM((POOL_PAD + tile_prompt, POOL_WIDTH), f32),
            pltpu.VMEM((CONV_PAD + tile_prompt, CONV_WIDTH), f32),
            pltpu.VMEM((tile_prompt, D_FF), bf16))

    @pl.when(step > n_prompt_tiles)
    def _():
        pl.run_scoped(
            functools.partial(_sample_tile, xs_ref, ps_ref, pool_hist_ref, conv_hist_ref, vectors,
                              w_vmem, ys_ref, pool_new_ref, conv_new_ref, seqs=seqs, steps=steps),
            pltpu.VMEM((seqs * steps, D_FF), bf16))


def _resident(shape):
    zeros = (0,) * len(shape)
    return pl.BlockSpec(shape, lambda i: zeros, pipeline_mode=pl.Buffered(1))


def _run_layer(x_p, p_p, x_s, p_s, pool_hist, conv_hist, vectors, matrices, *, seq_len, steps):
    tokens_p, tokens_s = x_p.shape[0], x_s.shape[0] * steps
    assert x_s.shape[1] == steps
    tile_p, tile_s = TILE_PROMPT, TILE_SAMPLE
    assert tokens_p % tile_p == 0 and seq_len % tile_p == 0
    assert tile_p % (PROMPT_CHAINS * SUBLANES) == 0 and tile_p // PROMPT_CHAINS >= POOL_PAD
    assert tokens_s % tile_s == 0 and tile_s % (SAMPLE_CHAINS * steps * SUBLANES) == 0
    assert len(vectors) == N_VECTORS and len(matrices) == N_MATRICES
    n_p, n_s = tokens_p // tile_p, tokens_s // tile_s
    tiles_per_seq = seq_len // tile_p
    n_seq = tokens_p // seq_len
    seqs = tile_s // steps

    tile_p_of = lambda i: jnp.clip(i - 1, 0, n_p - 1)
    tile_s_of = lambda i: jnp.clip(i - 1 - n_p, 0, n_s - 1)
    row_p = lambda width: pl.BlockSpec((tile_p, width), lambda i: (tile_p_of(i), 0))
    pool_state_p = pl.BlockSpec((POOL_HIST, n_seq, POOL_WIDTH), lambda i: (0, 0, 0))
    conv_state_p = pl.BlockSpec((1, CONV_HIST, CONV_WIDTH),
                                lambda i: (tile_p_of(i) // tiles_per_seq, 0, 0))
    row_s_in = lambda width: pl.BlockSpec((seqs, steps, width), lambda i: (tile_s_of(i), 0, 0),
                                          pipeline_mode=pl.Buffered(1))
    pool_s = pl.BlockSpec((POOL_HIST, seqs, POOL_WIDTH), lambda i: (0, tile_s_of(i), 0),
                          pipeline_mode=pl.Buffered(1))
    conv_s = pl.BlockSpec((seqs, CONV_HIST, CONV_WIDTH), lambda i: (tile_s_of(i), 0, 0),
                          pipeline_mode=pl.Buffered(1))

    f32 = jnp.float32
    out_shape = [
        jax.ShapeDtypeStruct((tokens_p, D_MODEL), f32),
        jax.ShapeDtypeStruct((POOL_HIST, n_seq, POOL_WIDTH), f32),
        jax.ShapeDtypeStruct((n_seq, CONV_HIST, CONV_WIDTH), f32),
        jax.ShapeDtypeStruct((tokens_s // steps, steps, D_MODEL), f32),
        jax.ShapeDtypeStruct((POOL_HIST, tokens_s // steps, POOL_WIDTH), f32),
        jax.ShapeDtypeStruct((tokens_s // steps, CONV_HIST, CONV_WIDTH), f32),
    ]
    out_specs = [row_p(D_MODEL), pool_state_p, conv_state_p,
                 pl.BlockSpec((seqs, steps, D_MODEL), lambda i: (tile_s_of(i), 0, 0),
                              pipeline_mode=pl.Buffered(1)),
                 pool_s, conv_s]
    scratch = [pltpu.VMEM((POOL_PAD, POOL_WIDTH), f32), pltpu.VMEM((CONV_PAD, CONV_WIDTH), f32)]
    scratch += [pltpu.VMEM(w.shape, jnp.bfloat16) for w in matrices]

    kernel = functools.partial(_layer_kernel, n_prompt_tiles=n_p, tile_prompt=tile_p,
                               tiles_per_seq=tiles_per_seq, seqs=seqs, steps=steps)
    return pl.pallas_call(
        kernel,
        grid=(1 + n_p + n_s,),
        in_specs=([row_p(D_MODEL), row_p(D_PLE), row_s_in(D_MODEL), row_s_in(D_PLE),
                   pool_s, conv_s]
                  + [_resident(v.shape) for v in vectors]
                  + [pl.BlockSpec(memory_space=pl.ANY)] * N_MATRICES),
        out_specs=out_specs,
        out_shape=out_shape,
        scratch_shapes=scratch,
        compiler_params=pltpu.CompilerParams(
            dimension_semantics=("arbitrary",),
            vmem_limit_bytes=VMEM_LIMIT_BYTES),
        name="hybrid_layer",
    )(x_p, p_p, x_s, p_s, pool_hist, conv_hist, *vectors, *matrices)


def kernel(x_prompt, x_sample, state_pool, state_conv, p_prompt, p_sample, g_mix, w_in, w_pool_group, pool_scale, w_pool_up, w_conv, w_conv_out, w_o, g_ffn, w_ffn_in, w_ffn_out, g_ple, w_ple, w_ple_gate, g_final):
    depth = g_mix.shape[0]
    assert depth == 1, "single-layer step"
    vec = lambda a: a.reshape(1, -1)
    vectors = (vec(g_mix[0]), vec(pool_scale[0]), vec(w_conv[0]), vec(g_ffn[0]), vec(g_ple[0]),
               vec(g_final))
    matrices = (w_in[0], w_pool_group[0].reshape(POOL_WIDTH, POOL_GROUP), w_pool_up[0],
                w_conv_out[0], w_o[0], w_ffn_in[0], w_ffn_out[0], w_ple[0], w_ple_gate[0])

    batch, seq, _ = x_prompt.shape
    dec_seq = x_sample.shape[1]

    y_p, pool_state_p, conv_state_p, y_s, pool_state_s, conv_state_s = _run_layer(
        x_prompt.reshape(batch * seq, D_MODEL), p_prompt[0].reshape(batch * seq, D_PLE),
        x_sample, p_sample[0],
        jnp.swapaxes(state_pool[0], 0, 1), state_conv[0],
        vectors, matrices, seq_len=seq, steps=dec_seq)

    return (y_p.reshape(batch, seq, D_MODEL), y_s,
            jnp.swapaxes(pool_state_p, 0, 1)[None], conv_state_p[None],
            jnp.swapaxes(pool_state_s, 0, 1)[None], conv_state_s[None])
```

```python
import functools

import jax
import jax.numpy as jnp
from jax.experimental import pallas as pl
from jax.experimental.pallas import tpu as pltpu

D_MODEL = 1024
POOL_WINDOWS = (2, 4, 8, 16)
POOL_GROUP = 128
POOL_WIDTH = 512
POOL_HIST = 15
CONV_WIDTH = 512
CONV_K = 3
CONV_HIST = 2
D_FF = 2816
D_PLE = 256
EPS = 1e-6

OFF_U = 0
OFF_B = OFF_U + POOL_WIDTH
OFF_C = OFF_B + CONV_WIDTH
OFF_H = OFF_C + CONV_WIDTH
OFF_GP = OFF_H + CONV_WIDTH
OFF_GC = OFF_GP + D_MODEL

V7X_VMEM_BYTES = 64 * 1024 * 1024
V7X_MXU_WIDTH = 256
SUBLANES = 8
DMA_PRIORITIES = 2

POOL_PAD = 16
CONV_PAD = 8
FF_CHUNK = V7X_MXU_WIDTH
TILE_PROMPT = 512
TILE_SAMPLE = 256
PROMPT_CHAINS = 2
SAMPLE_CHAINS = 2
PROMPT_LAG = 1
SAMPLE_LAG = 0
PREP_ROWS = 256
PREP_COLS = 1024
PREP_BUFS = 8
VMEM_RESERVE_BYTES = 4 * 1024 * 1024
VMEM_LIMIT_BYTES = V7X_VMEM_BYTES - VMEM_RESERVE_BYTES

N_VECTORS = 6
N_MATRICES = 9
N_INPUTS = 6
N_OUTPUTS = 6


def _rmsnorm(x, g):
    ms = jnp.mean(x * x, axis=-1, keepdims=True)
    return x * jax.lax.rsqrt(ms + EPS) * g


def _dot(a, b):
    return jnp.dot(a, b, preferred_element_type=jnp.float32)


def _conv_tap(w_conv_ref, k):
    return w_conv_ref[:, k * CONV_WIDTH:(k + 1) * CONV_WIDTH]


def _prep_weights(w_hbm, w_vmem, stage_ref, sem):
    chunks = []
    for src, dst in zip(w_hbm, w_vmem):
        n_rows, n_cols = dst.shape
        for r0 in range(0, n_rows, PREP_ROWS):
            for c0 in range(0, n_cols, PREP_COLS):
                chunks.append((src, dst, r0, min(PREP_ROWS, n_rows - r0),
                               c0, min(PREP_COLS, n_cols - c0)))

    def chunk_copy(k):
        src, _, r0, nr, c0, nc = chunks[k]
        slot = k % PREP_BUFS
        return pltpu.make_async_copy(src.at[r0:r0 + nr, c0:c0 + nc],
                                     stage_ref.at[slot, :nr, :nc], sem.at[slot])

    for k in range(min(PREP_BUFS, len(chunks))):
        chunk_copy(k).start(priority=k % DMA_PRIORITIES)
    for k, (_, dst, r0, nr, c0, nc) in enumerate(chunks):
        chunk_copy(k).wait()
        dst[r0:r0 + nr, c0:c0 + nc] = stage_ref[k % PREP_BUFS, :nr, :nc].astype(jnp.bfloat16)
        if k + PREP_BUFS < len(chunks):
            chunk_copy(k + PREP_BUFS).start(priority=(k + PREP_BUFS) % DMA_PRIORITIES)


def _layer_stages(x, p, vectors, w_vmem, act_ref, pool_conv):
    g_mix_ref, pool_scale_ref, _, g_ffn_ref, g_ple_ref, g_final_ref = vectors
    (w_in_ref, w_group_ref, w_pool_up_ref, w_conv_out_ref, w_o_ref, w_ffn_in_ref,
     w_ffn_out_ref, w_ple_ref, w_ple_gate_ref) = w_vmem
    bf16 = jnp.bfloat16

    hn = _rmsnorm(x, g_mix_ref[...]).astype(bf16)

    def proj(off, width):
        return _dot(hn, w_in_ref[:, off:off + width])

    u = proj(OFF_U, POOL_WIDTH)
    yield
    v = proj(OFF_C, CONV_WIDTH) * proj(OFF_H, CONV_WIDTH)
    yield
    pooled, yconv = pool_conv(u, v)
    conv_out = (proj(OFF_B, CONV_WIDTH) * yconv).astype(bf16)
    yield
    gate_pool = jax.nn.sigmoid(proj(OFF_GP, D_MODEL))
    yield
    gate_conv = jax.nn.sigmoid(proj(OFF_GC, D_MODEL))
    yield
    mixed = [_dot(d, w_group_ref[gi * POOL_GROUP:(gi + 1) * POOL_GROUP, :])
             for gi, d in enumerate(pooled)]
    pool_out = (jnp.concatenate(mixed, axis=-1) * pool_scale_ref[...]).astype(bf16)
    yield
    gated_pool = gate_pool * _dot(pool_out, w_pool_up_ref[...])
    yield
    merged = (gated_pool + gate_conv * _dot(conv_out, w_conv_out_ref[...])).astype(bf16)
    yield
    x = x + _dot(merged, w_o_ref[...])
    yield

    hn = _rmsnorm(x, g_ffn_ref[...]).astype(bf16)
    for c0 in range(0, D_FF, FF_CHUNK):
        gate = _dot(hn, w_ffn_in_ref[:, c0:c0 + FF_CHUNK])
        up = _dot(hn, w_ffn_in_ref[:, D_FF + c0:D_FF + c0 + FF_CHUNK])
        act_ref[:, c0:c0 + FF_CHUNK] = (jax.nn.silu(gate) * up).astype(bf16)
        yield
    x = x + _dot(act_ref[...], w_ffn_out_ref[...])
    yield

    hn = _rmsnorm(x, g_ple_ref[...]).astype(bf16)
    ple_gate = jax.nn.sigmoid(_dot(hn, w_ple_gate_ref[...]))
    yield
    x = x + ple_gate * _dot(p.astype(bf16), w_ple_ref[...])
    return _rmsnorm(x, g_final_ref[...])


def _run_interleaved(chains, lag):
    results = [None] * len(chains)
    live = list(range(len(chains)))
    rounds = 0
    while live:
        for i in list(live):
            if rounds < i * lag:
                continue
            try:
                next(chains[i])
            except StopIteration as done:
                results[i] = done.value
                live.remove(i)
        rounds += 1
    return results


def _prompt_tile(t, x_ref, p_ref, vectors, w_vmem, y_ref, pool_state_ref, conv_state_ref,
                 u_carry_ref, v_carry_ref, ext_ref, vext_ref, act_ref, *, tile, tiles_per_seq):
    wc = functools.partial(_conv_tap, vectors[2])
    rows = tile // PROMPT_CHAINS

    @pl.when(t % tiles_per_seq == 0)
    def _():
        u_carry_ref[...] = jnp.zeros((POOL_PAD, POOL_WIDTH), jnp.float32)
        v_carry_ref[...] = jnp.zeros((CONV_PAD, CONV_WIDTH), jnp.float32)

    ext_ref[:POOL_PAD, :] = u_carry_ref[...]
    vext_ref[:CONV_PAD, :] = v_carry_ref[...]

    def pool_conv_at(r0):
        def pool_conv(u, v):
            ext_ref[POOL_PAD + r0:POOL_PAD + r0 + rows, :] = u
            vext_ref[CONV_PAD + r0:CONV_PAD + r0 + rows, :] = v

            def ext_rows(back, c0, c1):
                return ext_ref[POOL_PAD + r0 - back:POOL_PAD + r0 - back + rows, c0:c1]

            def vext_rows(back):
                return vext_ref[CONV_PAD + r0 - back:CONV_PAD + r0 - back + rows, :]

            pos = (t % tiles_per_seq) * tile + r0 + jax.lax.broadcasted_iota(
                jnp.int32, (rows, POOL_GROUP), 0)
            valid = (pos + 1).astype(jnp.float32)
            pooled = []
            for gi, w in enumerate(POOL_WINDOWS):
                c0, c1 = gi * POOL_GROUP, (gi + 1) * POOL_GROUP
                win = ext_rows(0, c0, c1)
                for back in range(1, w):
                    win = win + ext_rows(back, c0, c1)
                mean = win / jnp.minimum(valid, float(w))
                pooled.append((mean - ext_rows(0, c0, c1)).astype(jnp.bfloat16))

            yconv = wc(0) * vext_rows(2) + wc(1) * vext_rows(1) + wc(2) * vext_rows(0)
            return pooled, yconv
        return pool_conv

    starts = range(0, tile, rows)
    ys = _run_interleaved([
        _layer_stages(x_ref[r0:r0 + rows, :], p_ref[r0:r0 + rows, :], vectors, w_vmem,
                      act_ref.at[r0:r0 + rows, :], pool_conv_at(r0)) for r0 in starts],
        PROMPT_LAG)
    for r0, y in zip(starts, ys):
        y_ref[r0:r0 + rows, :] = y

    u_tail = ext_ref[tile:tile + POOL_PAD, :]
    v_tail = vext_ref[tile:tile + CONV_PAD, :]
    u_carry_ref[...] = u_tail
    v_carry_ref[...] = v_tail
    seq = t // tiles_per_seq
    pool_state_ref[:, pl.ds(seq, 1), :] = u_tail[POOL_PAD - POOL_HIST:][:, None, :]
    conv_state_ref[0] = v_tail[CONV_PAD - CONV_HIST:]


def _sample_tile(x_ref, p_ref, pool_hist_ref, conv_hist_ref, vectors, w_vmem,
                 y_ref, pool_new_ref, conv_new_ref, act_ref, *, seqs, steps):
    wc = functools.partial(_conv_tap, vectors[2])
    n = seqs // SAMPLE_CHAINS

    def chain(q0):
        qs = slice(q0, q0 + n)
        by_step = lambda ref: jnp.concatenate([ref[qs, s, :] for s in range(steps)], axis=0)
        slab = lambda a, s: a[s * n:(s + 1) * n]

        def pool_conv(u, v):
            line = ([pool_hist_ref[s, qs, :] for s in range(POOL_HIST)]
                    + [slab(u, s) for s in range(steps)])
            pooled = []
            for gi, w in enumerate(POOL_WINDOWS):
                c0, c1 = gi * POOL_GROUP, (gi + 1) * POOL_GROUP
                per_step = []
                for s in range(steps):
                    now = POOL_HIST + s
                    win = line[now][:, c0:c1]
                    for back in range(1, w):
                        win = win + line[now - back][:, c0:c1]
                    per_step.append(win * (1.0 / w) - line[now][:, c0:c1])
                pooled.append(jnp.concatenate(per_step, axis=0).astype(jnp.bfloat16))
            for s in range(POOL_HIST):
                pool_new_ref[s, qs, :] = line[steps + s]

            vline = ([conv_hist_ref[qs, s, :] for s in range(CONV_HIST)]
                     + [slab(v, s) for s in range(steps)])
            yconv = jnp.concatenate(
                [wc(0) * vline[s] + wc(1) * vline[s + 1] + wc(2) * vline[s + 2]
                 for s in range(steps)], axis=0)
            for s in range(CONV_HIST):
                conv_new_ref[qs, s, :] = vline[steps + s]
            return pooled, yconv

        return _layer_stages(by_step(x_ref), by_step(p_ref), vectors, w_vmem,
                             act_ref.at[q0 * steps:(q0 + n) * steps, :], pool_conv)

    starts = range(0, seqs, n)
    for q0, y in zip(starts, _run_interleaved([chain(q0) for q0 in starts], SAMPLE_LAG)):
        for s in range(steps):
            y_ref[q0:q0 + n, s, :] = y[s * n:(s + 1) * n]


def _layer_kernel(*refs, n_prompt_tiles, tile_prompt, tiles_per_seq, seqs, steps):
    xp_ref, pp_ref, xs_ref, ps_ref, pool_hist_ref, conv_hist_ref = refs[:N_INPUTS]
    refs = refs[N_INPUTS:]
    vectors, refs = refs[:N_VECTORS], refs[N_VECTORS:]
    w_hbm, refs = refs[:N_MATRICES], refs[N_MATRICES:]
    (yp_ref, pool_state_ref, conv_state_ref, ys_ref, pool_new_ref,
     conv_new_ref) = refs[:N_OUTPUTS]
    refs = refs[N_OUTPUTS:]
    u_carry_ref, v_carry_ref = refs[:2]
    w_vmem = refs[2:]
    f32, bf16 = jnp.float32, jnp.bfloat16

    step = pl.program_id(0)

    @pl.when(step == 0)
    def _():
        for _ in range(2):
            pl.run_scoped(
                functools.partial(_prep_weights, w_hbm, w_vmem),
                pltpu.VMEM((PREP_BUFS, PREP_ROWS, PREP_COLS), f32),
                pltpu.SemaphoreType.DMA((PREP_BUFS,)))

    @pl.when(jnp.logical_and(step >= 1, step <= n_prompt_tiles))
    def _():
        pl.run_scoped(
            functools.partial(_prompt_tile, step - 1, xp_ref, pp_ref, vectors, w_vmem, yp_ref,
                              pool_state_ref, conv_state_ref, u_carry_ref, v_carry_ref,
                              tile=tile_prompt, tiles_per_seq=tiles_per_seq),
            pltpu.VMEM((POOL_PAD + tile_prompt, POOL_WIDTH), f32),
            pltpu.VMEM((CONV_PAD + tile_prompt, CONV_WIDTH), f32),
            pltpu.VMEM((tile_prompt, D_FF), bf16))

    @pl.when(step > n_prompt_tiles)
    def _():
        pl.run_scoped(
            functools.partial(_sample_tile, xs_ref, ps_ref, pool_hist_ref, conv_hist_ref, vectors,
                              w_vmem, ys_ref, pool_new_ref, conv_new_ref, seqs=seqs, steps=steps),
            pltpu.VMEM((seqs * steps, D_FF), bf16))


def _resident(shape):
    zeros = (0,) * len(shape)
    return pl.BlockSpec(shape, lambda i: zeros, pipeline_mode=pl.Buffered(1))


def _run_layer(x_p, p_p, x_s, p_s, pool_hist, conv_hist, vectors, matrices, *, seq_len, steps):
    tokens_p, tokens_s = x_p.shape[0], x_s.shape[0] * steps
    assert x_s.shape[1] == steps
    tile_p, tile_s = TILE_PROMPT, TILE_SAMPLE
    assert tokens_p % tile_p == 0 and seq_len % tile_p == 0
    assert tile_p % (PROMPT_CHAINS * SUBLANES) == 0 and tile_p // PROMPT_CHAINS >= POOL_PAD
    assert tokens_s % tile_s == 0 and tile_s % (SAMPLE_CHAINS * steps * SUBLANES) == 0
    assert len(vectors) == N_VECTORS and len(matrices) == N_MATRICES
    n_p, n_s = tokens_p // tile_p, tokens_s // tile_s
    tiles_per_seq = seq_len // tile_p
    n_seq = tokens_p // seq_len
    seqs = tile_s // steps

    tile_p_of = lambda i: jnp.clip(i - 1, 0, n_p - 1)
    tile_s_of = lambda i: jnp.clip(i - 1 - n_p, 0, n_s - 1)
    row_p = lambda width: pl.BlockSpec((tile_p, width), lambda i: (tile_p_of(i), 0))
    pool_state_p = pl.BlockSpec((POOL_HIST, n_seq, POOL_WIDTH), lambda i: (0, 0, 0))
    conv_state_p = pl.BlockSpec((1, CONV_HIST, CONV_WIDTH),
                                lambda i: (tile_p_of(i) // tiles_per_seq, 0, 0))
    row_s_in = lambda width: pl.BlockSpec((seqs, steps, width), lambda i: (tile_s_of(i), 0, 0),
                                          pipeline_mode=pl.Buffered(1))
    pool_s = pl.BlockSpec((POOL_HIST, seqs, POOL_WIDTH), lambda i: (0, tile_s_of(i), 0),
                          pipeline_mode=pl.Buffered(1))
    conv_s = pl.BlockSpec((seqs, CONV_HIST, CONV_WIDTH), lambda i: (tile_s_of(i), 0, 0),
                          pipeline_mode=pl.Buffered(1))

    f32 = jnp.float32
    out_shape = [
        jax.ShapeDtypeStruct((tokens_p, D_MODEL), f32),
        jax.ShapeDtypeStruct((POOL_HIST, n_seq, POOL_WIDTH), f32),
        jax.ShapeDtypeStruct((n_seq, CONV_HIST, CONV_WIDTH), f32),
        jax.ShapeDtypeStruct((tokens_s // steps, steps, D_MODEL), f32),
        jax.ShapeDtypeStruct((POOL_HIST, tokens_s // steps, POOL_WIDTH), f32),
        jax.ShapeDtypeStruct((tokens_s // steps, CONV_HIST, CONV_WIDTH), f32),
    ]
    out_specs = [row_p(D_MODEL), pool_state_p, conv_state_p,
                 pl.BlockSpec((seqs, steps, D_MODEL), lambda i: (tile_s_of(i), 0, 0),
                              pipeline_mode=pl.Buffered(1)),
                 pool_s, conv_s]
    scratch = [pltpu.VMEM((POOL_PAD, POOL_WIDTH), f32), pltpu.VMEM((CONV_PAD, CONV_WIDTH), f32)]
    scratch += [pltpu.VMEM(w.shape, jnp.bfloat16) for w in matrices]

    kernel = functools.partial(_layer_kernel, n_prompt_tiles=n_p, tile_prompt=tile_p,
                               tiles_per_seq=tiles_per_seq, seqs=seqs, steps=steps)
    return pl.pallas_call(
        kernel,
        grid=(1 + n_p + n_s,),
        in_specs=([row_p(D_MODEL), row_p(D_PLE), row_s_in(D_MODEL), row_s_in(D_PLE),
                   pool_s, conv_s]
                  + [_resident(v.shape) for v in vectors]
                  + [pl.BlockSpec(memory_space=pl.ANY)] * N_MATRICES),
        out_specs=out_specs,
        out_shape=out_shape,
        scratch_shapes=scratch,
        compiler_params=pltpu.CompilerParams(
            dimension_semantics=("arbitrary",),
            vmem_limit_bytes=VMEM_LIMIT_BYTES),
        name="hybrid_layer",
    )(x_p, p_p, x_s, p_s, pool_hist, conv_hist, *vectors, *matrices)


def kernel(x_prompt, x_sample, state_pool, state_conv, p_prompt, p_sample, g_mix, w_in, w_pool_group, pool_scale, w_pool_up, w_conv, w_conv_out, w_o, g_ffn, w_ffn_in, w_ffn_out, g_ple, w_ple, w_ple_gate, g_final):
    depth = g_mix.shape[0]
    assert depth == 1, "single-layer step"
    vec = lambda a: a.reshape(1, -1)
    vectors = (vec(g_mix[0]), vec(pool_scale[0]), vec(w_conv[0]), vec(g_ffn[0]), vec(g_ple[0]),
               vec(g_final))
    matrices = (w_in[0], w_pool_group[0].reshape(POOL_WIDTH, POOL_GROUP), w_pool_up[0],
                w_conv_out[0], w_o[0], w_ffn_in[0], w_ffn_out[0], w_ple[0], w_ple_gate[0])

    batch, seq, _ = x_prompt.shape
    dec_seq = x_sample.shape[1]

    y_p, pool_state_p, conv_state_p, y_s, pool_state_s, conv_state_s = _run_layer(
        x_prompt.reshape(batch * seq, D_MODEL), p_prompt[0].reshape(batch * seq, D_PLE),
        x_sample, p_sample[0],
        jnp.swapaxes(state_pool[0], 0, 1), state_conv[0],
        vectors, matrices, seq_len=seq, steps=dec_seq)

    return (y_p.reshape(batch, seq, D_MODEL), y_s,
            jnp.swapaxes(pool_state_p, 0, 1)[None], conv_state_p[None],
            jnp.swapaxes(pool_state_s, 0, 1)[None], conv_state_s[None])
```

```python
import functools

import jax
import jax.numpy as jnp
from jax.experimental import pallas as pl
from jax.experimental.pallas import tpu as pltpu

D_MODEL = 1024
POOL_WINDOWS = (2, 4, 8, 16)
POOL_GROUP = 128
POOL_WIDTH = 512
POOL_HIST = 15
CONV_WIDTH = 512
CONV_K = 3
CONV_HIST = 2
D_FF = 2816
D_PLE = 256
EPS = 1e-6

OFF_U = 0
OFF_B = OFF_U + POOL_WIDTH
OFF_C = OFF_B + CONV_WIDTH
OFF_H = OFF_C + CONV_WIDTH
OFF_GP = OFF_H + CONV_WIDTH
OFF_GC = OFF_GP + D_MODEL

V7X_VMEM_BYTES = 64 * 1024 * 1024
V7X_MXU_WIDTH = 256
SUBLANES = 8
DMA_PRIORITIES = 2

POOL_PAD = 16
CONV_PAD = 8
FF_CHUNK = V7X_MXU_WIDTH
TILE_PROMPT = 512
TILE_SAMPLE = 256
PROMPT_CHAINS = 2
SAMPLE_CHAINS = 2
PROMPT_LAG = 1
SAMPLE_LAG = 0
PREP_ROWS = 256
PREP_COLS = 1024
PREP_BUFS = 8
VMEM_RESERVE_BYTES = 4 * 1024 * 1024
VMEM_LIMIT_BYTES = V7X_VMEM_BYTES - VMEM_RESERVE_BYTES

N_VECTORS = 6
N_MATRICES = 9
N_INPUTS = 6
N_OUTPUTS = 6


def _rmsnorm(x, g):
    ms = jnp.mean(x * x, axis=-1, keepdims=True)
    return x * jax.lax.rsqrt(ms + EPS) * g


def _dot(a, b):
    return jnp.dot(a, b, preferred_element_type=jnp.float32)


def _conv_tap(w_conv_ref, k):
    return w_conv_ref[:, k * CONV_WIDTH:(k + 1) * CONV_WIDTH]


def _prep_weights(w_hbm, w_vmem, stage_ref, sem):
    chunks = []
    for src, dst in zip(w_hbm, w_vmem):
        n_rows, n_cols = dst.shape
        for r0 in range(0, n_rows, PREP_ROWS):
            for c0 in range(0, n_cols, PREP_COLS):
                chunks.append((src, dst, r0, min(PREP_ROWS, n_rows - r0),
                               c0, min(PREP_COLS, n_cols - c0)))

    def chunk_copy(k):
        src, _, r0, nr, c0, nc = chunks[k]
        slot = k % PREP_BUFS
        return pltpu.make_async_copy(src.at[r0:r0 + nr, c0:c0 + nc],
                                     stage_ref.at[slot, :nr, :nc], sem.at[slot])

    for k in range(min(PREP_BUFS, len(chunks))):
        chunk_copy(k).start(priority=k % DMA_PRIORITIES)
    for k, (_, dst, r0, nr, c0, nc) in enumerate(chunks):
        chunk_copy(k).wait()
        dst[r0:r0 + nr, c0:c0 + nc] = stage_ref[k % PREP_BUFS, :nr, :nc].astype(jnp.bfloat16)
        if k + PREP_BUFS < len(chunks):
            chunk_copy(k + PREP_BUFS).start(priority=(k + PREP_BUFS) % DMA_PRIORITIES)


def _prompt_in_copies(t, slot, x_hbm, p_hbm, x_buf, p_buf, sem, tile):
    rows = pl.ds(t * tile, tile)
    return (pltpu.make_async_copy(x_hbm.at[rows, :], x_buf.at[slot], sem.at[slot, 0]),
            pltpu.make_async_copy(p_hbm.at[rows, :], p_buf.at[slot], sem.at[slot, 1]))


def _prompt_out_copy(t, slot, y_buf, y_hbm, sem, tile):
    return pltpu.make_async_copy(y_buf.at[slot], y_hbm.at[pl.ds(t * tile, tile), :], sem.at[slot])


def _layer_stages(x, p, vectors, w_vmem, act_ref, pool_conv):
    g_mix_ref, pool_scale_ref, _, g_ffn_ref, g_ple_ref, g_final_ref = vectors
    (w_in_ref, w_group_ref, w_pool_up_ref, w_conv_out_ref, w_o_ref, w_ffn_in_ref,
     w_ffn_out_ref, w_ple_ref, w_ple_gate_ref) = w_vmem
    bf16 = jnp.bfloat16

    hn = _rmsnorm(x, g_mix_ref[...]).astype(bf16)

    def proj(off, width):
        return _dot(hn, w_in_ref[:, off:off + width])

    u = proj(OFF_U, POOL_WIDTH)
    yield
    v = proj(OFF_C, CONV_WIDTH) * proj(OFF_H, CONV_WIDTH)
    yield
    pooled, yconv = pool_conv(u, v)
    conv_out = (proj(OFF_B, CONV_WIDTH) * yconv).astype(bf16)
    yield
    gate_pool = jax.nn.sigmoid(proj(OFF_GP, D_MODEL))
    yield
    gate_conv = jax.nn.sigmoid(proj(OFF_GC, D_MODEL))
    yield
    mixed = [_dot(d, w_group_ref[gi * POOL_GROUP:(gi + 1) * POOL_GROUP, :])
             for gi, d in enumerate(pooled)]
    pool_out = (jnp.concatenate(mixed, axis=-1) * pool_scale_ref[...]).astype(bf16)
    yield
    gated_pool = gate_pool * _dot(pool_out, w_pool_up_ref[...])
    yield
    merged = (gated_pool + gate_conv * _dot(conv_out, w_conv_out_ref[...])).astype(bf16)
    yield
    x = x + _dot(merged, w_o_ref[...])
    yield

    hn = _rmsnorm(x, g_ffn_ref[...]).astype(bf16)
    for c0 in range(0, D_FF, FF_CHUNK):
        gate = _dot(hn, w_ffn_in_ref[:, c0:c0 + FF_CHUNK])
        up = _dot(hn, w_ffn_in_ref[:, D_FF + c0:D_FF + c0 + FF_CHUNK])
        act_ref[:, c0:c0 + FF_CHUNK] = (jax.nn.silu(gate) * up).astype(bf16)
        yield
    x = x + _dot(act_ref[...], w_ffn_out_ref[...])
    yield

    hn = _rmsnorm(x, g_ple_ref[...]).astype(bf16)
    ple_gate = jax.nn.sigmoid(_dot(hn, w_ple_gate_ref[...]))
    yield
    x = x + ple_gate * _dot(p.astype(bf16), w_ple_ref[...])
    return _rmsnorm(x, g_final_ref[...])


def _run_interleaved(chains, lag):
    results = [None] * len(chains)
    live = list(range(len(chains)))
    rounds = 0
    while live:
        for i in list(live):
            if rounds < i * lag:
                continue
            try:
                next(chains[i])
            except StopIteration as done:
                results[i] = done.value
                live.remove(i)
        rounds += 1
    return results


def _prompt_tile(t, x_ref, p_ref, vectors, w_vmem, y_ref, pool_state_ref, conv_state_ref,
                 u_carry_ref, v_carry_ref, ext_ref, vext_ref, act_ref, *, tile, tiles_per_seq):
    wc = functools.partial(_conv_tap, vectors[2])
    rows = tile // PROMPT_CHAINS

    @pl.when(t % tiles_per_seq == 0)
    def _():
        u_carry_ref[...] = jnp.zeros((POOL_PAD, POOL_WIDTH), jnp.float32)
        v_carry_ref[...] = jnp.zeros((CONV_PAD, CONV_WIDTH), jnp.float32)

    ext_ref[:POOL_PAD, :] = u_carry_ref[...]
    vext_ref[:CONV_PAD, :] = v_carry_ref[...]

    def pool_conv_at(r0):
        def pool_conv(u, v):
            ext_ref[POOL_PAD + r0:POOL_PAD + r0 + rows, :] = u
            vext_ref[CONV_PAD + r0:CONV_PAD + r0 + rows, :] = v

            def ext_rows(back, c0, c1):
                return ext_ref[POOL_PAD + r0 - back:POOL_PAD + r0 - back + rows, c0:c1]

            def vext_rows(back):
                return vext_ref[CONV_PAD + r0 - back:CONV_PAD + r0 - back + rows, :]

            pos = (t % tiles_per_seq) * tile + r0 + jax.lax.broadcasted_iota(
                jnp.int32, (rows, POOL_GROUP), 0)
            valid = (pos + 1).astype(jnp.float32)
            pooled = []
            for gi, w in enumerate(POOL_WINDOWS):
                c0, c1 = gi * POOL_GROUP, (gi + 1) * POOL_GROUP
                win = ext_rows(0, c0, c1)
                for back in range(1, w):
                    win = win + ext_rows(back, c0, c1)
                mean = win / jnp.minimum(valid, float(w))
                pooled.append((mean - ext_rows(0, c0, c1)).astype(jnp.bfloat16))

            yconv = wc(0) * vext_rows(2) + wc(1) * vext_rows(1) + wc(2) * vext_rows(0)
            return pooled, yconv
        return pool_conv

    starts = range(0, tile, rows)
    ys = _run_interleaved([
        _layer_stages(x_ref[r0:r0 + rows, :], p_ref[r0:r0 + rows, :], vectors, w_vmem,
                      act_ref.at[r0:r0 + rows, :], pool_conv_at(r0)) for r0 in starts],
        PROMPT_LAG)
    for r0, y in zip(starts, ys):
        y_ref[r0:r0 + rows, :] = y

    u_tail = ext_ref[tile:tile + POOL_PAD, :]
    v_tail = vext_ref[tile:tile + CONV_PAD, :]
    u_carry_ref[...] = u_tail
    v_carry_ref[...] = v_tail
    seq = t // tiles_per_seq
    pool_state_ref[:, pl.ds(seq, 1), :] = u_tail[POOL_PAD - POOL_HIST:][:, None, :]
    conv_state_ref[0] = v_tail[CONV_PAD - CONV_HIST:]


def _sample_tile(x_ref, p_ref, pool_hist_ref, conv_hist_ref, vectors, w_vmem,
                 y_ref, pool_new_ref, conv_new_ref, act_ref, *, seqs, steps):
    wc = functools.partial(_conv_tap, vectors[2])
    n = seqs // SAMPLE_CHAINS

    def chain(q0):
        qs = slice(q0, q0 + n)
        by_step = lambda ref: jnp.concatenate([ref[qs, s, :] for s in range(steps)], axis=0)
        slab = lambda a, s: a[s * n:(s + 1) * n]

        def pool_conv(u, v):
            line = ([pool_hist_ref[s, qs, :] for s in range(POOL_HIST)]
                    + [slab(u, s) for s in range(steps)])
            pooled = []
            for gi, w in enumerate(POOL_WINDOWS):
                c0, c1 = gi * POOL_GROUP, (gi + 1) * POOL_GROUP
                per_step = []
                for s in range(steps):
                    now = POOL_HIST + s
                    win = line[now][:, c0:c1]
                    for back in range(1, w):
                        win = win + line[now - back][:, c0:c1]
                    per_step.append(win * (1.0 / w) - line[now][:, c0:c1])
                pooled.append(jnp.concatenate(per_step, axis=0).astype(jnp.bfloat16))
            for s in range(POOL_HIST):
                pool_new_ref[s, qs, :] = line[steps + s]

            vline = ([conv_hist_ref[qs, s, :] for s in range(CONV_HIST)]
                     + [slab(v, s) for s in range(steps)])
            yconv = jnp.concatenate(
                [wc(0) * vline[s] + wc(1) * vline[s + 1] + wc(2) * vline[s + 2]
                 for s in range(steps)], axis=0)
            for s in range(CONV_HIST):
                conv_new_ref[qs, s, :] = vline[steps + s]
            return pooled, yconv

        return _layer_stages(by_step(x_ref), by_step(p_ref), vectors, w_vmem,
                             act_ref.at[q0 * steps:(q0 + n) * steps, :], pool_conv)

    starts = range(0, seqs, n)
    for q0, y in zip(starts, _run_interleaved([chain(q0) for q0 in starts], SAMPLE_LAG)):
        for s in range(steps):
            y_ref[q0:q0 + n, s, :] = y[s * n:(s + 1) * n]


def _layer_kernel(*refs, n_prompt_tiles, tile_prompt, tiles_per_seq, seqs, steps):
    xp_ref, pp_ref, xs_ref, ps_ref, pool_hist_ref, conv_hist_ref = refs[:N_INPUTS]
    refs = refs[N_INPUTS:]
    vectors, refs = refs[:N_VECTORS], refs[N_VECTORS:]
    w_hbm, refs = refs[:N_MATRICES], refs[N_MATRICES:]
    (yp_ref, pool_state_ref, conv_state_ref, ys_ref, pool_new_ref,
     conv_new_ref) = refs[:N_OUTPUTS]
    refs = refs[N_OUTPUTS:]
    u_carry_ref, v_carry_ref, x_buf, p_buf, y_buf, in_sem, out_sem = refs[:7]
    w_vmem = refs[7:]
    f32, bf16 = jnp.float32, jnp.bfloat16
    in_copies = functools.partial(_prompt_in_copies, x_hbm=xp_ref, p_hbm=pp_ref, x_buf=x_buf,
                                  p_buf=p_buf, sem=in_sem, tile=tile_prompt)
    out_copy = functools.partial(_prompt_out_copy, y_buf=y_buf, y_hbm=yp_ref, sem=out_sem,
                                 tile=tile_prompt)

    step = pl.program_id(0)

    @pl.when(step == 0)
    def _():
        for c in in_copies(0, 0):
            c.start()
        pl.run_scoped(
            functools.partial(_prep_weights, w_hbm, w_vmem),
            pltpu.VMEM((PREP_BUFS, PREP_ROWS, PREP_COLS), f32),
            pltpu.SemaphoreType.DMA((PREP_BUFS,)))

    @pl.when(jnp.logical_and(step >= 1, step <= n_prompt_tiles))
    def _():
        t = step - 1
        slot = t % 2
        for c in in_copies(t, slot):
            c.wait()

        @pl.when(t + 1 < n_prompt_tiles)
        def _():
            for c in in_copies(t + 1, 1 - slot):
                c.start()

        @pl.when(t >= 2)
        def _():
            out_copy(t - 2, slot).wait()

        pl.run_scoped(
            functools.partial(_prompt_tile, t, x_buf.at[slot], p_buf.at[slot], vectors, w_vmem,
                              y_buf.at[slot], pool_state_ref, conv_state_ref, u_carry_ref,
                              v_carry_ref, tile=tile_prompt, tiles_per_seq=tiles_per_seq),
            pltpu.VMEM((POOL_PAD + tile_prompt, POOL_WIDTH), f32),
            pltpu.VMEM((CONV_PAD + tile_prompt, CONV_WIDTH), f32),
            pltpu.VMEM((tile_prompt, D_FF), bf16))
        out_copy(t, slot).start()

        @pl.when(t == n_prompt_tiles - 1)
        def _():
            out_copy(t - 1, 1 - slot).wait()
            out_copy(t, slot).wait()

    @pl.when(step > n_prompt_tiles)
    def _():
        pl.run_scoped(
            functools.partial(_sample_tile, xs_ref, ps_ref, pool_hist_ref, conv_hist_ref, vectors,
                              w_vmem, ys_ref, pool_new_ref, conv_new_ref, seqs=seqs, steps=steps),
            pltpu.VMEM((seqs * steps, D_FF), bf16))


def _resident(shape):
    zeros = (0,) * len(shape)
    return pl.BlockSpec(shape, lambda i: zeros, pipeline_mode=pl.Buffered(1))


def _run_layer(x_p, p_p, x_s, p_s, pool_hist, conv_hist, vectors, matrices, *, seq_len, steps):
    tokens_p, tokens_s = x_p.shape[0], x_s.shape[0] * steps
    assert x_s.shape[1] == steps
    tile_p, tile_s = TILE_PROMPT, TILE_SAMPLE
    assert tokens_p % tile_p == 0 and seq_len % tile_p == 0 and tokens_p >= 2 * tile_p
    assert tile_p % (PROMPT_CHAINS * SUBLANES) == 0 and tile_p // PROMPT_CHAINS >= POOL_PAD
    assert tokens_s % tile_s == 0 and tile_s % (SAMPLE_CHAINS * steps * SUBLANES) == 0
    assert len(vectors) == N_VECTORS and len(matrices) == N_MATRICES
    n_p, n_s = tokens_p // tile_p, tokens_s // tile_s
    tiles_per_seq = seq_len // tile_p
    n_seq = tokens_p // seq_len
    seqs = tile_s // steps

    tile_p_of = lambda i: jnp.clip(i - 1, 0, n_p - 1)
    tile_s_of = lambda i: jnp.clip(i - 1 - n_p, 0, n_s - 1)
    pool_state_p = pl.BlockSpec((POOL_HIST, n_seq, POOL_WIDTH), lambda i: (0, 0, 0))
    conv_state_p = pl.BlockSpec((1, CONV_HIST, CONV_WIDTH),
                                lambda i: (tile_p_of(i) // tiles_per_seq, 0, 0))
    row_s_in = lambda width: pl.BlockSpec((seqs, steps, width), lambda i: (tile_s_of(i), 0, 0),
                                          pipeline_mode=pl.Buffered(1))
    pool_s = pl.BlockSpec((POOL_HIST, seqs, POOL_WIDTH), lambda i: (0, tile_s_of(i), 0),
                          pipeline_mode=pl.Buffered(1))
    conv_s = pl.BlockSpec((seqs, CONV_HIST, CONV_WIDTH), lambda i: (tile_s_of(i), 0, 0),
                          pipeline_mode=pl.Buffered(1))

    f32 = jnp.float32
    out_shape = [
        jax.ShapeDtypeStruct((tokens_p, D_MODEL), f32),
        jax.ShapeDtypeStruct((POOL_HIST, n_seq, POOL_WIDTH), f32),
        jax.ShapeDtypeStruct((n_seq, CONV_HIST, CONV_WIDTH), f32),
        jax.ShapeDtypeStruct((tokens_s // steps, steps, D_MODEL), f32),
        jax.ShapeDtypeStruct((POOL_HIST, tokens_s // steps, POOL_WIDTH), f32),
        jax.ShapeDtypeStruct((tokens_s // steps, CONV_HIST, CONV_WIDTH), f32),
    ]
    out_specs = [pl.BlockSpec(memory_space=pl.ANY), pool_state_p, conv_state_p,
                 pl.BlockSpec((seqs, steps, D_MODEL), lambda i: (tile_s_of(i), 0, 0),
                              pipeline_mode=pl.Buffered(1)),
                 pool_s, conv_s]
    scratch = [pltpu.VMEM((POOL_PAD, POOL_WIDTH), f32), pltpu.VMEM((CONV_PAD, CONV_WIDTH), f32),
               pltpu.VMEM((2, tile_p, D_MODEL), f32), pltpu.VMEM((2, tile_p, D_PLE), f32),
               pltpu.VMEM((2, tile_p, D_MODEL), f32),
               pltpu.SemaphoreType.DMA((2, 2)), pltpu.SemaphoreType.DMA((2,))]
    scratch += [pltpu.VMEM(w.shape, jnp.bfloat16) for w in matrices]

    kernel = functools.partial(_layer_kernel, n_prompt_tiles=n_p, tile_prompt=tile_p,
                               tiles_per_seq=tiles_per_seq, seqs=seqs, steps=steps)
    return pl.pallas_call(
        kernel,
        grid=(1 + n_p + n_s,),
        in_specs=([pl.BlockSpec(memory_space=pl.ANY), pl.BlockSpec(memory_space=pl.ANY),
                   row_s_in(D_MODEL), row_s_in(D_PLE),
                   pool_s, conv_s]
                  + [_resident(v.shape) for v in vectors]
                  + [pl.BlockSpec(memory_space=pl.ANY)] * N_MATRICES),
        out_specs=out_specs,
        out_shape=out_shape,
        scratch_shapes=scratch,
        compiler_params=pltpu.CompilerParams(
            dimension_semantics=("arbitrary",),
            vmem_limit_bytes=VMEM_LIMIT_BYTES),
        name="hybrid_layer",
    )(x_p, p_p, x_s, p_s, pool_hist, conv_hist, *vectors, *matrices)


def kernel(x_prompt, x_sample, state_pool, state_conv, p_prompt, p_sample, g_mix, w_in, w_pool_group, pool_scale, w_pool_up, w_conv, w_conv_out, w_o, g_ffn, w_ffn_in, w_ffn_out, g_ple, w_ple, w_ple_gate, g_final):
    depth = g_mix.shape[0]
    assert depth == 1, "single-layer step"
    vec = lambda a: a.reshape(1, -1)
    vectors = (vec(g_mix[0]), vec(pool_scale[0]), vec(w_conv[0]), vec(g_ffn[0]), vec(g_ple[0]),
               vec(g_final))
    matrices = (w_in[0], w_pool_group[0].reshape(POOL_WIDTH, POOL_GROUP), w_pool_up[0],
                w_conv_out[0], w_o[0], w_ffn_in[0], w_ffn_out[0], w_ple[0], w_ple_gate[0])

    batch, seq, _ = x_prompt.shape
    dec_seq = x_sample.shape[1]

    y_p, pool_state_p, conv_state_p, y_s, pool_state_s, conv_state_s = _run_layer(
        x_prompt.reshape(batch * seq, D_MODEL), p_prompt[0].reshape(batch * seq, D_PLE),
        x_sample, p_sample[0],
        jnp.swapaxes(state_pool[0], 0, 1), state_conv[0],
        vectors, matrices, seq_len=seq, steps=dec_seq)

    return (y_p.reshape(batch, seq, D_MODEL), y_s,
            jnp.swapaxes(pool_state_p, 0, 1)[None], conv_state_p[None],
            jnp.swapaxes(pool_state_s, 0, 1)[None], conv_state_s[None])
```

```python
import functools

import jax
import jax.numpy as jnp
from jax.experimental import pallas as pl
from jax.experimental.pallas import tpu as pltpu

D_MODEL = 1024
POOL_WINDOWS = (2, 4, 8, 16)
POOL_GROUP = 128
POOL_WIDTH = 512
POOL_HIST = 15
CONV_WIDTH = 512
CONV_K = 3
CONV_HIST = 2
D_FF = 2816
D_PLE = 256
EPS = 1e-6

OFF_U = 0
OFF_B = OFF_U + POOL_WIDTH
OFF_C = OFF_B + CONV_WIDTH
OFF_H = OFF_C + CONV_WIDTH
OFF_GP = OFF_H + CONV_WIDTH
OFF_GC = OFF_GP + D_MODEL

V7X_VMEM_BYTES = 64 * 1024 * 1024
V7X_MXU_WIDTH = 256
SUBLANES = 8
DMA_PRIORITIES = 2

POOL_PAD = 16
CONV_PAD = 8
FF_CHUNK = V7X_MXU_WIDTH
TILE_PROMPT = 512
TILE_SAMPLE = 256
PROMPT_CHAINS = 2
SAMPLE_CHAINS = 2
PROMPT_LAG = 1
SAMPLE_LAG = 0
PREP_ROWS = 256
PREP_COLS = 1024
PREP_BUFS = 5
VMEM_RESERVE_BYTES = 4 * 1024 * 1024
VMEM_LIMIT_BYTES = V7X_VMEM_BYTES - VMEM_RESERVE_BYTES

N_VECTORS = 6
N_MATRICES = 9
N_INPUTS = 6
N_OUTPUTS = 6


def _rmsnorm(x, g):
    ms = jnp.mean(x * x, axis=-1, keepdims=True)
    return x * jax.lax.rsqrt(ms + EPS) * g


def _dot(a, b):
    return jnp.dot(a, b, preferred_element_type=jnp.float32)


def _conv_tap(w_conv_ref, k):
    return w_conv_ref[:, k * CONV_WIDTH:(k + 1) * CONV_WIDTH]


def _prep_weights(w_hbm, w_vmem, stage_ref, sem):
    chunks = []
    for src, dst in zip(w_hbm, w_vmem):
        n_rows, n_cols = dst.shape
        for r0 in range(0, n_rows, PREP_ROWS):
            for c0 in range(0, n_cols, PREP_COLS):
                chunks.append((src, dst, r0, min(PREP_ROWS, n_rows - r0),
                               c0, min(PREP_COLS, n_cols - c0)))

    def chunk_copy(k):
        src, _, r0, nr, c0, nc = chunks[k]
        slot = k % PREP_BUFS
        return pltpu.make_async_copy(src.at[r0:r0 + nr, c0:c0 + nc],
                                     stage_ref.at[slot, :nr, :nc], sem.at[slot])

    for k in range(min(PREP_BUFS, len(chunks))):
        chunk_copy(k).start(priority=k % DMA_PRIORITIES)
    for k, (_, dst, r0, nr, c0, nc) in enumerate(chunks):
        chunk_copy(k).wait()
        dst[r0:r0 + nr, c0:c0 + nc] = stage_ref[k % PREP_BUFS, :nr, :nc].astype(jnp.bfloat16)
        if k + PREP_BUFS < len(chunks):
            chunk_copy(k + PREP_BUFS).start(priority=(k + PREP_BUFS) % DMA_PRIORITIES)


def _prompt_in_copies(t, slot, x_hbm, p_hbm, x_buf, p_buf, sem, tile):
    rows = pl.ds(t * tile, tile)
    return (pltpu.make_async_copy(x_hbm.at[rows, :], x_buf.at[slot], sem.at[slot, 0]),
            pltpu.make_async_copy(p_hbm.at[rows, :], p_buf.at[slot], sem.at[slot, 1]))


def _prompt_out_copy(t, slot, y_buf, y_hbm, sem, tile):
    return pltpu.make_async_copy(y_buf.at[slot], y_hbm.at[pl.ds(t * tile, tile), :], sem.at[slot])


def _layer_stages(x, p, vectors, w_vmem, act_ref, pool_conv, finish=None):
    g_mix_ref, pool_scale_ref, _, g_ffn_ref, g_ple_ref, g_final_ref = vectors
    (w_in_ref, w_group_ref, w_pool_up_ref, w_conv_out_ref, w_o_ref, w_ffn_in_ref,
     w_ffn_out_ref, w_ple_ref, w_ple_gate_ref) = w_vmem
    bf16 = jnp.bfloat16

    hn = _rmsnorm(x, g_mix_ref[...]).astype(bf16)

    def proj(off, width):
        return _dot(hn, w_in_ref[:, off:off + width])

    u = proj(OFF_U, POOL_WIDTH)
    yield
    v = proj(OFF_C, CONV_WIDTH) * proj(OFF_H, CONV_WIDTH)
    yield
    pooled, yconv = pool_conv(u, v)
    conv_out = (proj(OFF_B, CONV_WIDTH) * yconv).astype(bf16)
    yield
    gate_pool = jax.nn.sigmoid(proj(OFF_GP, D_MODEL))
    yield
    gate_conv = jax.nn.sigmoid(proj(OFF_GC, D_MODEL))
    yield
    mixed = [_dot(d, w_group_ref[gi * POOL_GROUP:(gi + 1) * POOL_GROUP, :])
             for gi, d in enumerate(pooled)]
    pool_out = (jnp.concatenate(mixed, axis=-1) * pool_scale_ref[...]).astype(bf16)
    yield
    gated_pool = gate_pool * _dot(pool_out, w_pool_up_ref[...])
    yield
    merged = (gated_pool + gate_conv * _dot(conv_out, w_conv_out_ref[...])).astype(bf16)
    yield
    x = x + _dot(merged, w_o_ref[...])
    yield

    hn = _rmsnorm(x, g_ffn_ref[...]).astype(bf16)
    for c0 in range(0, D_FF, FF_CHUNK):
        gate = _dot(hn, w_ffn_in_ref[:, c0:c0 + FF_CHUNK])
        up = _dot(hn, w_ffn_in_ref[:, D_FF + c0:D_FF + c0 + FF_CHUNK])
        act_ref[:, c0:c0 + FF_CHUNK] = (jax.nn.silu(gate) * up).astype(bf16)
        yield
    x = x + _dot(act_ref[...], w_ffn_out_ref[...])
    yield

    hn = _rmsnorm(x, g_ple_ref[...]).astype(bf16)
    ple_gate = jax.nn.sigmoid(_dot(hn, w_ple_gate_ref[...]))
    yield
    x = x + ple_gate * _dot(p.astype(bf16), w_ple_ref[...])
    return _rmsnorm(x, g_final_ref[...]) if finish is None else finish(x)


def _run_interleaved(chains, lag):
    results = [None] * len(chains)
    live = list(range(len(chains)))
    rounds = 0
    while live:
        for i in list(live):
            if rounds < i * lag:
                continue
            try:
                next(chains[i])
            except StopIteration as done:
                results[i] = done.value
                live.remove(i)
        rounds += 1
    return results


def _close_last_chain(y_ref, x3_ref, vectors):
    rows = x3_ref.shape[0]
    y_ref[y_ref.shape[0] - rows:, :] = _rmsnorm(x3_ref[...], vectors[5][...])


def _prompt_tile(t, x_ref, p_ref, vectors, w_vmem, y_ref, y_prev_ref, x3_ref, pool_state_ref,
                 conv_state_ref, u_carry_ref, v_carry_ref, ext_ref, vext_ref, act_ref,
                 *, tile, tiles_per_seq):
    wc = functools.partial(_conv_tap, vectors[2])
    rows = tile // PROMPT_CHAINS
    _close_last_chain(y_prev_ref, x3_ref, vectors)

    def keep_residual(x3):
        x3_ref[...] = x3

    @pl.when(t % tiles_per_seq == 0)
    def _():
        u_carry_ref[...] = jnp.zeros((POOL_PAD, POOL_WIDTH), jnp.float32)
        v_carry_ref[...] = jnp.zeros((CONV_PAD, CONV_WIDTH), jnp.float32)

    ext_ref[:POOL_PAD, :] = u_carry_ref[...]
    vext_ref[:CONV_PAD, :] = v_carry_ref[...]

    def pool_conv_at(r0):
        def pool_conv(u, v):
            ext_ref[POOL_PAD + r0:POOL_PAD + r0 + rows, :] = u
            vext_ref[CONV_PAD + r0:CONV_PAD + r0 + rows, :] = v

            def ext_rows(back, c0, c1):
                return ext_ref[POOL_PAD + r0 - back:POOL_PAD + r0 - back + rows, c0:c1]

            def vext_rows(back):
                return vext_ref[CONV_PAD + r0 - back:CONV_PAD + r0 - back + rows, :]

            pos = (t % tiles_per_seq) * tile + r0 + jax.lax.broadcasted_iota(
                jnp.int32, (rows, POOL_GROUP), 0)
            valid = (pos + 1).astype(jnp.float32)
            pooled = []
            for gi, w in enumerate(POOL_WINDOWS):
                c0, c1 = gi * POOL_GROUP, (gi + 1) * POOL_GROUP
                win = ext_rows(0, c0, c1)
                for back in range(1, w):
                    win = win + ext_rows(back, c0, c1)
                mean = win / jnp.minimum(valid, float(w))
                pooled.append((mean - ext_rows(0, c0, c1)).astype(jnp.bfloat16))

            yconv = wc(0) * vext_rows(2) + wc(1) * vext_rows(1) + wc(2) * vext_rows(0)
            return pooled, yconv
        return pool_conv

    starts = range(0, tile, rows)
    ys = _run_interleaved([
        _layer_stages(x_ref[r0:r0 + rows, :], p_ref[r0:r0 + rows, :], vectors, w_vmem,
                      act_ref.at[r0:r0 + rows, :], pool_conv_at(r0),
                      keep_residual if r0 == starts[-1] else None) for r0 in starts],
        PROMPT_LAG)
    for r0, y in zip(starts[:-1], ys):
        y_ref[r0:r0 + rows, :] = y

    u_tail = ext_ref[tile:tile + POOL_PAD, :]
    v_tail = vext_ref[tile:tile + CONV_PAD, :]
    u_carry_ref[...] = u_tail
    v_carry_ref[...] = v_tail
    seq = t // tiles_per_seq
    pool_state_ref[:, pl.ds(seq, 1), :] = u_tail[POOL_PAD - POOL_HIST:][:, None, :]
    conv_state_ref[0] = v_tail[CONV_PAD - CONV_HIST:]


def _sample_tile(x_ref, p_ref, pool_hist_ref, conv_hist_ref, vectors, w_vmem,
                 y_ref, pool_new_ref, conv_new_ref, act_ref, *, seqs, steps):
    wc = functools.partial(_conv_tap, vectors[2])
    n = seqs // SAMPLE_CHAINS

    def chain(q0):
        qs = slice(q0, q0 + n)
        by_step = lambda ref: jnp.concatenate([ref[qs, s, :] for s in range(steps)], axis=0)
        slab = lambda a, s: a[s * n:(s + 1) * n]

        def pool_conv(u, v):
            line = ([pool_hist_ref[s, qs, :] for s in range(POOL_HIST)]
                    + [slab(u, s) for s in range(steps)])
            pooled = []
            for gi, w in enumerate(POOL_WINDOWS):
                c0, c1 = gi * POOL_GROUP, (gi + 1) * POOL_GROUP
                per_step = []
                for s in range(steps):
                    now = POOL_HIST + s
                    win = line[now][:, c0:c1]
                    for back in range(1, w):
                        win = win + line[now - back][:, c0:c1]
                    per_step.append(win * (1.0 / w) - line[now][:, c0:c1])
                pooled.append(jnp.concatenate(per_step, axis=0).astype(jnp.bfloat16))
            for s in range(POOL_HIST):
                pool_new_ref[s, qs, :] = line[steps + s]

            vline = ([conv_hist_ref[qs, s, :] for s in range(CONV_HIST)]
                     + [slab(v, s) for s in range(steps)])
            yconv = jnp.concatenate(
                [wc(0) * vline[s] + wc(1) * vline[s + 1] + wc(2) * vline[s + 2]
                 for s in range(steps)], axis=0)
            for s in range(CONV_HIST):
                conv_new_ref[qs, s, :] = vline[steps + s]
            return pooled, yconv

        return _layer_stages(by_step(x_ref), by_step(p_ref), vectors, w_vmem,
                             act_ref.at[q0 * steps:(q0 + n) * steps, :], pool_conv)

    starts = range(0, seqs, n)
    for q0, y in zip(starts, _run_interleaved([chain(q0) for q0 in starts], SAMPLE_LAG)):
        for s in range(steps):
            y_ref[q0:q0 + n, s, :] = y[s * n:(s + 1) * n]


def _layer_kernel(*refs, n_prompt_tiles, tile_prompt, tiles_per_seq, seqs, steps):
    xp_ref, pp_ref, xs_ref, ps_ref, pool_hist_ref, conv_hist_ref = refs[:N_INPUTS]
    refs = refs[N_INPUTS:]
    vectors, refs = refs[:N_VECTORS], refs[N_VECTORS:]
    w_hbm, refs = refs[:N_MATRICES], refs[N_MATRICES:]
    (yp_ref, pool_state_ref, conv_state_ref, ys_ref, pool_new_ref,
     conv_new_ref) = refs[:N_OUTPUTS]
    refs = refs[N_OUTPUTS:]
    u_carry_ref, v_carry_ref, x_buf, p_buf, y_buf, x3_buf, in_sem, out_sem = refs[:8]
    w_vmem = refs[8:]
    f32, bf16 = jnp.float32, jnp.bfloat16
    in_copies = functools.partial(_prompt_in_copies, x_hbm=xp_ref, p_hbm=pp_ref, x_buf=x_buf,
                                  p_buf=p_buf, sem=in_sem, tile=tile_prompt)
    out_copy = functools.partial(_prompt_out_copy, y_buf=y_buf, y_hbm=yp_ref, sem=out_sem,
                                 tile=tile_prompt)

    step = pl.program_id(0)

    @pl.when(step == 0)
    def _():
        for c in in_copies(0, 0):
            c.start()
        x3_buf[...] = jnp.zeros(x3_buf.shape, f32)
        pl.run_scoped(
            functools.partial(_prep_weights, w_hbm, w_vmem),
            pltpu.VMEM((PREP_BUFS, PREP_ROWS, PREP_COLS), f32),
            pltpu.SemaphoreType.DMA((PREP_BUFS,)))

    @pl.when(jnp.logical_and(step >= 1, step <= n_prompt_tiles))
    def _():
        t = step - 1
        slot, y_slot, y_prev = t % 2, t % 3, (t + 2) % 3
        for c in in_copies(t, slot):
            c.wait()

        @pl.when(t + 1 < n_prompt_tiles)
        def _():
            for c in in_copies(t + 1, 1 - slot):
                c.start()

        @pl.when(t >= 3)
        def _():
            out_copy(t - 3, y_slot).wait()

        pl.run_scoped(
            functools.partial(_prompt_tile, t, x_buf.at[slot], p_buf.at[slot], vectors, w_vmem,
                              y_buf.at[y_slot], y_buf.at[y_prev], x3_buf, pool_state_ref,
                              conv_state_ref, u_carry_ref, v_carry_ref,
                              tile=tile_prompt, tiles_per_seq=tiles_per_seq),
            pltpu.VMEM((POOL_PAD + tile_prompt, POOL_WIDTH), f32),
            pltpu.VMEM((CONV_PAD + tile_prompt, CONV_WIDTH), f32),
            pltpu.VMEM((tile_prompt, D_FF), bf16))

        @pl.when(t >= 1)
        def _():
            out_copy(t - 1, y_prev).start()

        @pl.when(t == n_prompt_tiles - 1)
        def _():
            _close_last_chain(y_buf.at[y_slot], x3_buf, vectors)
            out_copy(t, y_slot).start()
            out_copy(t - 2, (t + 1) % 3).wait()
            out_copy(t - 1, y_prev).wait()
            out_copy(t, y_slot).wait()

    @pl.when(step > n_prompt_tiles)
    def _():
        pl.run_scoped(
            functools.partial(_sample_tile, xs_ref, ps_ref, pool_hist_ref, conv_hist_ref, vectors,
                              w_vmem, ys_ref, pool_new_ref, conv_new_ref, seqs=seqs, steps=steps),
            pltpu.VMEM((seqs * steps, D_FF), bf16))


def _resident(shape):
    zeros = (0,) * len(shape)
    return pl.BlockSpec(shape, lambda i: zeros, pipeline_mode=pl.Buffered(1))


def _run_layer(x_p, p_p, x_s, p_s, pool_hist, conv_hist, vectors, matrices, *, seq_len, steps):
    tokens_p, tokens_s = x_p.shape[0], x_s.shape[0] * steps
    assert x_s.shape[1] == steps
    tile_p, tile_s = TILE_PROMPT, TILE_SAMPLE
    assert tokens_p % tile_p == 0 and seq_len % tile_p == 0 and tokens_p >= 3 * tile_p
    assert tile_p % (PROMPT_CHAINS * SUBLANES) == 0 and tile_p // PROMPT_CHAINS >= POOL_PAD
    assert tokens_s % tile_s == 0 and tile_s % (SAMPLE_CHAINS * steps * SUBLANES) == 0
    assert len(vectors) == N_VECTORS and len(matrices) == N_MATRICES
    n_p, n_s = tokens_p // tile_p, tokens_s // tile_s
    tiles_per_seq = seq_len // tile_p
    n_seq = tokens_p // seq_len
    seqs = tile_s // steps

    tile_p_of = lambda i: jnp.clip(i - 1, 0, n_p - 1)
    tile_s_of = lambda i: jnp.clip(i - 1 - n_p, 0, n_s - 1)
    pool_state_p = pl.BlockSpec((POOL_HIST, n_seq, POOL_WIDTH), lambda i: (0, 0, 0))
    conv_state_p = pl.BlockSpec((1, CONV_HIST, CONV_WIDTH),
                                lambda i: (tile_p_of(i) // tiles_per_seq, 0, 0))
    row_s_in = lambda width: pl.BlockSpec((seqs, steps, width), lambda i: (tile_s_of(i), 0, 0),
                                          pipeline_mode=pl.Buffered(1))
    pool_s = pl.BlockSpec((POOL_HIST, seqs, POOL_WIDTH), lambda i: (0, tile_s_of(i), 0),
                          pipeline_mode=pl.Buffered(1))
    conv_s = pl.BlockSpec((seqs, CONV_HIST, CONV_WIDTH), lambda i: (tile_s_of(i), 0, 0),
                          pipeline_mode=pl.Buffered(1))

    f32 = jnp.float32
    out_shape = [
        jax.ShapeDtypeStruct((tokens_p, D_MODEL), f32),
        jax.ShapeDtypeStruct((POOL_HIST, n_seq, POOL_WIDTH), f32),
        jax.ShapeDtypeStruct((n_seq, CONV_HIST, CONV_WIDTH), f32),
        jax.ShapeDtypeStruct((tokens_s // steps, steps, D_MODEL), f32),
        jax.ShapeDtypeStruct((POOL_HIST, tokens_s // steps, POOL_WIDTH), f32),
        jax.ShapeDtypeStruct((tokens_s // steps, CONV_HIST, CONV_WIDTH), f32),
    ]
    out_specs = [pl.BlockSpec(memory_space=pl.ANY), pool_state_p, conv_state_p,
                 pl.BlockSpec((seqs, steps, D_MODEL), lambda i: (tile_s_of(i), 0, 0),
                              pipeline_mode=pl.Buffered(1)),
                 pool_s, conv_s]
    scratch = [pltpu.VMEM((POOL_PAD, POOL_WIDTH), f32), pltpu.VMEM((CONV_PAD, CONV_WIDTH), f32),
               pltpu.VMEM((2, tile_p, D_MODEL), f32), pltpu.VMEM((2, tile_p, D_PLE), f32),
               pltpu.VMEM((3, tile_p, D_MODEL), f32),
               pltpu.VMEM((tile_p // PROMPT_CHAINS, D_MODEL), f32),
               pltpu.SemaphoreType.DMA((2, 2)), pltpu.SemaphoreType.DMA((3,))]
    scratch += [pltpu.VMEM(w.shape, jnp.bfloat16) for w in matrices]

    kernel = functools.partial(_layer_kernel, n_prompt_tiles=n_p, tile_prompt=tile_p,
                               tiles_per_seq=tiles_per_seq, seqs=seqs, steps=steps)
    return pl.pallas_call(
        kernel,
        grid=(1 + n_p + n_s,),
        in_specs=([pl.BlockSpec(memory_space=pl.ANY), pl.BlockSpec(memory_space=pl.ANY),
                   row_s_in(D_MODEL), row_s_in(D_PLE),
                   pool_s, conv_s]
                  + [_resident(v.shape) for v in vectors]
                  + [pl.BlockSpec(memory_space=pl.ANY)] * N_MATRICES),
        out_specs=out_specs,
        out_shape=out_shape,
        scratch_shapes=scratch,
        compiler_params=pltpu.CompilerParams(
            dimension_semantics=("arbitrary",),
            vmem_limit_bytes=VMEM_LIMIT_BYTES),
        name="hybrid_layer",
    )(x_p, p_p, x_s, p_s, pool_hist, conv_hist, *vectors, *matrices)


def kernel(x_prompt, x_sample, state_pool, state_conv, p_prompt, p_sample, g_mix, w_in, w_pool_group, pool_scale, w_pool_up, w_conv, w_conv_out, w_o, g_ffn, w_ffn_in, w_ffn_out, g_ple, w_ple, w_ple_gate, g_final):
    depth = g_mix.shape[0]
    assert depth == 1, "single-layer step"
    vec = lambda a: a.reshape(1, -1)
    vectors = (vec(g_mix[0]), vec(pool_scale[0]), vec(w_conv[0]), vec(g_ffn[0]), vec(g_ple[0]),
               vec(g_final))
    matrices = (w_in[0], w_pool_group[0].reshape(POOL_WIDTH, POOL_GROUP), w_pool_up[0],
                w_conv_out[0], w_o[0], w_ffn_in[0], w_ffn_out[0], w_ple[0], w_ple_gate[0])

    batch, seq, _ = x_prompt.shape
    dec_seq = x_sample.shape[1]

    y_p, pool_state_p, conv_state_p, y_s, pool_state_s, conv_state_s = _run_layer(
        x_prompt.reshape(batch * seq, D_MODEL), p_prompt[0].reshape(batch * seq, D_PLE),
        x_sample, p_sample[0],
        jnp.swapaxes(state_pool[0], 0, 1), state_conv[0],
        vectors, matrices, seq_len=seq, steps=dec_seq)

    return (y_p.reshape(batch, seq, D_MODEL), y_s,
            jnp.swapaxes(pool_state_p, 0, 1)[None], conv_state_p[None],
            jnp.swapaxes(pool_state_s, 0, 1)[None], conv_state_s[None])
```

```python
import functools

import jax
import jax.numpy as jnp
from jax.experimental import pallas as pl
from jax.experimental.pallas import tpu as pltpu

D_MODEL = 1024
POOL_WINDOWS = (2, 4, 8, 16)
POOL_GROUP = 128
POOL_WIDTH = 512
POOL_HIST = 15
CONV_WIDTH = 512
CONV_K = 3
CONV_HIST = 2
D_FF = 2816
D_PLE = 256
EPS = 1e-6

OFF_U = 0
OFF_B = OFF_U + POOL_WIDTH
OFF_C = OFF_B + CONV_WIDTH
OFF_H = OFF_C + CONV_WIDTH
OFF_GP = OFF_H + CONV_WIDTH
OFF_GC = OFF_GP + D_MODEL

V7X_VMEM_BYTES = 64 * 1024 * 1024
V7X_MXU_WIDTH = 256
SUBLANES = 8
DMA_PRIORITIES = 2

POOL_PAD = 16
CONV_PAD = 8
FF_CHUNK = V7X_MXU_WIDTH
TILE_PROMPT = 512
TILE_SAMPLE = 256
PROMPT_CHAINS = 2
SAMPLE_CHAINS = 2
PROMPT_LAG = 1
SAMPLE_LAG = 0
PREP_ROWS = 256
PREP_COLS = 1024
PREP_BUFS = 8
VMEM_RESERVE_BYTES = 4 * 1024 * 1024
VMEM_LIMIT_BYTES = V7X_VMEM_BYTES - VMEM_RESERVE_BYTES

N_VECTORS = 6
N_MATRICES = 9
N_INPUTS = 6
N_OUTPUTS = 6


def _rmsnorm(x, g):
    ms = jnp.mean(x * x, axis=-1, keepdims=True)
    return x * jax.lax.rsqrt(ms + EPS) * g


def _dot(a, b):
    return jnp.dot(a, b, preferred_element_type=jnp.float32)


def _conv_tap(w_conv_ref, k):
    return w_conv_ref[:, k * CONV_WIDTH:(k + 1) * CONV_WIDTH]


def _prep_weights(w_hbm, w_vmem, stage_ref, sem):
    chunks = []
    for src, dst in zip(w_hbm, w_vmem):
        n_rows, n_cols = dst.shape
        for r0 in range(0, n_rows, PREP_ROWS):
            for c0 in range(0, n_cols, PREP_COLS):
                chunks.append((src, dst, r0, min(PREP_ROWS, n_rows - r0),
                               c0, min(PREP_COLS, n_cols - c0)))

    def chunk_copy(k):
        src, _, r0, nr, c0, nc = chunks[k]
        slot = k % PREP_BUFS
        return pltpu.make_async_copy(src.at[r0:r0 + nr, c0:c0 + nc],
                                     stage_ref.at[slot, :nr, :nc], sem.at[slot])

    for k in range(min(PREP_BUFS, len(chunks))):
        chunk_copy(k).start(priority=k % DMA_PRIORITIES)
    for k, (_, dst, r0, nr, c0, nc) in enumerate(chunks):
        chunk_copy(k).wait()
        dst[r0:r0 + nr, c0:c0 + nc] = stage_ref[k % PREP_BUFS, :nr, :nc].astype(jnp.bfloat16)
        if k + PREP_BUFS < len(chunks):
            chunk_copy(k + PREP_BUFS).start(priority=(k + PREP_BUFS) % DMA_PRIORITIES)


def _prompt_in_copies(t, slot, x_hbm, p_hbm, x_buf, p_buf, sem, tile):
    rows = pl.ds(t * tile, tile)
    return (pltpu.make_async_copy(x_hbm.at[rows, :], x_buf.at[slot], sem.at[slot, 0]),
            pltpu.make_async_copy(p_hbm.at[rows, :], p_buf.at[slot], sem.at[slot, 1]))


def _prompt_out_copy(t, slot, y_buf, y_hbm, sem, tile):
    return pltpu.make_async_copy(y_buf.at[slot], y_hbm.at[pl.ds(t * tile, tile), :], sem.at[slot])


def _layer_stages(x, p, vectors, w_vmem, act_ref, pool_conv):
    g_mix_ref, pool_scale_ref, _, g_ffn_ref, g_ple_ref, g_final_ref = vectors
    (w_in_ref, w_group_ref, w_pool_up_ref, w_conv_out_ref, w_o_ref, w_ffn_in_ref,
     w_ffn_out_ref, w_ple_ref, w_ple_gate_ref) = w_vmem
    bf16 = jnp.bfloat16

    hn = _rmsnorm(x, g_mix_ref[...]).astype(bf16)

    def proj(off, width):
        return _dot(hn, w_in_ref[:, off:off + width])

    u = proj(OFF_U, POOL_WIDTH)
    yield
    v = proj(OFF_C, CONV_WIDTH) * proj(OFF_H, CONV_WIDTH)
    yield
    pooled, yconv = pool_conv(u, v)
    conv_out = (proj(OFF_B, CONV_WIDTH) * yconv).astype(bf16)
    yield
    gate_pool = jax.nn.sigmoid(proj(OFF_GP, D_MODEL))
    yield
    gate_conv = jax.nn.sigmoid(proj(OFF_GC, D_MODEL))
    yield
    mixed = [_dot(d, w_group_ref[gi * POOL_GROUP:(gi + 1) * POOL_GROUP, :])
             for gi, d in enumerate(pooled)]
    pool_out = (jnp.concatenate(mixed, axis=-1) * pool_scale_ref[...]).astype(bf16)
    yield
    gated_pool = gate_pool * _dot(pool_out, w_pool_up_ref[...])
    yield
    merged = (gated_pool + gate_conv * _dot(conv_out, w_conv_out_ref[...])).astype(bf16)
    yield
    x = x + _dot(merged, w_o_ref[...])
    yield

    hn = _rmsnorm(x, g_ffn_ref[...]).astype(bf16)
    for c0 in range(0, D_FF, FF_CHUNK):
        gate = _dot(hn, w_ffn_in_ref[:, c0:c0 + FF_CHUNK])
        up = _dot(hn, w_ffn_in_ref[:, D_FF + c0:D_FF + c0 + FF_CHUNK])
        act_ref[:, c0:c0 + FF_CHUNK] = (jax.nn.silu(gate) * up).astype(bf16)
        yield
    x = x + _dot(act_ref[...], w_ffn_out_ref[...])
    yield

    hn = _rmsnorm(x, g_ple_ref[...]).astype(bf16)
    ple_gate = jax.nn.sigmoid(_dot(hn, w_ple_gate_ref[...]))
    yield
    x = x + ple_gate * _dot(p.astype(bf16), w_ple_ref[...])
    return _rmsnorm(x, g_final_ref[...])


def _run_interleaved(chains, lag):
    results = [None] * len(chains)
    live = list(range(len(chains)))
    rounds = 0
    while live:
        for i in list(live):
            if rounds < i * lag:
                continue
            try:
                next(chains[i])
            except StopIteration as done:
                results[i] = done.value
                live.remove(i)
        rounds += 1
    return results


def _prompt_tile(t, x_ref, p_ref, vectors, w_vmem, y_ref, pool_state_ref, conv_state_ref,
                 u_carry_ref, v_carry_ref, ext_ref, vext_ref, act_ref, *, tile, tiles_per_seq):
    wc = functools.partial(_conv_tap, vectors[2])
    rows = tile // PROMPT_CHAINS

    @pl.when(t % tiles_per_seq == 0)
    def _():
        u_carry_ref[...] = jnp.zeros((POOL_PAD, POOL_WIDTH), jnp.float32)
        v_carry_ref[...] = jnp.zeros((CONV_PAD, CONV_WIDTH), jnp.float32)

    ext_ref[:POOL_PAD, :] = u_carry_ref[...]
    vext_ref[:CONV_PAD, :] = v_carry_ref[...]

    def pool_conv_at(r0):
        def pool_conv(u, v):
            ext_ref[POOL_PAD + r0:POOL_PAD + r0 + rows, :] = u
            vext_ref[CONV_PAD + r0:CONV_PAD + r0 + rows, :] = v

            def ext_rows(back, c0, c1):
                return ext_ref[POOL_PAD + r0 - back:POOL_PAD + r0 - back + rows, c0:c1]

            def vext_rows(back):
                return vext_ref[CONV_PAD + r0 - back:CONV_PAD + r0 - back + rows, :]

            pos = (t % tiles_per_seq) * tile + r0 + jax.lax.broadcasted_iota(
                jnp.int32, (rows, POOL_GROUP), 0)
            valid = (pos + 1).astype(jnp.float32)
            pooled = []
            for gi, w in enumerate(POOL_WINDOWS):
                c0, c1 = gi * POOL_GROUP, (gi + 1) * POOL_GROUP
                win = ext_rows(0, c0, c1)
                for back in range(1, w):
                    win = win + ext_rows(back, c0, c1)
                mean = win / jnp.minimum(valid, float(w))
                pooled.append((mean - ext_rows(0, c0, c1)).astype(jnp.bfloat16))

            yconv = wc(0) * vext_rows(2) + wc(1) * vext_rows(1) + wc(2) * vext_rows(0)
            return pooled, yconv
        return pool_conv

    starts = range(0, tile, rows)
    ys = _run_interleaved([
        _layer_stages(x_ref[r0:r0 + rows, :], p_ref[r0:r0 + rows, :], vectors, w_vmem,
                      act_ref.at[r0:r0 + rows, :], pool_conv_at(r0)) for r0 in starts],
        PROMPT_LAG)
    for r0, y in zip(starts, ys):
        y_ref[r0:r0 + rows, :] = y

    u_tail = ext_ref[tile:tile + POOL_PAD, :]
    v_tail = vext_ref[tile:tile + CONV_PAD, :]
    u_carry_ref[...] = u_tail
    v_carry_ref[...] = v_tail
    seq = t // tiles_per_seq
    pool_state_ref[:, pl.ds(seq, 1), :] = u_tail[POOL_PAD - POOL_HIST:][:, None, :]
    conv_state_ref[0] = v_tail[CONV_PAD - CONV_HIST:]


def _sample_tile(x_ref, p_ref, pool_hist_ref, conv_hist_ref, vectors, w_vmem,
                 y_ref, pool_new_ref, conv_new_ref, act_ref, *, seqs, steps):
    wc = functools.partial(_conv_tap, vectors[2])
    n = seqs // SAMPLE_CHAINS

    def chain(q0):
        qs = slice(q0, q0 + n)
        by_step = lambda ref: jnp.concatenate([ref[qs, s, :] for s in range(steps)], axis=0)
        slab = lambda a, s: a[s * n:(s + 1) * n]

        def pool_conv(u, v):
            line = ([pool_hist_ref[s, qs, :] for s in range(POOL_HIST)]
                    + [slab(u, s) for s in range(steps)])
            pooled = []
            for gi, w in enumerate(POOL_WINDOWS):
                c0, c1 = gi * POOL_GROUP, (gi + 1) * POOL_GROUP
                per_step = []
                for s in range(steps):
                    now = POOL_HIST + s
                    win = line[now][:, c0:c1]
                    for back in range(1, w):
                        win = win + line[now - back][:, c0:c1]
                    per_step.append(win * (1.0 / w) - line[now][:, c0:c1])
                pooled.append(jnp.concatenate(per_step, axis=0).astype(jnp.bfloat16))
            for s in range(POOL_HIST):
                pool_new_ref[s, qs, :] = line[steps + s]

            vline = ([conv_hist_ref[qs, s, :] for s in range(CONV_HIST)]
                     + [slab(v, s) for s in range(steps)])
            yconv = jnp.concatenate(
                [wc(0) * vline[s] + wc(1) * vline[s + 1] + wc(2) * vline[s + 2]
                 for s in range(steps)], axis=0)
            for s in range(CONV_HIST):
                conv_new_ref[qs, s, :] = vline[steps + s]
            return pooled, yconv

        return _layer_stages(by_step(x_ref), by_step(p_ref), vectors, w_vmem,
                             act_ref.at[q0 * steps:(q0 + n) * steps, :], pool_conv)

    starts = range(0, seqs, n)
    for q0, y in zip(starts, _run_interleaved([chain(q0) for q0 in starts], SAMPLE_LAG)):
        for s in range(steps):
            y_ref[q0:q0 + n, s, :] = y[s * n:(s + 1) * n]


def _layer_kernel(*refs, n_prompt_tiles, tile_prompt, tiles_per_seq, seqs, steps):
    xp_ref, pp_ref, xs_ref, ps_ref, pool_hist_ref, conv_hist_ref = refs[:N_INPUTS]
    refs = refs[N_INPUTS:]
    vectors, refs = refs[:N_VECTORS], refs[N_VECTORS:]
    w_hbm, refs = refs[:N_MATRICES], refs[N_MATRICES:]
    (yp_ref, pool_state_ref, conv_state_ref, ys_ref, pool_new_ref,
     conv_new_ref) = refs[:N_OUTPUTS]
    refs = refs[N_OUTPUTS:]
    u_carry_ref, v_carry_ref, x_buf, p_buf, y_buf, in_sem, out_sem = refs[:7]
    w_vmem = refs[7:]
    f32, bf16 = jnp.float32, jnp.bfloat16
    in_copies = functools.partial(_prompt_in_copies, x_hbm=xp_ref, p_hbm=pp_ref, x_buf=x_buf,
                                  p_buf=p_buf, sem=in_sem, tile=tile_prompt)
    out_copy = functools.partial(_prompt_out_copy, y_buf=y_buf, y_hbm=yp_ref, sem=out_sem,
                                 tile=tile_prompt)

    step = pl.program_id(0)

    @pl.when(step == 0)
    def _():
        for c in in_copies(0, 0):
            c.start()
        pl.run_scoped(
            functools.partial(_prep_weights, w_hbm, w_vmem),
            pltpu.VMEM((PREP_BUFS, PREP_ROWS, PREP_COLS), f32),
            pltpu.SemaphoreType.DMA((PREP_BUFS,)))

    @pl.when(jnp.logical_and(step >= 1, step <= n_prompt_tiles))
    def _():
        t = step - 1
        slot = t % 2
        for c in in_copies(t, slot):
            c.wait()

        @pl.when(t + 1 < n_prompt_tiles)
        def _():
            for c in in_copies(t + 1, 1 - slot):
                c.start()

        @pl.when(t >= 2)
        def _():
            out_copy(t - 2, slot).wait()

        pl.run_scoped(
            functools.partial(_prompt_tile, t, x_buf.at[slot], p_buf.at[slot], vectors, w_vmem,
                              y_buf.at[slot], pool_state_ref, conv_state_ref, u_carry_ref,
                              v_carry_ref, tile=tile_prompt, tiles_per_seq=tiles_per_seq),
            pltpu.VMEM((POOL_PAD + tile_prompt, POOL_WIDTH), f32),
            pltpu.VMEM((CONV_PAD + tile_prompt, CONV_WIDTH), f32),
            pltpu.VMEM((tile_prompt, D_FF), bf16))
        out_copy(t, slot).start(priority=1)

        @pl.when(t == n_prompt_tiles - 1)
        def _():
            out_copy(t - 1, 1 - slot).wait()
            out_copy(t, slot).wait()

    @pl.when(step > n_prompt_tiles)
    def _():
        pl.run_scoped(
            functools.partial(_sample_tile, xs_ref, ps_ref, pool_hist_ref, conv_hist_ref, vectors,
                              w_vmem, ys_ref, pool_new_ref, conv_new_ref, seqs=seqs, steps=steps),
            pltpu.VMEM((seqs * steps, D_FF), bf16))


def _resident(shape):
    zeros = (0,) * len(shape)
    return pl.BlockSpec(shape, lambda i: zeros, pipeline_mode=pl.Buffered(1))


def _run_layer(x_p, p_p, x_s, p_s, pool_hist, conv_hist, vectors, matrices, *, seq_len, steps):
    tokens_p, tokens_s = x_p.shape[0], x_s.shape[0] * steps
    assert x_s.shape[1] == steps
    tile_p, tile_s = TILE_PROMPT, TILE_SAMPLE
    assert tokens_p % tile_p == 0 and seq_len % tile_p == 0 and tokens_p >= 2 * tile_p
    assert tile_p % (PROMPT_CHAINS * SUBLANES) == 0 and tile_p // PROMPT_CHAINS >= POOL_PAD
    assert tokens_s % tile_s == 0 and tile_s % (SAMPLE_CHAINS * steps * SUBLANES) == 0
    assert len(vectors) == N_VECTORS and len(matrices) == N_MATRICES
    n_p, n_s = tokens_p // tile_p, tokens_s // tile_s
    tiles_per_seq = seq_len // tile_p
    n_seq = tokens_p // seq_len
    seqs = tile_s // steps

    tile_p_of = lambda i: jnp.clip(i - 1, 0, n_p - 1)
    tile_s_of = lambda i: jnp.clip(i - 1 - n_p, 0, n_s - 1)
    pool_state_p = pl.BlockSpec((POOL_HIST, n_seq, POOL_WIDTH), lambda i: (0, 0, 0))
    conv_state_p = pl.BlockSpec((1, CONV_HIST, CONV_WIDTH),
                                lambda i: (tile_p_of(i) // tiles_per_seq, 0, 0))
    row_s_in = lambda width: pl.BlockSpec((seqs, steps, width), lambda i: (tile_s_of(i), 0, 0),
                                          pipeline_mode=pl.Buffered(1))
    pool_s = pl.BlockSpec((POOL_HIST, seqs, POOL_WIDTH), lambda i: (0, tile_s_of(i), 0),
                          pipeline_mode=pl.Buffered(1))
    conv_s = pl.BlockSpec((seqs, CONV_HIST, CONV_WIDTH), lambda i: (tile_s_of(i), 0, 0),
                          pipeline_mode=pl.Buffered(1))

    f32 = jnp.float32
    out_shape = [
        jax.ShapeDtypeStruct((tokens_p, D_MODEL), f32),
        jax.ShapeDtypeStruct((POOL_HIST, n_seq, POOL_WIDTH), f32),
        jax.ShapeDtypeStruct((n_seq, CONV_HIST, CONV_WIDTH), f32),
        jax.ShapeDtypeStruct((tokens_s // steps, steps, D_MODEL), f32),
        jax.ShapeDtypeStruct((POOL_HIST, tokens_s // steps, POOL_WIDTH), f32),
        jax.ShapeDtypeStruct((tokens_s // steps, CONV_HIST, CONV_WIDTH), f32),
    ]
    out_specs = [pl.BlockSpec(memory_space=pl.ANY), pool_state_p, conv_state_p,
                 pl.BlockSpec((seqs, steps, D_MODEL), lambda i: (tile_s_of(i), 0, 0),
                              pipeline_mode=pl.Buffered(1)),
                 pool_s, conv_s]
    scratch = [pltpu.VMEM((POOL_PAD, POOL_WIDTH), f32), pltpu.VMEM((CONV_PAD, CONV_WIDTH), f32),
               pltpu.VMEM((2, tile_p, D_MODEL), f32), pltpu.VMEM((2, tile_p, D_PLE), f32),
               pltpu.VMEM((2, tile_p, D_MODEL), f32),
               pltpu.SemaphoreType.DMA((2, 2)), pltpu.SemaphoreType.DMA((2,))]
    scratch += [pltpu.VMEM(w.shape, jnp.bfloat16) for w in matrices]

    kernel = functools.partial(_layer_kernel, n_prompt_tiles=n_p, tile_prompt=tile_p,
                               tiles_per_seq=tiles_per_seq, seqs=seqs, steps=steps)
    return pl.pallas_call(
        kernel,
        grid=(1 + n_p + n_s,),
        in_specs=([pl.BlockSpec(memory_space=pl.ANY), pl.BlockSpec(memory_space=pl.ANY),
                   row_s_in(D_MODEL), row_s_in(D_PLE),
                   pool_s, conv_s]
                  + [_resident(v.shape) for v in vectors]
                  + [pl.BlockSpec(memory_space=pl.ANY)] * N_MATRICES),
        out_specs=out_specs,
        out_shape=out_shape,
        scratch_shapes=scratch,
        compiler_params=pltpu.CompilerParams(
            dimension_semantics=("arbitrary",),
            vmem_limit_bytes=VMEM_LIMIT_BYTES),
        name="hybrid_layer",
    )(x_p, p_p, x_s, p_s, pool_hist, conv_hist, *vectors, *matrices)


def kernel(x_prompt, x_sample, state_pool, state_conv, p_prompt, p_sample, g_mix, w_in, w_pool_group, pool_scale, w_pool_up, w_conv, w_conv_out, w_o, g_ffn, w_ffn_in, w_ffn_out, g_ple, w_ple, w_ple_gate, g_final):
    depth = g_mix.shape[0]
    assert depth == 1, "single-layer step"
    vec = lambda a: a.reshape(1, -1)
    vectors = (vec(g_mix[0]), vec(pool_scale[0]), vec(w_conv[0]), vec(g_ffn[0]), vec(g_ple[0]),
               vec(g_final))
    matrices = (w_in[0], w_pool_group[0].reshape(POOL_WIDTH, POOL_GROUP), w_pool_up[0],
                w_conv_out[0], w_o[0], w_ffn_in[0], w_ffn_out[0], w_ple[0], w_ple_gate[0])

    batch, seq, _ = x_prompt.shape
    dec_seq = x_sample.shape[1]

    y_p, pool_state_p, conv_state_p, y_s, pool_state_s, conv_state_s = _run_layer(
        x_prompt.reshape(batch * seq, D_MODEL), p_prompt[0].reshape(batch * seq, D_PLE),
        x_sample, p_sample[0],
        jnp.swapaxes(state_pool[0], 0, 1), state_conv[0],
        vectors, matrices, seq_len=seq, steps=dec_seq)

    return (y_p.reshape(batch, seq, D_MODEL), y_s,
            jnp.swapaxes(pool_state_p, 0, 1)[None], conv_state_p[None],
            jnp.swapaxes(pool_state_s, 0, 1)[None], conv_state_s[None])
```

```python
import functools

import jax
import jax.numpy as jnp
from jax.experimental import pallas as pl
from jax.experimental.pallas import tpu as pltpu

D_MODEL = 1024
POOL_WINDOWS = (2, 4, 8, 16)
POOL_GROUP = 128
POOL_WIDTH = 512
POOL_HIST = 15
CONV_WIDTH = 512
CONV_K = 3
CONV_HIST = 2
D_FF = 2816
D_PLE = 256
EPS = 1e-6

OFF_U = 0
OFF_B = OFF_U + POOL_WIDTH
OFF_C = OFF_B + CONV_WIDTH
OFF_H = OFF_C + CONV_WIDTH
OFF_GP = OFF_H + CONV_WIDTH
OFF_GC = OFF_GP + D_MODEL

V7X_VMEM_BYTES = 64 * 1024 * 1024
V7X_MXU_WIDTH = 256
SUBLANES = 8
DMA_PRIORITIES = 2

POOL_PAD = 16
CONV_PAD = 8
FF_CHUNK = V7X_MXU_WIDTH
TILE_PROMPT = 512
TILE_SAMPLE = 256
PROMPT_CHAINS = 2
SAMPLE_CHAINS = 2
PROMPT_LAG = 1
SAMPLE_LAG = 0
PREP_ROWS = 256
PREP_COLS = 1024
PREP_BUFS = 8
VMEM_RESERVE_BYTES = 4 * 1024 * 1024
VMEM_LIMIT_BYTES = V7X_VMEM_BYTES - VMEM_RESERVE_BYTES

N_VECTORS = 6
N_MATRICES = 9
N_INPUTS = 6
N_OUTPUTS = 6


def _rmsnorm(x, g):
    ms = jnp.mean(x * x, axis=-1, keepdims=True)
    return x * jax.lax.rsqrt(ms + EPS) * g


def _sigmoid(x):
    return 0.5 * jnp.tanh(0.5 * x) + 0.5


def _dot(a, b):
    return jnp.dot(a, b, preferred_element_type=jnp.float32)


def _conv_tap(w_conv_ref, k):
    return w_conv_ref[:, k * CONV_WIDTH:(k + 1) * CONV_WIDTH]


def _prep_weights(w_hbm, w_vmem, stage_ref, sem):
    chunks = []
    for src, dst in zip(w_hbm, w_vmem):
        n_rows, n_cols = dst.shape
        for r0 in range(0, n_rows, PREP_ROWS):
            for c0 in range(0, n_cols, PREP_COLS):
                chunks.append((src, dst, r0, min(PREP_ROWS, n_rows - r0),
                               c0, min(PREP_COLS, n_cols - c0)))

    def chunk_copy(k):
        src, _, r0, nr, c0, nc = chunks[k]
        slot = k % PREP_BUFS
        return pltpu.make_async_copy(src.at[r0:r0 + nr, c0:c0 + nc],
                                     stage_ref.at[slot, :nr, :nc], sem.at[slot])

    for k in range(min(PREP_BUFS, len(chunks))):
        chunk_copy(k).start(priority=k % DMA_PRIORITIES)
    for k, (_, dst, r0, nr, c0, nc) in enumerate(chunks):
        chunk_copy(k).wait()
        dst[r0:r0 + nr, c0:c0 + nc] = stage_ref[k % PREP_BUFS, :nr, :nc].astype(jnp.bfloat16)
        if k + PREP_BUFS < len(chunks):
            chunk_copy(k + PREP_BUFS).start(priority=(k + PREP_BUFS) % DMA_PRIORITIES)


def _prompt_in_copies(t, slot, x_hbm, p_hbm, x_buf, p_buf, sem, tile):
    rows = pl.ds(t * tile, tile)
    return (pltpu.make_async_copy(x_hbm.at[rows, :], x_buf.at[slot], sem.at[slot, 0]),
            pltpu.make_async_copy(p_hbm.at[rows, :], p_buf.at[slot], sem.at[slot, 1]))


def _prompt_out_copy(t, slot, y_buf, y_hbm, sem, tile):
    return pltpu.make_async_copy(y_buf.at[slot], y_hbm.at[pl.ds(t * tile, tile), :], sem.at[slot])


def _layer_stages(x, p, vectors, w_vmem, act_ref, pool_conv):
    g_mix_ref, pool_scale_ref, _, g_ffn_ref, g_ple_ref, g_final_ref = vectors
    (w_in_ref, w_group_ref, w_pool_up_ref, w_conv_out_ref, w_o_ref, w_ffn_in_ref,
     w_ffn_out_ref, w_ple_ref, w_ple_gate_ref) = w_vmem
    bf16 = jnp.bfloat16

    hn = _rmsnorm(x, g_mix_ref[...]).astype(bf16)

    def proj(off, width):
        return _dot(hn, w_in_ref[:, off:off + width])

    u = proj(OFF_U, POOL_WIDTH)
    yield
    v = proj(OFF_C, CONV_WIDTH) * proj(OFF_H, CONV_WIDTH)
    yield
    pooled, yconv = pool_conv(u, v)
    conv_out = (proj(OFF_B, CONV_WIDTH) * yconv).astype(bf16)
    yield
    gate_pool = _sigmoid(proj(OFF_GP, D_MODEL))
    yield
    gate_conv = _sigmoid(proj(OFF_GC, D_MODEL))
    yield
    mixed = [_dot(d, w_group_ref[gi * POOL_GROUP:(gi + 1) * POOL_GROUP, :])
             for gi, d in enumerate(pooled)]
    pool_out = (jnp.concatenate(mixed, axis=-1) * pool_scale_ref[...]).astype(bf16)
    yield
    gated_pool = gate_pool * _dot(pool_out, w_pool_up_ref[...])
    yield
    merged = (gated_pool + gate_conv * _dot(conv_out, w_conv_out_ref[...])).astype(bf16)
    yield
    x = x + _dot(merged, w_o_ref[...])
    yield

    hn = _rmsnorm(x, g_ffn_ref[...]).astype(bf16)
    for c0 in range(0, D_FF, FF_CHUNK):
        gate = _dot(hn, w_ffn_in_ref[:, c0:c0 + FF_CHUNK])
        up = _dot(hn, w_ffn_in_ref[:, D_FF + c0:D_FF + c0 + FF_CHUNK])
        act_ref[:, c0:c0 + FF_CHUNK] = (gate * _sigmoid(gate) * up).astype(bf16)
        yield
    x = x + _dot(act_ref[...], w_ffn_out_ref[...])
    yield

    hn = _rmsnorm(x, g_ple_ref[...]).astype(bf16)
    ple_gate = _sigmoid(_dot(hn, w_ple_gate_ref[...]))
    yield
    x = x + ple_gate * _dot(p.astype(bf16), w_ple_ref[...])
    return _rmsnorm(x, g_final_ref[...])


def _run_interleaved(chains, lag):
    results = [None] * len(chains)
    live = list(range(len(chains)))
    rounds = 0
    while live:
        for i in list(live):
            if rounds < i * lag:
                continue
            try:
                next(chains[i])
            except StopIteration as done:
                results[i] = done.value
                live.remove(i)
        rounds += 1
    return results


def _prompt_tile(t, x_ref, p_ref, vectors, w_vmem, y_ref, pool_state_ref, conv_state_ref,
                 u_carry_ref, v_carry_ref, ext_ref, vext_ref, act_ref, *, tile, tiles_per_seq):
    wc = functools.partial(_conv_tap, vectors[2])
    rows = tile // PROMPT_CHAINS

    @pl.when(t % tiles_per_seq == 0)
    def _():
        u_carry_ref[...] = jnp.zeros((POOL_PAD, POOL_WIDTH), jnp.float32)
        v_carry_ref[...] = jnp.zeros((CONV_PAD, CONV_WIDTH), jnp.float32)

    ext_ref[:POOL_PAD, :] = u_carry_ref[...]
    vext_ref[:CONV_PAD, :] = v_carry_ref[...]

    def pool_conv_at(r0):
        def pool_conv(u, v):
            ext_ref[POOL_PAD + r0:POOL_PAD + r0 + rows, :] = u
            vext_ref[CONV_PAD + r0:CONV_PAD + r0 + rows, :] = v

            def ext_rows(back, c0, c1):
                return ext_ref[POOL_PAD + r0 - back:POOL_PAD + r0 - back + rows, c0:c1]

            def vext_rows(back):
                return vext_ref[CONV_PAD + r0 - back:CONV_PAD + r0 - back + rows, :]

            pos = (t % tiles_per_seq) * tile + r0 + jax.lax.broadcasted_iota(
                jnp.int32, (rows, POOL_GROUP), 0)
            valid = (pos + 1).astype(jnp.float32)
            pooled = []
            for gi, w in enumerate(POOL_WINDOWS):
                c0, c1 = gi * POOL_GROUP, (gi + 1) * POOL_GROUP
                win = ext_rows(0, c0, c1)
                for back in range(1, w):
                    win = win + ext_rows(back, c0, c1)
                mean = win / jnp.minimum(valid, float(w))
                pooled.append((mean - ext_rows(0, c0, c1)).astype(jnp.bfloat16))

            yconv = wc(0) * vext_rows(2) + wc(1) * vext_rows(1) + wc(2) * vext_rows(0)
            return pooled, yconv
        return pool_conv

    starts = range(0, tile, rows)
    ys = _run_interleaved([
        _layer_stages(x_ref[r0:r0 + rows, :], p_ref[r0:r0 + rows, :], vectors, w_vmem,
                      act_ref.at[r0:r0 + rows, :], pool_conv_at(r0)) for r0 in starts],
        PROMPT_LAG)
    for r0, y in zip(starts, ys):
        y_ref[r0:r0 + rows, :] = y

    u_tail = ext_ref[tile:tile + POOL_PAD, :]
    v_tail = vext_ref[tile:tile + CONV_PAD, :]
    u_carry_ref[...] = u_tail
    v_carry_ref[...] = v_tail
    seq = t // tiles_per_seq
    pool_state_ref[:, pl.ds(seq, 1), :] = u_tail[POOL_PAD - POOL_HIST:][:, None, :]
    conv_state_ref[0] = v_tail[CONV_PAD - CONV_HIST:]


def _sample_tile(x_ref, p_ref, pool_hist_ref, conv_hist_ref, vectors, w_vmem,
                 y_ref, pool_new_ref, conv_new_ref, act_ref, *, seqs, steps):
    wc = functools.partial(_conv_tap, vectors[2])
    n = seqs // SAMPLE_CHAINS

    def chain(q0):
        qs = slice(q0, q0 + n)
        by_step = lambda ref: jnp.concatenate([ref[qs, s, :] for s in range(steps)], axis=0)
        slab = lambda a, s: a[s * n:(s + 1) * n]

        def pool_conv(u, v):
            line = ([pool_hist_ref[s, qs, :] for s in range(POOL_HIST)]
                    + [slab(u, s) for s in range(steps)])
            pooled = []
            for gi, w in enumerate(POOL_WINDOWS):
                c0, c1 = gi * POOL_GROUP, (gi + 1) * POOL_GROUP
                per_step = []
                for s in range(steps):
                    now = POOL_HIST + s
                    win = line[now][:, c0:c1]
                    for back in range(1, w):
                        win = win + line[now - back][:, c0:c1]
                    per_step.append(win * (1.0 / w) - line[now][:, c0:c1])
                pooled.append(jnp.concatenate(per_step, axis=0).astype(jnp.bfloat16))
            for s in range(POOL_HIST):
                pool_new_ref[s, qs, :] = line[steps + s]

            vline = ([conv_hist_ref[qs, s, :] for s in range(CONV_HIST)]
                     + [slab(v, s) for s in range(steps)])
            yconv = jnp.concatenate(
                [wc(0) * vline[s] + wc(1) * vline[s + 1] + wc(2) * vline[s + 2]
                 for s in range(steps)], axis=0)
            for s in range(CONV_HIST):
                conv_new_ref[qs, s, :] = vline[steps + s]
            return pooled, yconv

        return _layer_stages(by_step(x_ref), by_step(p_ref), vectors, w_vmem,
                             act_ref.at[q0 * steps:(q0 + n) * steps, :], pool_conv)

    starts = range(0, seqs, n)
    for q0, y in zip(starts, _run_interleaved([chain(q0) for q0 in starts], SAMPLE_LAG)):
        for s in range(steps):
            y_ref[q0:q0 + n, s, :] = y[s * n:(s + 1) * n]


def _layer_kernel(*refs, n_prompt_tiles, tile_prompt, tiles_per_seq, seqs, steps):
    xp_ref, pp_ref, xs_ref, ps_ref, pool_hist_ref, conv_hist_ref = refs[:N_INPUTS]
    refs = refs[N_INPUTS:]
    vectors, refs = refs[:N_VECTORS], refs[N_VECTORS:]
    w_hbm, refs = refs[:N_MATRICES], refs[N_MATRICES:]
    (yp_ref, pool_state_ref, conv_state_ref, ys_ref, pool_new_ref,
     conv_new_ref) = refs[:N_OUTPUTS]
    refs = refs[N_OUTPUTS:]
    u_carry_ref, v_carry_ref, x_buf, p_buf, y_buf, in_sem, out_sem = refs[:7]
    w_vmem = refs[7:]
    f32, bf16 = jnp.float32, jnp.bfloat16
    in_copies = functools.partial(_prompt_in_copies, x_hbm=xp_ref, p_hbm=pp_ref, x_buf=x_buf,
                                  p_buf=p_buf, sem=in_sem, tile=tile_prompt)
    out_copy = functools.partial(_prompt_out_copy, y_buf=y_buf, y_hbm=yp_ref, sem=out_sem,
                                 tile=tile_prompt)

    step = pl.program_id(0)

    @pl.when(step == 0)
    def _():
        for c in in_copies(0, 0):
            c.start()
        pl.run_scoped(
            functools.partial(_prep_weights, w_hbm, w_vmem),
            pltpu.VMEM((PREP_BUFS, PREP_ROWS, PREP_COLS), f32),
            pltpu.SemaphoreType.DMA((PREP_BUFS,)))

    @pl.when(jnp.logical_and(step >= 1, step <= n_prompt_tiles))
    def _():
        t = step - 1
        slot = t % 2
        for c in in_copies(t, slot):
            c.wait()

        @pl.when(t + 1 < n_prompt_tiles)
        def _():
            for c in in_copies(t + 1, 1 - slot):
                c.start()

        @pl.when(t >= 2)
        def _():
            out_copy(t - 2, slot).wait()

        pl.run_scoped(
            functools.partial(_prompt_tile, t, x_buf.at[slot], p_buf.at[slot], vectors, w_vmem,
                              y_buf.at[slot], pool_state_ref, conv_state_ref, u_carry_ref,
                              v_carry_ref, tile=tile_prompt, tiles_per_seq=tiles_per_seq),
            pltpu.VMEM((POOL_PAD + tile_prompt, POOL_WIDTH), f32),
            pltpu.VMEM((CONV_PAD + tile_prompt, CONV_WIDTH), f32),
            pltpu.VMEM((tile_prompt, D_FF), bf16))
        out_copy(t, slot).start(priority=1)

        @pl.when(t == n_prompt_tiles - 1)
        def _():
            out_copy(t - 1, 1 - slot).wait()
            out_copy(t, slot).wait()

    @pl.when(step > n_prompt_tiles)
    def _():
        pl.run_scoped(
            functools.partial(_sample_tile, xs_ref, ps_ref, pool_hist_ref, conv_hist_ref, vectors,
                              w_vmem, ys_ref, pool_new_ref, conv_new_ref, seqs=seqs, steps=steps),
            pltpu.VMEM((seqs * steps, D_FF), bf16))


def _resident(shape):
    zeros = (0,) * len(shape)
    return pl.BlockSpec(shape, lambda i: zeros, pipeline_mode=pl.Buffered(1))


def _run_layer(x_p, p_p, x_s, p_s, pool_hist, conv_hist, vectors, matrices, *, seq_len, steps):
    tokens_p, tokens_s = x_p.shape[0], x_s.shape[0] * steps
    assert x_s.shape[1] == steps
    tile_p, tile_s = TILE_PROMPT, TILE_SAMPLE
    assert tokens_p % tile_p == 0 and seq_len % tile_p == 0 and tokens_p >= 2 * tile_p
    assert tile_p % (PROMPT_CHAINS * SUBLANES) == 0 and tile_p // PROMPT_CHAINS >= POOL_PAD
    assert tokens_s % tile_s == 0 and tile_s % (SAMPLE_CHAINS * steps * SUBLANES) == 0
    assert len(vectors) == N_VECTORS and len(matrices) == N_MATRICES
    n_p, n_s = tokens_p // tile_p, tokens_s // tile_s
    tiles_per_seq = seq_len // tile_p
    n_seq = tokens_p // seq_len
    seqs = tile_s // steps

    tile_p_of = lambda i: jnp.clip(i - 1, 0, n_p - 1)
    tile_s_of = lambda i: jnp.clip(i - 1 - n_p, 0, n_s - 1)
    pool_state_p = pl.BlockSpec((POOL_HIST, n_seq, POOL_WIDTH), lambda i: (0, 0, 0))
    conv_state_p = pl.BlockSpec((1, CONV_HIST, CONV_WIDTH),
                                lambda i: (tile_p_of(i) // tiles_per_seq, 0, 0))
    row_s_in = lambda width: pl.BlockSpec((seqs, steps, width), lambda i: (tile_s_of(i), 0, 0),
                                          pipeline_mode=pl.Buffered(1))
    pool_s = pl.BlockSpec((POOL_HIST, seqs, POOL_WIDTH), lambda i: (0, tile_s_of(i), 0),
                          pipeline_mode=pl.Buffered(1))
    conv_s = pl.BlockSpec((seqs, CONV_HIST, CONV_WIDTH), lambda i: (tile_s_of(i), 0, 0),
                          pipeline_mode=pl.Buffered(1))

    f32 = jnp.float32
    out_shape = [
        jax.ShapeDtypeStruct((tokens_p, D_MODEL), f32),
        jax.ShapeDtypeStruct((POOL_HIST, n_seq, POOL_WIDTH), f32),
        jax.ShapeDtypeStruct((n_seq, CONV_HIST, CONV_WIDTH), f32),
        jax.ShapeDtypeStruct((tokens_s // steps, steps, D_MODEL), f32),
        jax.ShapeDtypeStruct((POOL_HIST, tokens_s // steps, POOL_WIDTH), f32),
        jax.ShapeDtypeStruct((tokens_s // steps, CONV_HIST, CONV_WIDTH), f32),
    ]
    out_specs = [pl.BlockSpec(memory_space=pl.ANY), pool_state_p, conv_state_p,
                 pl.BlockSpec((seqs, steps, D_MODEL), lambda i: (tile_s_of(i), 0, 0),
                              pipeline_mode=pl.Buffered(1)),
                 pool_s, conv_s]
    scratch = [pltpu.VMEM((POOL_PAD, POOL_WIDTH), f32), pltpu.VMEM((CONV_PAD, CONV_WIDTH), f32),
               pltpu.VMEM((2, tile_p, D_MODEL), f32), pltpu.VMEM((2, tile_p, D_PLE), f32),
               pltpu.VMEM((2, tile_p, D_MODEL), f32),
               pltpu.SemaphoreType.DMA((2, 2)), pltpu.SemaphoreType.DMA((2,))]
    scratch += [pltpu.VMEM(w.shape, jnp.bfloat16) for w in matrices]

    kernel = functools.partial(_layer_kernel, n_prompt_tiles=n_p, tile_prompt=tile_p,
                               tiles_per_seq=tiles_per_seq, seqs=seqs, steps=steps)
    return pl.pallas_call(
        kernel,
        grid=(1 + n_p + n_s,),
        in_specs=([pl.BlockSpec(memory_space=pl.ANY), pl.BlockSpec(memory_space=pl.ANY),
                   row_s_in(D_MODEL), row_s_in(D_PLE),
                   pool_s, conv_s]
                  + [_resident(v.shape) for v in vectors]
                  + [pl.BlockSpec(memory_space=pl.ANY)] * N_MATRICES),
        out_specs=out_specs,
        out_shape=out_shape,
        scratch_shapes=scratch,
        compiler_params=pltpu.CompilerParams(
            dimension_semantics=("arbitrary",),
            vmem_limit_bytes=VMEM_LIMIT_BYTES),
        name="hybrid_layer",
    )(x_p, p_p, x_s, p_s, pool_hist, conv_hist, *vectors, *matrices)


def kernel(x_prompt, x_sample, state_pool, state_conv, p_prompt, p_sample, g_mix, w_in, w_pool_group, pool_scale, w_pool_up, w_conv, w_conv_out, w_o, g_ffn, w_ffn_in, w_ffn_out, g_ple, w_ple, w_ple_gate, g_final):
    depth = g_mix.shape[0]
    assert depth == 1, "single-layer step"
    vec = lambda a: a.reshape(1, -1)
    vectors = (vec(g_mix[0]), vec(pool_scale[0]), vec(w_conv[0]), vec(g_ffn[0]), vec(g_ple[0]),
               vec(g_final))
    matrices = (w_in[0], w_pool_group[0].reshape(POOL_WIDTH, POOL_GROUP), w_pool_up[0],
                w_conv_out[0], w_o[0], w_ffn_in[0], w_ffn_out[0], w_ple[0], w_ple_gate[0])

    batch, seq, _ = x_prompt.shape
    dec_seq = x_sample.shape[1]

    y_p, pool_state_p, conv_state_p, y_s, pool_state_s, conv_state_s = _run_layer(
        x_prompt.reshape(batch * seq, D_MODEL), p_prompt[0].reshape(batch * seq, D_PLE),
        x_sample, p_sample[0],
        jnp.swapaxes(state_pool[0], 0, 1), state_conv[0],
        vectors, matrices, seq_len=seq, steps=dec_seq)

    return (y_p.reshape(batch, seq, D_MODEL), y_s,
            jnp.swapaxes(pool_state_p, 0, 1)[None], conv_state_p[None],
            jnp.swapaxes(pool_state_s, 0, 1)[None], conv_state_s[None])
```

```python
import functools

import jax
import jax.numpy as jnp
from jax.experimental import pallas as pl
from jax.experimental.pallas import tpu as pltpu

D_MODEL = 1024
POOL_WINDOWS = (2, 4, 8, 16)
POOL_GROUP = 128
POOL_WIDTH = 512
POOL_HIST = 15
CONV_WIDTH = 512
CONV_K = 3
CONV_HIST = 2
D_FF = 2816
D_PLE = 256
EPS = 1e-6

OFF_U = 0
OFF_B = OFF_U + POOL_WIDTH
OFF_C = OFF_B + CONV_WIDTH
OFF_H = OFF_C + CONV_WIDTH
OFF_GP = OFF_H + CONV_WIDTH
OFF_GC = OFF_GP + D_MODEL

V7X_VMEM_BYTES = 64 * 1024 * 1024
V7X_MXU_WIDTH = 256
SUBLANES = 8
DMA_PRIORITIES = 2

POOL_PAD = 16
CONV_PAD = 8
FF_CHUNK = V7X_MXU_WIDTH
TILE_PROMPT = 512
TILE_SAMPLE = 256
PROMPT_CHAINS = 2
SAMPLE_CHAINS = 2
PROMPT_LAG = 1
SAMPLE_LAG = 0
PREP_ROWS = 256
PREP_COLS = 1024
PREP_BUFS = 8
VMEM_RESERVE_BYTES = 4 * 1024 * 1024
VMEM_LIMIT_BYTES = V7X_VMEM_BYTES - VMEM_RESERVE_BYTES

N_VECTORS = 6
N_MATRICES = 9
N_INPUTS = 6
N_OUTPUTS = 6


def _rmsnorm(x, g):
    ms = jnp.mean(x * x, axis=-1, keepdims=True)
    return x * jax.lax.rsqrt(ms + EPS) * g


def _dot(a, b):
    return jnp.dot(a, b, preferred_element_type=jnp.float32)


def _conv_tap(w_conv_ref, k):
    return w_conv_ref[:, k * CONV_WIDTH:(k + 1) * CONV_WIDTH]


def _prep_weights(w_hbm, w_vmem, stage_ref, sem):
    chunks = []
    for src, dst in zip(w_hbm, w_vmem):
        n_rows, n_cols = dst.shape
        for r0 in range(0, n_rows, PREP_ROWS):
            for c0 in range(0, n_cols, PREP_COLS):
                chunks.append((src, dst, r0, min(PREP_ROWS, n_rows - r0),
                               c0, min(PREP_COLS, n_cols - c0)))

    def chunk_copy(k):
        src, _, r0, nr, c0, nc = chunks[k]
        slot = k % PREP_BUFS
        return pltpu.make_async_copy(src.at[r0:r0 + nr, c0:c0 + nc],
                                     stage_ref.at[slot, :nr, :nc], sem.at[slot])

    for k in range(min(PREP_BUFS, len(chunks))):
        chunk_copy(k).start(priority=k % DMA_PRIORITIES)
    for k, (_, dst, r0, nr, c0, nc) in enumerate(chunks):
        chunk_copy(k).wait()
        dst[r0:r0 + nr, c0:c0 + nc] = stage_ref[k % PREP_BUFS, :nr, :nc].astype(jnp.bfloat16)
        if k + PREP_BUFS < len(chunks):
            chunk_copy(k + PREP_BUFS).start(priority=(k + PREP_BUFS) % DMA_PRIORITIES)


def _prompt_in_copies(t, slot, x_hbm, p_hbm, x_buf, p_buf, sem, tile):
    rows = pl.ds(t * tile, tile)
    return (pltpu.make_async_copy(x_hbm.at[rows, :], x_buf.at[slot], sem.at[slot, 0]),
            pltpu.make_async_copy(p_hbm.at[rows, :], p_buf.at[slot], sem.at[slot, 1]))


def _prompt_out_copy(t, slot, y_buf, y_hbm, sem, tile):
    return pltpu.make_async_copy(y_buf.at[slot], y_hbm.at[pl.ds(t * tile, tile), :], sem.at[slot])


def _layer_stages(x, p, vectors, w_vmem, act_ref, pool_conv):
    g_mix_ref, pool_scale_ref, _, g_ffn_ref, g_ple_ref, g_final_ref = vectors
    (w_in_ref, w_group_ref, w_pool_up_ref, w_conv_out_ref, w_o_ref, w_ffn_in_ref,
     w_ffn_out_ref, w_ple_ref, w_ple_gate_ref) = w_vmem
    bf16 = jnp.bfloat16

    hn = _rmsnorm(x, g_mix_ref[...]).astype(bf16)

    def proj(off, width):
        return _dot(hn, w_in_ref[:, off:off + width])

    u = proj(OFF_U, POOL_WIDTH)
    yield
    v = proj(OFF_C, CONV_WIDTH) * proj(OFF_H, CONV_WIDTH)
    yield
    pooled, yconv = pool_conv(u, v)
    conv_out = (proj(OFF_B, CONV_WIDTH) * yconv).astype(bf16)
    yield
    gate_pool = jax.nn.sigmoid(proj(OFF_GP, D_MODEL))
    yield
    gate_conv = jax.nn.sigmoid(proj(OFF_GC, D_MODEL))
    yield
    mixed = [_dot(d, w_group_ref[gi * POOL_GROUP:(gi + 1) * POOL_GROUP, :])
             for gi, d in enumerate(pooled)]
    pool_out = (jnp.concatenate(mixed, axis=-1) * pool_scale_ref[...]).astype(bf16)
    yield
    gated_pool = gate_pool * _dot(pool_out, w_pool_up_ref[...])
    yield
    merged = (gated_pool + gate_conv * _dot(conv_out, w_conv_out_ref[...])).astype(bf16)
    yield
    x = x + _dot(merged, w_o_ref[...])
    yield

    hn = _rmsnorm(x, g_ffn_ref[...]).astype(bf16)
    for c0 in range(0, D_FF, FF_CHUNK):
        gate = _dot(hn, w_ffn_in_ref[:, c0:c0 + FF_CHUNK])
        up = _dot(hn, w_ffn_in_ref[:, D_FF + c0:D_FF + c0 + FF_CHUNK])
        act_ref[:, c0:c0 + FF_CHUNK] = (jax.nn.silu(gate) * up).astype(bf16)
        yield
    x = x + _dot(act_ref[...], w_ffn_out_ref[...])
    yield

    hn = _rmsnorm(x, g_ple_ref[...]).astype(bf16)
    ple_gate = jax.nn.sigmoid(_dot(hn, w_ple_gate_ref[...]))
    yield
    x = x + ple_gate * _dot(p.astype(bf16), w_ple_ref[...])
    return _rmsnorm(x, g_final_ref[...])


def _run_interleaved(chains, lag):
    results = [None] * len(chains)
    live = list(range(len(chains)))
    rounds = 0
    while live:
        for i in list(live):
            if rounds < i * lag:
                continue
            try:
                next(chains[i])
            except StopIteration as done:
                results[i] = done.value
                live.remove(i)
        rounds += 1
    return results


def _prompt_tile(t, x_ref, p_ref, vectors, w_vmem, y_ref, pool_state_ref, conv_state_ref,
                 u_carry_ref, v_carry_ref, ext_ref, vext_ref, act_ref, *, tile, tiles_per_seq):
    wc = functools.partial(_conv_tap, vectors[2])
    rows = tile // PROMPT_CHAINS

    @pl.when(t % tiles_per_seq == 0)
    def _():
        u_carry_ref[...] = jnp.zeros((POOL_PAD, POOL_WIDTH), jnp.float32)
        v_carry_ref[...] = jnp.zeros((CONV_PAD, CONV_WIDTH), jnp.float32)

    ext_ref[:POOL_PAD, :] = u_carry_ref[...]
    vext_ref[:CONV_PAD, :] = v_carry_ref[...]

    def pool_conv_at(r0):
        def pool_conv(u, v):
            ext_ref[POOL_PAD + r0:POOL_PAD + r0 + rows, :] = u
            vext_ref[CONV_PAD + r0:CONV_PAD + r0 + rows, :] = v

            def ext_rows(back, c0, c1):
                return ext_ref[POOL_PAD + r0 - back:POOL_PAD + r0 - back + rows, c0:c1]

            def vext_rows(back):
                return vext_ref[CONV_PAD + r0 - back:CONV_PAD + r0 - back + rows, :]

            pos = (t % tiles_per_seq) * tile + r0 + jax.lax.broadcasted_iota(
                jnp.int32, (rows, POOL_GROUP), 0)
            valid = (pos + 1).astype(jnp.float32)
            pooled = []
            inv_rows = 1.0 / jnp.minimum(valid, float(POOL_WINDOWS[-1]))
            for gi, w in enumerate(POOL_WINDOWS):
                c0, c1 = gi * POOL_GROUP, (gi + 1) * POOL_GROUP
                win = ext_rows(0, c0, c1)
                for back in range(1, w):
                    win = win + ext_rows(back, c0, c1)
                mean = win * jnp.maximum(inv_rows, 1.0 / w)
                pooled.append((mean - ext_rows(0, c0, c1)).astype(jnp.bfloat16))

            yconv = wc(0) * vext_rows(2) + wc(1) * vext_rows(1) + wc(2) * vext_rows(0)
            return pooled, yconv
        return pool_conv

    starts = range(0, tile, rows)
    ys = _run_interleaved([
        _layer_stages(x_ref[r0:r0 + rows, :], p_ref[r0:r0 + rows, :], vectors, w_vmem,
                      act_ref.at[r0:r0 + rows, :], pool_conv_at(r0)) for r0 in starts],
        PROMPT_LAG)
    for r0, y in zip(starts, ys):
        y_ref[r0:r0 + rows, :] = y

    u_tail = ext_ref[tile:tile + POOL_PAD, :]
    v_tail = vext_ref[tile:tile + CONV_PAD, :]
    u_carry_ref[...] = u_tail
    v_carry_ref[...] = v_tail
    seq = t // tiles_per_seq
    pool_state_ref[:, pl.ds(seq, 1), :] = u_tail[POOL_PAD - POOL_HIST:][:, None, :]
    conv_state_ref[0] = v_tail[CONV_PAD - CONV_HIST:]


def _sample_tile(x_ref, p_ref, pool_hist_ref, conv_hist_ref, vectors, w_vmem,
                 y_ref, pool_new_ref, conv_new_ref, act_ref, *, seqs, steps):
    wc = functools.partial(_conv_tap, vectors[2])
    n = seqs // SAMPLE_CHAINS

    def chain(q0):
        qs = slice(q0, q0 + n)
        by_step = lambda ref: jnp.concatenate([ref[qs, s, :] for s in range(steps)], axis=0)
        slab = lambda a, s: a[s * n:(s + 1) * n]

        def pool_conv(u, v):
            line = ([pool_hist_ref[s, qs, :] for s in range(POOL_HIST)]
                    + [slab(u, s) for s in range(steps)])
            pooled = []
            for gi, w in enumerate(POOL_WINDOWS):
                c0, c1 = gi * POOL_GROUP, (gi + 1) * POOL_GROUP
                per_step = []
                for s in range(steps):
                    now = POOL_HIST + s
                    win = line[now][:, c0:c1]
                    for back in range(1, w):
                        win = win + line[now - back][:, c0:c1]
                    per_step.append(win * (1.0 / w) - line[now][:, c0:c1])
                pooled.append(jnp.concatenate(per_step, axis=0).astype(jnp.bfloat16))
            for s in range(POOL_HIST):
                pool_new_ref[s, qs, :] = line[steps + s]

            vline = ([conv_hist_ref[qs, s, :] for s in range(CONV_HIST)]
                     + [slab(v, s) for s in range(steps)])
            yconv = jnp.concatenate(
                [wc(0) * vline[s] + wc(1) * vline[s + 1] + wc(2) * vline[s + 2]
                 for s in range(steps)], axis=0)
            for s in range(CONV_HIST):
                conv_new_ref[qs, s, :] = vline[steps + s]
            return pooled, yconv

        return _layer_stages(by_step(x_ref), by_step(p_ref), vectors, w_vmem,
                             act_ref.at[q0 * steps:(q0 + n) * steps, :], pool_conv)

    starts = range(0, seqs, n)
    for q0, y in zip(starts, _run_interleaved([chain(q0) for q0 in starts], SAMPLE_LAG)):
        for s in range(steps):
            y_ref[q0:q0 + n, s, :] = y[s * n:(s + 1) * n]


def _layer_kernel(*refs, n_prompt_tiles, tile_prompt, tiles_per_seq, seqs, steps):
    xp_ref, pp_ref, xs_ref, ps_ref, pool_hist_ref, conv_hist_ref = refs[:N_INPUTS]
    refs = refs[N_INPUTS:]
    vectors, refs = refs[:N_VECTORS], refs[N_VECTORS:]
    w_hbm, refs = refs[:N_MATRICES], refs[N_MATRICES:]
    (yp_ref, pool_state_ref, conv_state_ref, ys_ref, pool_new_ref,
     conv_new_ref) = refs[:N_OUTPUTS]
    refs = refs[N_OUTPUTS:]
    u_carry_ref, v_carry_ref, x_buf, p_buf, y_buf, in_sem, out_sem = refs[:7]
    w_vmem = refs[7:]
    f32, bf16 = jnp.float32, jnp.bfloat16
    in_copies = functools.partial(_prompt_in_copies, x_hbm=xp_ref, p_hbm=pp_ref, x_buf=x_buf,
                                  p_buf=p_buf, sem=in_sem, tile=tile_prompt)
    out_copy = functools.partial(_prompt_out_copy, y_buf=y_buf, y_hbm=yp_ref, sem=out_sem,
                                 tile=tile_prompt)

    step = pl.program_id(0)

    @pl.when(step == 0)
    def _():
        for c in in_copies(0, 0):
            c.start()
        pl.run_scoped(
            functools.partial(_prep_weights, w_hbm, w_vmem),
            pltpu.VMEM((PREP_BUFS, PREP_ROWS, PREP_COLS), f32),
            pltpu.SemaphoreType.DMA((PREP_BUFS,)))

    @pl.when(jnp.logical_and(step >= 1, step <= n_prompt_tiles))
    def _():
        t = step - 1
        slot = t % 2
        for c in in_copies(t, slot):
            c.wait()

        @pl.when(t + 1 < n_prompt_tiles)
        def _():
            for c in in_copies(t + 1, 1 - slot):
                c.start()

        @pl.when(t >= 2)
        def _():
            out_copy(t - 2, slot).wait()

        pl.run_scoped(
            functools.partial(_prompt_tile, t, x_buf.at[slot], p_buf.at[slot], vectors, w_vmem,
                              y_buf.at[slot], pool_state_ref, conv_state_ref, u_carry_ref,
                              v_carry_ref, tile=tile_prompt, tiles_per_seq=tiles_per_seq),
            pltpu.VMEM((POOL_PAD + tile_prompt, POOL_WIDTH), f32),
            pltpu.VMEM((CONV_PAD + tile_prompt, CONV_WIDTH), f32),
            pltpu.VMEM((tile_prompt, D_FF), bf16))
        out_copy(t, slot).start(priority=1)

        @pl.when(t == n_prompt_tiles - 1)
        def _():
            out_copy(t - 1, 1 - slot).wait()
            out_copy(t, slot).wait()

    @pl.when(step > n_prompt_tiles)
    def _():
        pl.run_scoped(
            functools.partial(_sample_tile, xs_ref, ps_ref, pool_hist_ref, conv_hist_ref, vectors,
                              w_vmem, ys_ref, pool_new_ref, conv_new_ref, seqs=seqs, steps=steps),
            pltpu.VMEM((seqs * steps, D_FF), bf16))


def _resident(shape):
    zeros = (0,) * len(shape)
    return pl.BlockSpec(shape, lambda i: zeros, pipeline_mode=pl.Buffered(1))


def _run_layer(x_p, p_p, x_s, p_s, pool_hist, conv_hist, vectors, matrices, *, seq_len, steps):
    tokens_p, tokens_s = x_p.shape[0], x_s.shape[0] * steps
    assert x_s.shape[1] == steps
    tile_p, tile_s = TILE_PROMPT, TILE_SAMPLE
    assert tokens_p % tile_p == 0 and seq_len % tile_p == 0 and tokens_p >= 2 * tile_p
    assert tile_p % (PROMPT_CHAINS * SUBLANES) == 0 and tile_p // PROMPT_CHAINS >= POOL_PAD
    assert tokens_s % tile_s == 0 and tile_s % (SAMPLE_CHAINS * steps * SUBLANES) == 0
    assert len(vectors) == N_VECTORS and len(matrices) == N_MATRICES
    n_p, n_s = tokens_p // tile_p, tokens_s // tile_s
    tiles_per_seq = seq_len // tile_p
    n_seq = tokens_p // seq_len
    seqs = tile_s // steps

    tile_p_of = lambda i: jnp.clip(i - 1, 0, n_p - 1)
    tile_s_of = lambda i: jnp.clip(i - 1 - n_p, 0, n_s - 1)
    pool_state_p = pl.BlockSpec((POOL_HIST, n_seq, POOL_WIDTH), lambda i: (0, 0, 0))
    conv_state_p = pl.BlockSpec((1, CONV_HIST, CONV_WIDTH),
                                lambda i: (tile_p_of(i) // tiles_per_seq, 0, 0))
    row_s_in = lambda width: pl.BlockSpec((seqs, steps, width), lambda i: (tile_s_of(i), 0, 0),
                                          pipeline_mode=pl.Buffered(1))
    pool_s = pl.BlockSpec((POOL_HIST, seqs, POOL_WIDTH), lambda i: (0, tile_s_of(i), 0),
                          pipeline_mode=pl.Buffered(1))
    conv_s = pl.BlockSpec((seqs, CONV_HIST, CONV_WIDTH), lambda i: (tile_s_of(i), 0, 0),
                          pipeline_mode=pl.Buffered(1))

    f32 = jnp.float32
    out_shape = [
        jax.ShapeDtypeStruct((tokens_p, D_MODEL), f32),
        jax.ShapeDtypeStruct((POOL_HIST, n_seq, POOL_WIDTH), f32),
        jax.ShapeDtypeStruct((n_seq, CONV_HIST, CONV_WIDTH), f32),
        jax.ShapeDtypeStruct((tokens_s // steps, steps, D_MODEL), f32),
        jax.ShapeDtypeStruct((POOL_HIST, tokens_s // steps, POOL_WIDTH), f32),
        jax.ShapeDtypeStruct((tokens_s // steps, CONV_HIST, CONV_WIDTH), f32),
    ]
    out_specs = [pl.BlockSpec(memory_space=pl.ANY), pool_state_p, conv_state_p,
                 pl.BlockSpec((seqs, steps, D_MODEL), lambda i: (tile_s_of(i), 0, 0),
                              pipeline_mode=pl.Buffered(1)),
                 pool_s, conv_s]
    scratch = [pltpu.VMEM((POOL_PAD, POOL_WIDTH), f32), pltpu.VMEM((CONV_PAD, CONV_WIDTH), f32),
               pltpu.VMEM((2, tile_p, D_MODEL), f32), pltpu.VMEM((2, tile_p, D_PLE), f32),
               pltpu.VMEM((2, tile_p, D_MODEL), f32),
               pltpu.SemaphoreType.DMA((2, 2)), pltpu.SemaphoreType.DMA((2,))]
    scratch += [pltpu.VMEM(w.shape, jnp.bfloat16) for w in matrices]

    kernel = functools.partial(_layer_kernel, n_prompt_tiles=n_p, tile_prompt=tile_p,
                               tiles_per_seq=tiles_per_seq, seqs=seqs, steps=steps)
    return pl.pallas_call(
        kernel,
        grid=(1 + n_p + n_s,),
        in_specs=([pl.BlockSpec(memory_space=pl.ANY), pl.BlockSpec(memory_space=pl.ANY),
                   row_s_in(D_MODEL), row_s_in(D_PLE),
                   pool_s, conv_s]
                  + [_resident(v.shape) for v in vectors]
                  + [pl.BlockSpec(memory_space=pl.ANY)] * N_MATRICES),
        out_specs=out_specs,
        out_shape=out_shape,
        scratch_shapes=scratch,
        compiler_params=pltpu.CompilerParams(
            dimension_semantics=("arbitrary",),
            vmem_limit_bytes=VMEM_LIMIT_BYTES),
        name="hybrid_layer",
    )(x_p, p_p, x_s, p_s, pool_hist, conv_hist, *vectors, *matrices)


def kernel(x_prompt, x_sample, state_pool, state_conv, p_prompt, p_sample, g_mix, w_in, w_pool_group, pool_scale, w_pool_up, w_conv, w_conv_out, w_o, g_ffn, w_ffn_in, w_ffn_out, g_ple, w_ple, w_ple_gate, g_final):
    depth = g_mix.shape[0]
    assert depth == 1, "single-layer step"
    vec = lambda a: a.reshape(1, -1)
    vectors = (vec(g_mix[0]), vec(pool_scale[0]), vec(w_conv[0]), vec(g_ffn[0]), vec(g_ple[0]),
               vec(g_final))
    matrices = (w_in[0], w_pool_group[0].reshape(POOL_WIDTH, POOL_GROUP), w_pool_up[0],
                w_conv_out[0], w_o[0], w_ffn_in[0], w_ffn_out[0], w_ple[0], w_ple_gate[0])

    batch, seq, _ = x_prompt.shape
    dec_seq = x_sample.shape[1]

    y_p, pool_state_p, conv_state_p, y_s, pool_state_s, conv_state_s = _run_layer(
        x_prompt.reshape(batch * seq, D_MODEL), p_prompt[0].reshape(batch * seq, D_PLE),
        x_sample, p_sample[0],
        jnp.swapaxes(state_pool[0], 0, 1), state_conv[0],
        vectors, matrices, seq_len=seq, steps=dec_seq)

    return (y_p.reshape(batch, seq, D_MODEL), y_s,
            jnp.swapaxes(pool_state_p, 0, 1)[None], conv_state_p[None],
            jnp.swapaxes(pool_state_s, 0, 1)[None], conv_state_s[None])
```

```python
import functools

import jax
import jax.numpy as jnp
from jax.experimental import pallas as pl
from jax.experimental.pallas import tpu as pltpu

D_MODEL = 1024
POOL_WINDOWS = (2, 4, 8, 16)
POOL_GROUP = 128
POOL_WIDTH = 512
POOL_HIST = 15
CONV_WIDTH = 512
CONV_K = 3
CONV_HIST = 2
D_FF = 2816
D_PLE = 256
EPS = 1e-6

OFF_U = 0
OFF_B = OFF_U + POOL_WIDTH
OFF_C = OFF_B + CONV_WIDTH
OFF_H = OFF_C + CONV_WIDTH
OFF_GP = OFF_H + CONV_WIDTH
OFF_GC = OFF_GP + D_MODEL

V7X_VMEM_BYTES = 64 * 1024 * 1024
V7X_MXU_WIDTH = 256
SUBLANES = 8
DMA_PRIORITIES = 2

POOL_PAD = 16
CONV_PAD = 8
FF_CHUNK = V7X_MXU_WIDTH
TILE_PROMPT = 512
TILE_SAMPLE = 256
PROMPT_CHAINS = 2
SAMPLE_CHAINS = 2
PROMPT_LAG = 1
SAMPLE_LAG = 1
PREP_ROWS = 256
PREP_COLS = 1024
PREP_BUFS = 8
VMEM_RESERVE_BYTES = 4 * 1024 * 1024
VMEM_LIMIT_BYTES = V7X_VMEM_BYTES - VMEM_RESERVE_BYTES

N_VECTORS = 6
N_MATRICES = 9
N_INPUTS = 6
N_OUTPUTS = 6


def _rmsnorm(x, g):
    ms = jnp.mean(x * x, axis=-1, keepdims=True)
    return x * jax.lax.rsqrt(ms + EPS) * g


def _dot(a, b):
    return jnp.dot(a, b, preferred_element_type=jnp.float32)


def _conv_tap(w_conv_ref, k):
    return w_conv_ref[:, k * CONV_WIDTH:(k + 1) * CONV_WIDTH]


def _prep_weights(w_hbm, w_vmem, stage_ref, sem):
    chunks = []
    for src, dst in zip(w_hbm, w_vmem):
        n_rows, n_cols = dst.shape
        for r0 in range(0, n_rows, PREP_ROWS):
            for c0 in range(0, n_cols, PREP_COLS):
                chunks.append((src, dst, r0, min(PREP_ROWS, n_rows - r0),
                               c0, min(PREP_COLS, n_cols - c0)))

    def chunk_copy(k):
        src, _, r0, nr, c0, nc = chunks[k]
        slot = k % PREP_BUFS
        return pltpu.make_async_copy(src.at[r0:r0 + nr, c0:c0 + nc],
                                     stage_ref.at[slot, :nr, :nc], sem.at[slot])

    for k in range(min(PREP_BUFS, len(chunks))):
        chunk_copy(k).start(priority=k % DMA_PRIORITIES)
    for k, (_, dst, r0, nr, c0, nc) in enumerate(chunks):
        chunk_copy(k).wait()
        dst[r0:r0 + nr, c0:c0 + nc] = stage_ref[k % PREP_BUFS, :nr, :nc].astype(jnp.bfloat16)
        if k + PREP_BUFS < len(chunks):
            chunk_copy(k + PREP_BUFS).start(priority=(k + PREP_BUFS) % DMA_PRIORITIES)


def _prompt_in_copies(t, slot, x_hbm, p_hbm, x_buf, p_buf, sem, tile):
    rows = pl.ds(t * tile, tile)
    return (pltpu.make_async_copy(x_hbm.at[rows, :], x_buf.at[slot], sem.at[slot, 0]),
            pltpu.make_async_copy(p_hbm.at[rows, :], p_buf.at[slot], sem.at[slot, 1]))


def _prompt_out_copy(t, slot, y_buf, y_hbm, sem, tile):
    return pltpu.make_async_copy(y_buf.at[slot], y_hbm.at[pl.ds(t * tile, tile), :], sem.at[slot])


def _layer_stages(x, p, vectors, w_vmem, act_ref, pool_conv):
    g_mix_ref, pool_scale_ref, _, g_ffn_ref, g_ple_ref, g_final_ref = vectors
    (w_in_ref, w_group_ref, w_pool_up_ref, w_conv_out_ref, w_o_ref, w_ffn_in_ref,
     w_ffn_out_ref, w_ple_ref, w_ple_gate_ref) = w_vmem
    bf16 = jnp.bfloat16

    hn = _rmsnorm(x, g_mix_ref[...]).astype(bf16)

    def proj(off, width):
        return _dot(hn, w_in_ref[:, off:off + width])

    u = proj(OFF_U, POOL_WIDTH)
    yield
    v = proj(OFF_C, CONV_WIDTH) * proj(OFF_H, CONV_WIDTH)
    yield
    pooled, yconv = pool_conv(u, v)
    conv_out = (proj(OFF_B, CONV_WIDTH) * yconv).astype(bf16)
    yield
    gate_pool = jax.nn.sigmoid(proj(OFF_GP, D_MODEL))
    yield
    gate_conv = jax.nn.sigmoid(proj(OFF_GC, D_MODEL))
    yield
    mixed = [_dot(d, w_group_ref[gi * POOL_GROUP:(gi + 1) * POOL_GROUP, :])
             for gi, d in enumerate(pooled)]
    pool_out = (jnp.concatenate(mixed, axis=-1) * pool_scale_ref[...]).astype(bf16)
    yield
    gated_pool = gate_pool * _dot(pool_out, w_pool_up_ref[...])
    yield
    merged = (gated_pool + gate_conv * _dot(conv_out, w_conv_out_ref[...])).astype(bf16)
    yield
    x = x + _dot(merged, w_o_ref[...])
    yield

    hn = _rmsnorm(x, g_ffn_ref[...]).astype(bf16)
    for c0 in range(0, D_FF, FF_CHUNK):
        gate = _dot(hn, w_ffn_in_ref[:, c0:c0 + FF_CHUNK])
        up = _dot(hn, w_ffn_in_ref[:, D_FF + c0:D_FF + c0 + FF_CHUNK])
        act_ref[:, c0:c0 + FF_CHUNK] = (jax.nn.silu(gate) * up).astype(bf16)
        yield
    x = x + _dot(act_ref[...], w_ffn_out_ref[...])
    yield

    hn = _rmsnorm(x, g_ple_ref[...]).astype(bf16)
    ple_gate = jax.nn.sigmoid(_dot(hn, w_ple_gate_ref[...]))
    yield
    x = x + ple_gate * _dot(p.astype(bf16), w_ple_ref[...])
    return _rmsnorm(x, g_final_ref[...])


def _run_interleaved(chains, lag):
    results = [None] * len(chains)
    live = list(range(len(chains)))
    rounds = 0
    while live:
        for i in list(live):
            if rounds < i * lag:
                continue
            try:
                next(chains[i])
            except StopIteration as done:
                results[i] = done.value
                live.remove(i)
        rounds += 1
    return results


def _prompt_tile(t, x_ref, p_ref, vectors, w_vmem, y_ref, pool_state_ref, conv_state_ref,
                 u_carry_ref, v_carry_ref, ext_ref, vext_ref, act_ref, *, tile, tiles_per_seq):
    wc = functools.partial(_conv_tap, vectors[2])
    rows = tile // PROMPT_CHAINS

    @pl.when(t % tiles_per_seq == 0)
    def _():
        u_carry_ref[...] = jnp.zeros((POOL_PAD, POOL_WIDTH), jnp.float32)
        v_carry_ref[...] = jnp.zeros((CONV_PAD, CONV_WIDTH), jnp.float32)

    ext_ref[:POOL_PAD, :] = u_carry_ref[...]
    vext_ref[:CONV_PAD, :] = v_carry_ref[...]

    def pool_conv_at(r0):
        def pool_conv(u, v):
            ext_ref[POOL_PAD + r0:POOL_PAD + r0 + rows, :] = u
            vext_ref[CONV_PAD + r0:CONV_PAD + r0 + rows, :] = v

            def ext_rows(back, c0, c1):
                return ext_ref[POOL_PAD + r0 - back:POOL_PAD + r0 - back + rows, c0:c1]

            def vext_rows(back):
                return vext_ref[CONV_PAD + r0 - back:CONV_PAD + r0 - back + rows, :]

            pos = (t % tiles_per_seq) * tile + r0 + jax.lax.broadcasted_iota(
                jnp.int32, (rows, POOL_GROUP), 0)
            valid = (pos + 1).astype(jnp.float32)
            pooled = []
            for gi, w in enumerate(POOL_WINDOWS):
                c0, c1 = gi * POOL_GROUP, (gi + 1) * POOL_GROUP
                win = ext_rows(0, c0, c1)
                for back in range(1, w):
                    win = win + ext_rows(back, c0, c1)
                mean = win / jnp.minimum(valid, float(w))
                pooled.append((mean - ext_rows(0, c0, c1)).astype(jnp.bfloat16))

            yconv = wc(0) * vext_rows(2) + wc(1) * vext_rows(1) + wc(2) * vext_rows(0)
            return pooled, yconv
        return pool_conv

    starts = range(0, tile, rows)
    ys = _run_interleaved([
        _layer_stages(x_ref[r0:r0 + rows, :], p_ref[r0:r0 + rows, :], vectors, w_vmem,
                      act_ref.at[r0:r0 + rows, :], pool_conv_at(r0)) for r0 in starts],
        PROMPT_LAG)
    for r0, y in zip(starts, ys):
        y_ref[r0:r0 + rows, :] = y

    u_tail = ext_ref[tile:tile + POOL_PAD, :]
    v_tail = vext_ref[tile:tile + CONV_PAD, :]
    u_carry_ref[...] = u_tail
    v_carry_ref[...] = v_tail
    seq = t // tiles_per_seq
    pool_state_ref[:, pl.ds(seq, 1), :] = u_tail[POOL_PAD - POOL_HIST:][:, None, :]
    conv_state_ref[0] = v_tail[CONV_PAD - CONV_HIST:]


def _sample_tile(x_ref, p_ref, pool_hist_ref, conv_hist_ref, vectors, w_vmem,
                 y_ref, pool_new_ref, conv_new_ref, act_ref, *, seqs, steps):
    wc = functools.partial(_conv_tap, vectors[2])
    n = seqs // SAMPLE_CHAINS

    def chain(q0):
        qs = slice(q0, q0 + n)
        by_step = lambda ref: jnp.concatenate([ref[qs, s, :] for s in range(steps)], axis=0)
        slab = lambda a, s: a[s * n:(s + 1) * n]

        def pool_conv(u, v):
            line = ([pool_hist_ref[s, qs, :] for s in range(POOL_HIST)]
                    + [slab(u, s) for s in range(steps)])
            pooled = []
            for gi, w in enumerate(POOL_WINDOWS):
                c0, c1 = gi * POOL_GROUP, (gi + 1) * POOL_GROUP
                per_step = []
                for s in range(steps):
                    now = POOL_HIST + s
                    win = line[now][:, c0:c1]
                    for back in range(1, w):
                        win = win + line[now - back][:, c0:c1]
                    per_step.append(win * (1.0 / w) - line[now][:, c0:c1])
                pooled.append(jnp.concatenate(per_step, axis=0).astype(jnp.bfloat16))
            for s in range(POOL_HIST):
                pool_new_ref[s, qs, :] = line[steps + s]

            vline = ([conv_hist_ref[qs, s, :] for s in range(CONV_HIST)]
                     + [slab(v, s) for s in range(steps)])
            yconv = jnp.concatenate(
                [wc(0) * vline[s] + wc(1) * vline[s + 1] + wc(2) * vline[s + 2]
                 for s in range(steps)], axis=0)
            for s in range(CONV_HIST):
                conv_new_ref[qs, s, :] = vline[steps + s]
            return pooled, yconv

        return _layer_stages(by_step(x_ref), by_step(p_ref), vectors, w_vmem,
                             act_ref.at[q0 * steps:(q0 + n) * steps, :], pool_conv)

    starts = range(0, seqs, n)
    for q0, y in zip(starts, _run_interleaved([chain(q0) for q0 in starts], SAMPLE_LAG)):
        for s in range(steps):
            y_ref[q0:q0 + n, s, :] = y[s * n:(s + 1) * n]


def _layer_kernel(*refs, n_prompt_tiles, tile_prompt, tiles_per_seq, seqs, steps):
    xp_ref, pp_ref, xs_ref, ps_ref, pool_hist_ref, conv_hist_ref = refs[:N_INPUTS]
    refs = refs[N_INPUTS:]
    vectors, refs = refs[:N_VECTORS], refs[N_VECTORS:]
    w_hbm, refs = refs[:N_MATRICES], refs[N_MATRICES:]
    (yp_ref, pool_state_ref, conv_state_ref, ys_ref, pool_new_ref,
     conv_new_ref) = refs[:N_OUTPUTS]
    refs = refs[N_OUTPUTS:]
    u_carry_ref, v_carry_ref, x_buf, p_buf, y_buf, in_sem, out_sem = refs[:7]
    w_vmem = refs[7:]
    f32, bf16 = jnp.float32, jnp.bfloat16
    in_copies = functools.partial(_prompt_in_copies, x_hbm=xp_ref, p_hbm=pp_ref, x_buf=x_buf,
                                  p_buf=p_buf, sem=in_sem, tile=tile_prompt)
    out_copy = functools.partial(_prompt_out_copy, y_buf=y_buf, y_hbm=yp_ref, sem=out_sem,
                                 tile=tile_prompt)

    step = pl.program_id(0)

    @pl.when(step == 0)
    def _():
        for c in in_copies(0, 0):
            c.start()
        pl.run_scoped(
            functools.partial(_prep_weights, w_hbm, w_vmem),
            pltpu.VMEM((PREP_BUFS, PREP_ROWS, PREP_COLS), f32),
            pltpu.SemaphoreType.DMA((PREP_BUFS,)))

    @pl.when(jnp.logical_and(step >= 1, step <= n_prompt_tiles))
    def _():
        t = step - 1
        slot = t % 2
        for c in in_copies(t, slot):
            c.wait()

        @pl.when(t + 1 < n_prompt_tiles)
        def _():
            for c in in_copies(t + 1, 1 - slot):
                c.start()

        @pl.when(t >= 2)
        def _():
            out_copy(t - 2, slot).wait()

        pl.run_scoped(
            functools.partial(_prompt_tile, t, x_buf.at[slot], p_buf.at[slot], vectors, w_vmem,
                              y_buf.at[slot], pool_state_ref, conv_state_ref, u_carry_ref,
                              v_carry_ref, tile=tile_prompt, tiles_per_seq=tiles_per_seq),
            pltpu.VMEM((POOL_PAD + tile_prompt, POOL_WIDTH), f32),
            pltpu.VMEM((CONV_PAD + tile_prompt, CONV_WIDTH), f32),
            pltpu.VMEM((tile_prompt, D_FF), bf16))
        out_copy(t, slot).start(priority=1)

        @pl.when(t == n_prompt_tiles - 1)
        def _():
            out_copy(t - 1, 1 - slot).wait()
            out_copy(t, slot).wait()

    @pl.when(step > n_prompt_tiles)
    def _():
        pl.run_scoped(
            functools.partial(_sample_tile, xs_ref, ps_ref, pool_hist_ref, conv_hist_ref, vectors,
                              w_vmem, ys_ref, pool_new_ref, conv_new_ref, seqs=seqs, steps=steps),
            pltpu.VMEM((seqs * steps, D_FF), bf16))


def _resident(shape):
    zeros = (0,) * len(shape)
    return pl.BlockSpec(shape, lambda i: zeros, pipeline_mode=pl.Buffered(1))


def _run_layer(x_p, p_p, x_s, p_s, pool_hist, conv_hist, vectors, matrices, *, seq_len, steps):
    tokens_p, tokens_s = x_p.shape[0], x_s.shape[0] * steps
    assert x_s.shape[1] == steps
    tile_p, tile_s = TILE_PROMPT, TILE_SAMPLE
    assert tokens_p % tile_p == 0 and seq_len % tile_p == 0 and tokens_p >= 2 * tile_p
    assert tile_p % (PROMPT_CHAINS * SUBLANES) == 0 and tile_p // PROMPT_CHAINS >= POOL_PAD
    assert tokens_s % tile_s == 0 and tile_s % (SAMPLE_CHAINS * steps * SUBLANES) == 0
    assert len(vectors) == N_VECTORS and len(matrices) == N_MATRICES
    n_p, n_s = tokens_p // tile_p, tokens_s // tile_s
    tiles_per_seq = seq_len // tile_p
    n_seq = tokens_p // seq_len
    seqs = tile_s // steps

    tile_p_of = lambda i: jnp.clip(i - 1, 0, n_p - 1)
    tile_s_of = lambda i: jnp.clip(i - 1 - n_p, 0, n_s - 1)
    pool_state_p = pl.BlockSpec((POOL_HIST, n_seq, POOL_WIDTH), lambda i: (0, 0, 0))
    conv_state_p = pl.BlockSpec((1, CONV_HIST, CONV_WIDTH),
                                lambda i: (tile_p_of(i) // tiles_per_seq, 0, 0))
    row_s_in = lambda width: pl.BlockSpec((seqs, steps, width), lambda i: (tile_s_of(i), 0, 0),
                                          pipeline_mode=pl.Buffered(1))
    pool_s = pl.BlockSpec((POOL_HIST, seqs, POOL_WIDTH), lambda i: (0, tile_s_of(i), 0),
                          pipeline_mode=pl.Buffered(1))
    conv_s = pl.BlockSpec((seqs, CONV_HIST, CONV_WIDTH), lambda i: (tile_s_of(i), 0, 0),
                          pipeline_mode=pl.Buffered(1))

    f32 = jnp.float32
    out_shape = [
        jax.ShapeDtypeStruct((tokens_p, D_MODEL), f32),
        jax.ShapeDtypeStruct((POOL_HIST, n_seq, POOL_WIDTH), f32),
        jax.ShapeDtypeStruct((n_seq, CONV_HIST, CONV_WIDTH), f32),
        jax.ShapeDtypeStruct((tokens_s // steps, steps, D_MODEL), f32),
        jax.ShapeDtypeStruct((POOL_HIST, tokens_s // steps, POOL_WIDTH), f32),
        jax.ShapeDtypeStruct((tokens_s // steps, CONV_HIST, CONV_WIDTH), f32),
    ]
    out_specs = [pl.BlockSpec(memory_space=pl.ANY), pool_state_p, conv_state_p,
                 pl.BlockSpec((seqs, steps, D_MODEL), lambda i: (tile_s_of(i), 0, 0),
                              pipeline_mode=pl.Buffered(1)),
                 pool_s, conv_s]
    scratch = [pltpu.VMEM((POOL_PAD, POOL_WIDTH), f32), pltpu.VMEM((CONV_PAD, CONV_WIDTH), f32),
               pltpu.VMEM((2, tile_p, D_MODEL), f32), pltpu.VMEM((2, tile_p, D_PLE), f32),
               pltpu.VMEM((2, tile_p, D_MODEL), f32),
               pltpu.SemaphoreType.DMA((2, 2)), pltpu.SemaphoreType.DMA((2,))]
    scratch += [pltpu.VMEM(w.shape, jnp.bfloat16) for w in matrices]

    kernel = functools.partial(_layer_kernel, n_prompt_tiles=n_p, tile_prompt=tile_p,
                               tiles_per_seq=tiles_per_seq, seqs=seqs, steps=steps)
    return pl.pallas_call(
        kernel,
        grid=(1 + n_p + n_s,),
        in_specs=([pl.BlockSpec(memory_space=pl.ANY), pl.BlockSpec(memory_space=pl.ANY),
                   row_s_in(D_MODEL), row_s_in(D_PLE),
                   pool_s, conv_s]
                  + [_resident(v.shape) for v in vectors]
                  + [pl.BlockSpec(memory_space=pl.ANY)] * N_MATRICES),
        out_specs=out_specs,
        out_shape=out_shape,
        scratch_shapes=scratch,
        compiler_params=pltpu.CompilerParams(
            dimension_semantics=("arbitrary",),
            vmem_limit_bytes=VMEM_LIMIT_BYTES),
        name="hybrid_layer",
    )(x_p, p_p, x_s, p_s, pool_hist, conv_hist, *vectors, *matrices)


def kernel(x_prompt, x_sample, state_pool, state_conv, p_prompt, p_sample, g_mix, w_in, w_pool_group, pool_scale, w_pool_up, w_conv, w_conv_out, w_o, g_ffn, w_ffn_in, w_ffn_out, g_ple, w_ple, w_ple_gate, g_final):
    depth = g_mix.shape[0]
    assert depth == 1, "single-layer step"
    vec = lambda a: a.reshape(1, -1)
    vectors = (vec(g_mix[0]), vec(pool_scale[0]), vec(w_conv[0]), vec(g_ffn[0]), vec(g_ple[0]),
               vec(g_final))
    matrices = (w_in[0], w_pool_group[0].reshape(POOL_WIDTH, POOL_GROUP), w_pool_up[0],
                w_conv_out[0], w_o[0], w_ffn_in[0], w_ffn_out[0], w_ple[0], w_ple_gate[0])

    batch, seq, _ = x_prompt.shape
    dec_seq = x_sample.shape[1]

    y_p, pool_state_p, conv_state_p, y_s, pool_state_s, conv_state_s = _run_layer(
        x_prompt.reshape(batch * seq, D_MODEL), p_prompt[0].reshape(batch * seq, D_PLE),
        x_sample, p_sample[0],
        jnp.swapaxes(state_pool[0], 0, 1), state_conv[0],
        vectors, matrices, seq_len=seq, steps=dec_seq)

    return (y_p.reshape(batch, seq, D_MODEL), y_s,
            jnp.swapaxes(pool_state_p, 0, 1)[None], conv_state_p[None],
            jnp.swapaxes(pool_state_s, 0, 1)[None], conv_state_s[None])
```
